```python
import math
import jax, jax.numpy as jnp
from jax import lax
import numpy as np

D_MODEL = 1024
BATCH = 1
SEQ = 16384
DEPTH = 4

GRID_W = 64
CTX_LEN = 256
HEAD_DIM = 64
A_HEADS = 8
A_KV_HEADS = 2
B_HEADS = 8
NB_KH_MAX = 8
NB_KW = 16
C_HEADS = 16
C_KV_HEADS = 2
C_WINDOW = 128
Q_BLOCK = 128
D_FF = 2816
CONV_W = 3
ROPE_THETA = 10000.0
LN_EPS = 1e-5
RMS_EPS = 1e-6
NEG = -1e30
DN_ALPHA = (2 * DEPTH) ** 0.25
DN_BETA = (8 * DEPTH) ** -0.25
N_EVEN = (DEPTH + 1) // 2
N_ODD = DEPTH // 2
A_Q = A_HEADS * HEAD_DIM
A_KV = A_KV_HEADS * HEAD_DIM
B_W = B_HEADS * HEAD_DIM
EVEN_IN = A_Q + 2 * A_KV + 3 * B_W
EVEN_MIX = A_Q + B_W
C_Q = C_HEADS * HEAD_DIM
C_KV = C_KV_HEADS * HEAD_DIM
ODD_IN = C_Q + 2 * C_KV
ODD_MIX = C_Q

kernel_name = 'hybrid_dit_gqa_natten_swa_convffn'


def layer_norm(x, g, b):
    xf = x.astype(jnp.float32)
    mu = jnp.mean(xf, -1, keepdims=True)
    var = jnp.mean(jnp.square(xf - mu), -1, keepdims=True)
    return ((xf - mu) * lax.rsqrt(var + LN_EPS) * g + b).astype(x.dtype)


def rms_norm(x, g):
    xf = x.astype(jnp.float32)
    return (xf * lax.rsqrt(jnp.mean(xf * xf, -1, keepdims=True) + RMS_EPS) * g).astype(x.dtype)


def axial_rope_tables(n_tokens):
    t = jnp.arange(n_tokens, dtype=jnp.int32)
    row = (t // GRID_W).astype(jnp.float32)
    col = (t % GRID_W).astype(jnp.float32)
    half = HEAD_DIM // 2
    inv_freq = ROPE_THETA ** (-jnp.arange(0, half, 2, dtype=jnp.float32) / half)
    ang_r = row[:, None] * inv_freq
    ang_c = col[:, None] * inv_freq
    ang = jnp.concatenate([ang_r, ang_r, ang_c, ang_c], -1)
    return jnp.cos(ang), jnp.sin(ang)


def apply_rope(x, cos, sin):
    x1, x2, x3, x4 = jnp.split(x, 4, axis=-1)
    rot = jnp.concatenate([-x2, x1, -x4, x3], -1)
    return (x * cos[None, :, None, :] + rot * sin[None, :, None, :]).astype(x.dtype)


def heads(z, n):
    return z.reshape(z.shape[0], z.shape[1], n, HEAD_DIM)


def modulate(x, shift, scale):
    return x * (1 + scale) + shift


def gqa_scores(q, k):
    return jnp.einsum('bqkgd,btkd->bkgqt', q, k).astype(jnp.float32)


def gqa_out(p, v):
    return jnp.einsum('bkgqt,btkd->bqkgd', p.astype(v.dtype), v)


def global_gqa(q_lat, k_lat, v_lat, q_ctx, k_ctx, v_ctx, ctx_out):
    B, S, H, d = q_lat.shape
    KV = k_lat.shape[2]
    G = H // KV
    scale = d ** -0.5
    k_all = jnp.concatenate([k_ctx, k_lat], 1)
    v_all = jnp.concatenate([v_ctx, v_lat], 1)
    qb = (q_lat * scale).reshape(B, S // Q_BLOCK, Q_BLOCK, KV, G, d).swapaxes(0, 1)

    def block(q):
        p = jax.nn.softmax(gqa_scores(q, k_all), axis=-1)
        return gqa_out(p, v_all)

    o = lax.map(block, qb)
    y_lat = o.swapaxes(0, 1).reshape(B, S, H * d)
    y_ctx = None
    if ctx_out:
        qc = (q_ctx * scale).reshape(B, q_ctx.shape[1], KV, G, d)
        p = jax.nn.softmax(gqa_scores(qc, k_ctx), axis=-1)
        y_ctx = gqa_out(p, v_ctx).reshape(B, q_ctx.shape[1], H * d)
    return y_lat, y_ctx


def neighbourhood_attention(q_lat, k_lat, v_lat, q_ctx, k_ctx, v_ctx, rpb, ctx_out):
    B, S, H, d = q_lat.shape
    rows = S // GRID_W
    kh = min(NB_KH_MAX, rows)
    rb = Q_BLOCK // GRID_W
    scale = d ** -0.5
    col = jnp.arange(GRID_W)
    cs = jnp.clip(col - NB_KW // 2, 0, GRID_W - NB_KW)
    kcol = cs[:, None] + jnp.arange(NB_KW)
    dcol = kcol - col[:, None]
    qb = (q_lat * scale).reshape(B, rows // rb, rb, GRID_W, H, d).swapaxes(0, 1)
    n_nb = kh * NB_KW

    def block(args):
        qblk, blk = args
        row = blk * rb + jnp.arange(rb)
        rs = jnp.clip(row - kh // 2, 0, rows - kh)
        krow = rs[:, None] + jnp.arange(kh)
        drow = krow - row[:, None]
        idx = krow[:, None, :, None] * GRID_W + kcol[None, :, None, :]
        kn = k_lat[:, idx]
        vn = v_lat[:, idx]
        bias = rpb[:, drow[:, None, :, None] + NB_KH_MAX - 1, dcol[None, :, None, :] + NB_KW - 1]
        s_nb = jnp.einsum('brwhd,brwyxhd->bhrwyx', qblk, kn).astype(jnp.float32) + bias[None]
        s_nb = s_nb.reshape(B, H, rb, GRID_W, n_nb)
        s_ctx = jnp.einsum('brwhd,bthd->bhrwt', qblk, k_ctx).astype(jnp.float32)
        p = jax.nn.softmax(jnp.concatenate([s_nb, s_ctx], -1), axis=-1)
        p_nb = p[..., :n_nb].reshape(B, H, rb, GRID_W, kh, NB_KW).astype(vn.dtype)
        p_ctx = p[..., n_nb:].astype(v_ctx.dtype)
        return (jnp.einsum('bhrwyx,brwyxhd->brwhd', p_nb, vn)
                + jnp.einsum('bhrwt,bthd->brwhd', p_ctx, v_ctx))

    o = lax.map(block, (qb, jnp.arange(rows // rb)))
    y_lat = o.swapaxes(0, 1).reshape(B, S, H * d)
    y_ctx = None
    if ctx_out:
        s = jnp.einsum('bqhd,bthd->bhqt', q_ctx * scale, k_ctx).astype(jnp.float32)
        p = jax.nn.softmax(s, axis=-1).astype(v_ctx.dtype)
        y_ctx = jnp.einsum('bhqt,bthd->bqhd', p, v_ctx).reshape(B, q_ctx.shape[1], H * d)
    return y_lat, y_ctx


def window_gqa_sink(q_lat, k_lat, v_lat, q_ctx, k_ctx, v_ctx, sink, ctx_out):
    B, S, H, d = q_lat.shape
    KV = k_lat.shape[2]
    G = H // KV
    C = k_ctx.shape[1]
    scale = d ** -0.5
    span = Q_BLOCK + 2 * C_WINDOW
    pad = ((0, 0), (C_WINDOW, C_WINDOW), (0, 0), (0, 0))
    kp = jnp.pad(k_lat, pad)
    vp = jnp.pad(v_lat, pad)
    sink5 = sink.astype(jnp.float32).reshape(1, KV, G, 1, 1)
    qb = (q_lat * scale).reshape(B, S // Q_BLOCK, Q_BLOCK, KV, G, d).swapaxes(0, 1)
    qpos = jnp.arange(Q_BLOCK)
    kpos = jnp.arange(span) - C_WINDOW
    sink_q = jnp.broadcast_to(sink5, (B, KV, G, Q_BLOCK, 1))

    def block(args):
        q, blk = args
        start = blk * Q_BLOCK
        kb = lax.dynamic_slice_in_dim(kp, start, span, axis=1)
        vb = lax.dynamic_slice_in_dim(vp, start, span, axis=1)
        abs_k = start + kpos
        valid = ((jnp.abs(kpos[None, :] - qpos[:, None]) <= C_WINDOW)
                 & ((abs_k >= 0) & (abs_k < S))[None, :])
        s_loc = jnp.where(valid, gqa_scores(q, kb), NEG)
        s_ctx = gqa_scores(q, k_ctx)
        p = jax.nn.softmax(jnp.concatenate([s_loc, s_ctx, sink_q], -1), axis=-1)
        return gqa_out(p[..., :span], vb) + gqa_out(p[..., span:span + C], v_ctx)

    o = lax.map(block, (qb, jnp.arange(S // Q_BLOCK)))
    y_lat = o.swapaxes(0, 1).reshape(B, S, H * d)
    y_ctx = None
    if ctx_out:
        qc = (q_ctx * scale).reshape(B, C, KV, G, d)
        s = jnp.concatenate([gqa_scores(qc, k_ctx), jnp.broadcast_to(sink5, (B, KV, G, C, 1))], -1)
        p = jax.nn.softmax(s, axis=-1)
        y_ctx = gqa_out(p[..., :C], v_ctx).reshape(B, C, H * d)
    return y_lat, y_ctx


def even_project(h, w_in, q_gain, k_gain):
    qa, ka, va, qb, kb, vb = jnp.split(
        h @ w_in, [A_Q, A_Q + A_KV, A_Q + 2 * A_KV, A_Q + 2 * A_KV + B_W, A_Q + 2 * A_KV + 2 * B_W], axis=-1)
    return (rms_norm(heads(qa, A_HEADS), q_gain), rms_norm(heads(ka, A_KV_HEADS), k_gain),
            heads(va, A_KV_HEADS), heads(qb, B_HEADS), heads(kb, B_HEADS), heads(vb, B_HEADS))


def even_mixer(h_lat, h_ctx, w_in, w_out, q_gain, k_gain, rpb, cos, sin, ctx_out):
    qa, ka, va, qb, kb, vb = even_project(h_lat, w_in, q_gain, k_gain)
    qa_c, ka_c, va_c, qb_c, kb_c, vb_c = even_project(h_ctx, w_in, q_gain, k_gain)
    qa = apply_rope(qa, cos, sin)
    ka = apply_rope(ka, cos, sin)
    ya, ya_c = global_gqa(qa, ka, va, qa_c, ka_c, va_c, ctx_out)
    yb, yb_c = neighbourhood_attention(qb, kb, vb, qb_c, kb_c, vb_c, rpb, ctx_out)
    y_lat = jnp.concatenate([ya, yb], -1) @ w_out
    y_ctx = (jnp.concatenate([ya_c, yb_c], -1) @ w_out) if ctx_out else None
    return y_lat, y_ctx


def odd_project(h, w_in):
    q, k, v = jnp.split(h @ w_in, [C_Q, C_Q + C_KV], axis=-1)
    return heads(q, C_HEADS), heads(k, C_KV_HEADS), heads(v, C_KV_HEADS)


def odd_mixer(h_lat, h_ctx, w_in, w_out, sink, cos, sin, ctx_out):
    q, k, v = odd_project(h_lat, w_in)
    qc, kc, vc = odd_project(h_ctx, w_in)
    q = apply_rope(q, cos, sin)
    k = apply_rope(k, cos, sin)
    y, yc = window_gqa_sink(q, k, v, qc, kc, vc, sink, ctx_out)
    return y @ w_out, ((yc @ w_out) if ctx_out else None)


def conv_ffn(h, w_up, conv_w, conv_b, w_down):
    u = h @ w_up
    L = u.shape[1]
    r = CONV_W // 2
    up = jnp.pad(u, ((0, 0), (r, r), (0, 0)))
    acc = conv_b
    for j in range(CONV_W):
        acc = acc + up[:, j:j + L] * conv_w[j]
    a, g = jnp.split(acc, 2, axis=-1)
    return (jax.nn.silu(g) * a) @ w_down


def setup_inputs(seed: int = 0) -> dict:
    key = jax.random.key(seed)
    ks = jax.random.split(key, 24)
    nrm = jax.random.normal
    f32 = jnp.float32
    return {
        'x': nrm(ks[0], (BATCH, SEQ, D_MODEL), f32),
        'c': nrm(ks[1], (BATCH, D_MODEL), f32),
        'ctx': nrm(ks[2], (BATCH, CTX_LEN, D_MODEL), f32),
        'c_ctx': nrm(ks[3], (D_MODEL,), f32),
        'ada_w': nrm(ks[4], (DEPTH, D_MODEL, 6 * D_MODEL), f32) * (0.5 * D_MODEL ** -0.5),
        'ada_b': nrm(ks[5], (DEPTH, 6 * D_MODEL), f32) * 0.02,
        'ln_g': 1.0 + 0.02 * nrm(ks[6], (DEPTH, 2, D_MODEL), f32),
        'ln_b': 0.02 * nrm(ks[7], (DEPTH, 2, D_MODEL), f32),
        'ev_w_in': nrm(ks[8], (N_EVEN, D_MODEL, EVEN_IN), f32) * D_MODEL ** -0.5,
        'ev_w_out': nrm(ks[9], (N_EVEN, EVEN_MIX, D_MODEL), f32) * (EVEN_MIX ** -0.5 * DN_BETA),
        'ev_q_gain': 1.0 + 0.02 * nrm(ks[10], (N_EVEN, HEAD_DIM), f32),
        'ev_k_gain': 1.0 + 0.02 * nrm(ks[11], (N_EVEN, HEAD_DIM), f32),
        'ev_rpb': 0.5 * nrm(ks[12], (N_EVEN, B_HEADS, 2 * NB_KH_MAX - 1, 2 * NB_KW - 1), f32),
        'od_w_in': nrm(ks[13], (N_ODD, D_MODEL, ODD_IN), f32) * D_MODEL ** -0.5,
        'od_w_out': nrm(ks[14], (N_ODD, ODD_MIX, D_MODEL), f32) * (ODD_MIX ** -0.5 * DN_BETA),
        'od_sink': nrm(ks[15], (N_ODD, C_HEADS), f32),
        'ffn_w_up': nrm(ks[16], (DEPTH, D_MODEL, 2 * D_FF), f32) * D_MODEL ** -0.5,
        'ffn_conv_w': nrm(ks[17], (DEPTH, CONV_W, 2 * D_FF), f32) * CONV_W ** -0.5,
        'ffn_conv_b': 0.02 * nrm(ks[18], (DEPTH, 2 * D_FF), f32),
        'ffn_w_down': nrm(ks[19], (DEPTH, D_FF, D_MODEL), f32) * (D_FF ** -0.5 * DN_BETA),
    }


def reference(x, c, ctx, c_ctx, ada_w, ada_b, ln_g, ln_b, ev_w_in, ev_w_out, ev_q_gain, ev_k_gain, ev_rpb,
              od_w_in, od_w_out, od_sink, ffn_w_up, ffn_conv_w, ffn_conv_b, ffn_w_down):
    S = x.shape[1]
    cos, sin = axial_rope_tables(S)
    x_lat, x_ctx = x, ctx
    for l in range(DEPTH):
        ctx_out = l < DEPTH - 1
        i = l // 2
        m_lat = [m[:, None, :] for m in jnp.split(jax.nn.silu(c) @ ada_w[l] + ada_b[l], 6, axis=-1)]
        m_ctx = jnp.split(jax.nn.silu(c_ctx) @ ada_w[l] + ada_b[l], 6, axis=-1)
        h_lat = modulate(x_lat, m_lat[0], m_lat[1])
        h_ctx = modulate(x_ctx, m_ctx[0], m_ctx[1])
        if l % 2 == 0:
            y_lat, y_ctx = even_mixer(h_lat, h_ctx, ev_w_in[i], ev_w_out[i], ev_q_gain[i], ev_k_gain[i],
                                      ev_rpb[i], cos, sin, ctx_out)
        else:
            y_lat, y_ctx = odd_mixer(h_lat, h_ctx, od_w_in[i], od_w_out[i], od_sink[i], cos, sin, ctx_out)
        x_lat = layer_norm(DN_ALPHA * x_lat + m_lat[2] * y_lat, ln_g[l, 0], ln_b[l, 0])
        f_lat = conv_ffn(modulate(x_lat, m_lat[3], m_lat[4]), ffn_w_up[l], ffn_conv_w[l], ffn_conv_b[l], ffn_w_down[l])
        x_lat = layer_norm(DN_ALPHA * x_lat + m_lat[5] * f_lat, ln_g[l, 1], ln_b[l, 1])
        if ctx_out:
            x_ctx = layer_norm(DN_ALPHA * x_ctx + m_ctx[2] * y_ctx, ln_g[l, 0], ln_b[l, 0])
            f_ctx = conv_ffn(modulate(x_ctx, m_ctx[3], m_ctx[4]), ffn_w_up[l], ffn_conv_w[l], ffn_conv_b[l], ffn_w_down[l])
            x_ctx = layer_norm(DN_ALPHA * x_ctx + m_ctx[5] * f_ctx, ln_g[l, 1], ln_b[l, 1])
    return x_lat
```

```python
import functools

import numpy as np
import jax
import jax.numpy as jnp
from jax import lax
from jax.experimental import pallas as pl
from jax.experimental.pallas import tpu as pltpu

D_MODEL = 1024
DEPTH = 4
GRID_W = 64
HEAD_DIM = 64
A_HEADS = 8
A_KV_HEADS = 2
B_HEADS = 8
NB_KH = 8
NB_KW = 16
C_HEADS = 16
C_KV_HEADS = 2
C_WINDOW = 128
D_FF = 2816
ROPE_THETA = 10000.0
LN_EPS = 1e-5
RMS_EPS = 1e-6
NEG = -1e30
DN_ALPHA = (2 * DEPTH) ** 0.25
A_Q = A_HEADS * HEAD_DIM
A_KV = A_KV_HEADS * HEAD_DIM
B_W = B_HEADS * HEAD_DIM
C_Q = C_HEADS * HEAD_DIM
C_KV = C_KV_HEADS * HEAD_DIM
QK_SCALE = HEAD_DIM ** -0.5

LANES = 128
HALO = 8
VMEM_LIMIT = 52 * 1024 * 1024

BF16 = jnp.bfloat16
F32 = jnp.float32

_NT = (((1,), (1,)), ((), ()))


def _params(sem):
    return pltpu.CompilerParams(dimension_semantics=sem, vmem_limit_bytes=VMEM_LIMIT)


def _full(shape):
    return pl.BlockSpec(shape, lambda *_: (0,) * len(shape))


def _lane_lo(shape):
    return lax.broadcasted_iota(jnp.int32, shape, len(shape) - 1) < HEAD_DIM


def _mod_kernel(c_ref, w_ref, b_ref, o_ref):
    s = c_ref[...]
    s = s * jax.nn.sigmoid(s)
    o_ref[0] = jnp.dot(s, w_ref[0], preferred_element_type=F32, precision=lax.Precision.HIGHEST) + b_ref[0]


def _modulation(cvec, ada_w, ada_b):
    nb = 6
    return pl.pallas_call(
        _mod_kernel,
        grid=(DEPTH, nb),
        in_specs=[pl.BlockSpec((8, D_MODEL), lambda l, j: (0, 0)),
                  pl.BlockSpec((1, D_MODEL, D_MODEL), lambda l, j: (l, 0, j)),
                  pl.BlockSpec((1, 1, D_MODEL), lambda l, j: (l, 0, j))],
        out_specs=pl.BlockSpec((1, 8, D_MODEL), lambda l, j: (l, 0, j)),
        out_shape=jax.ShapeDtypeStruct((DEPTH, 8, 6 * D_MODEL), F32),
        compiler_params=_params(("arbitrary", "arbitrary")),
        name="modulation",
    )(cvec, ada_w, ada_b.reshape(DEPTH, 1, 6 * D_MODEL))


def _group_sumsq(z, gmat):
    x2 = z * z
    hi = x2.astype(BF16)
    lo = (x2 - hi.astype(F32)).astype(BF16)
    return (jnp.dot(hi, gmat, preferred_element_type=F32) + jnp.dot(lo, gmat, preferred_element_type=F32))


def _proj_kernel(groups, x_ref, mod_ref, w_ref, gain_ref, cos_ref, sin_ref, gmat_ref, *out_refs):
    shift = mod_ref[0:1, :]
    scale = mod_ref[1:2, :]
    h = (x_ref[...] * (1.0 + scale) + shift).astype(BF16)
    cos = cos_ref[...]
    sin = sin_ref[...]
    gmat = gmat_ref[...]
    first = lax.broadcasted_iota(jnp.int32, cos.shape, 1) % 32 < 16
    for (c0, width, g0, rope, qscale), o_ref in zip(groups, out_refs):
        z = jnp.dot(h, w_ref[:, c0:c0 + width], preferred_element_type=F32)
        for b in range(width // LANES):
            zb = z[:, b * LANES:(b + 1) * LANES]
            if g0 is not None:
                ms = _group_sumsq(zb, gmat) * (1.0 / HEAD_DIM)
                zb = zb * lax.rsqrt(ms + RMS_EPS) * gain_ref[:, g0 + b * LANES:g0 + (b + 1) * LANES]
            if rope:
                rot = jnp.where(first, pltpu.roll(zb, LANES - 16, 1), pltpu.roll(zb, 16, 1))
                zb = zb * cos + rot * sin
            if qscale != 1.0:
                zb = zb * qscale
            o_ref[:, b * LANES:(b + 1) * LANES] = zb.astype(BF16)


def _project(x, mod, w, gains, cos, sin, gmat, groups, tm):
    n = x.shape[0]
    win = w.shape[1]
    out_shape = [jax.ShapeDtypeStruct((n, g[1]), BF16) for g in groups]
    out_specs = [pl.BlockSpec((tm, g[1]), lambda i: (i, 0)) for g in groups]
    return pl.pallas_call(
        functools.partial(_proj_kernel, groups),
        grid=(n // tm,),
        in_specs=[pl.BlockSpec((tm, D_MODEL), lambda i: (i, 0)),
                  _full((6, D_MODEL)),
                  _full((D_MODEL, win)),
                  _full(gains.shape),
                  pl.BlockSpec((tm, LANES), lambda i: (i, 0)),
                  pl.BlockSpec((tm, LANES), lambda i: (i, 0)),
                  _full((LANES, LANES))],
        out_specs=out_specs,
        out_shape=out_shape,
        compiler_params=_params(("parallel",)),
        name="qkv_project",
    )(x, mod, w, gains, cos, sin, gmat)


def _global_attn_kernel(q_ref, kc_ref, vc_ref, k_ref, v_ref, o_ref, qm_ref, m_ref, l_ref, acc_ref):
    kk = pl.program_id(1)
    nblk = q_ref.shape[1] // LANES
    nh = 2 * nblk

    def update(kb, vb):
        for h in range(nh):
            s = lax.dot_general(qm_ref[h], kb, _NT, preferred_element_type=F32)
            m_prev = m_ref[h]
            m_new = jnp.maximum(m_prev, jnp.max(s, axis=-1, keepdims=True))
            a = jnp.exp(m_prev - m_new)
            p = jnp.exp(s - m_new)
            l_ref[h] = a * l_ref[h] + jnp.sum(p, axis=-1, keepdims=True)
            acc_ref[h] = a * acc_ref[h] + jnp.dot(p.astype(BF16), vb, preferred_element_type=F32)
            m_ref[h] = m_new

    @pl.when(kk == 0)
    def _():
        lo = _lane_lo((q_ref.shape[0], LANES))
        for j in range(nblk):
            qj = q_ref[:, j * LANES:(j + 1) * LANES]
            zero = jnp.zeros_like(qj)
            qm_ref[2 * j] = jnp.where(lo, qj, zero)
            qm_ref[2 * j + 1] = jnp.where(lo, zero, qj)
        m_ref[...] = jnp.full(m_ref.shape, NEG, F32)
        l_ref[...] = jnp.zeros(l_ref.shape, F32)
        acc_ref[...] = jnp.zeros(acc_ref.shape, F32)
        update(kc_ref[...], vc_ref[...])

    update(k_ref[...], v_ref[...])

    @pl.when(kk == pl.num_programs(1) - 1)
    def _():
        lo = _lane_lo((q_ref.shape[0], LANES))
        for j in range(nblk):
            o_lo = acc_ref[2 * j] / l_ref[2 * j]
            o_hi = acc_ref[2 * j + 1] / l_ref[2 * j + 1]
            o_ref[:, j * LANES:(j + 1) * LANES] = jnp.where(lo, o_lo, o_hi).astype(o_ref.dtype)


def _global_attention(q, k, v, kc, vc, tq, tk):
    n, qw = q.shape
    nc = kc.shape[0]
    nh = 2 * (qw // LANES)
    return pl.pallas_call(
        _global_attn_kernel,
        grid=(n // tq, n // tk),
        in_specs=[pl.BlockSpec((tq, qw), lambda i, j: (i, 0)),
                  _full((nc, LANES)), _full((nc, LANES)),
                  pl.BlockSpec((tk, LANES), lambda i, j: (j, 0)),
                  pl.BlockSpec((tk, LANES), lambda i, j: (j, 0))],
        out_specs=pl.BlockSpec((tq, qw), lambda i, j: (i, 0)),
        out_shape=jax.ShapeDtypeStruct((n, qw), BF16),
        scratch_shapes=[pltpu.VMEM((nh, tq, LANES), BF16),
                        pltpu.VMEM((nh, tq, 1), F32),
                        pltpu.VMEM((nh, tq, 1), F32),
                        pltpu.VMEM((nh, tq, LANES), F32)],
        compiler_params=_params(("parallel", "arbitrary")),
        name="global_attention",
    )(q, kc, vc, k, v)


NB_QROWS = 8
NB_KROWS = NB_QROWS + NB_KH


def _nbr_attn_kernel(q_ref, k_ref, v_ref, kc_ref, vc_ref, bias_ref, o_ref):
    i = pl.program_id(1)
    rows = k_ref.shape[0] // GRID_W
    r0 = i * NB_QROWS
    ks = jnp.clip(r0 - NB_KH // 2, 0, rows - NB_KROWS) * GRID_W
    ks = pl.multiple_of(ks, GRID_W)
    kw = k_ref[pl.ds(ks, NB_KROWS * GRID_W), :]
    vw = v_ref[pl.ds(ks, NB_KROWS * GRID_W), :]
    kc = kc_ref[...]
    vc = vc_ref[...]
    q = q_ref[...]
    lo = _lane_lo(q.shape)
    zero = jnp.zeros_like(q)
    outs = []
    for half in range(2):
        qm = jnp.where(lo, q, zero) if half == 0 else jnp.where(lo, zero, q)
        s_nb = lax.dot_general(qm, kw, _NT, preferred_element_type=F32) + bias_ref[half, 0]
        s_cx = lax.dot_general(qm, kc, _NT, preferred_element_type=F32)
        m = jnp.maximum(jnp.max(s_nb, axis=-1, keepdims=True), jnp.max(s_cx, axis=-1, keepdims=True))
        p_nb = jnp.exp(s_nb - m)
        p_cx = jnp.exp(s_cx - m)
        den = jnp.sum(p_nb, axis=-1, keepdims=True) + jnp.sum(p_cx, axis=-1, keepdims=True)
        o = (jnp.dot(p_nb.astype(BF16), vw, preferred_element_type=F32)
             + jnp.dot(p_cx.astype(BF16), vc, preferred_element_type=F32))
        outs.append(o / den)
    o_ref[...] = jnp.where(lo, outs[0], outs[1]).astype(o_ref.dtype)


def _nbr_bias_tables(rpb, rows):
    nq = rows // NB_QROWS
    qi = np.arange(NB_QROWS * GRID_W)
    ki = np.arange(NB_KROWS * GRID_W)
    qr, qc = qi // GRID_W, qi % GRID_W
    kr, kc = ki // GRID_W, ki % GRID_W
    cs = np.clip(qc - NB_KW // 2, 0, GRID_W - NB_KW)
    col_ok = (kc[None, :] >= cs[:, None]) & (kc[None, :] < cs[:, None] + NB_KW)
    dcol = np.clip(kc[None, :] - qc[:, None] + NB_KW - 1, 0, 2 * NB_KW - 2)
    tabs = []
    for blk in (0, 1, nq - 1):
        r0 = blk * NB_QROWS
        k0 = int(np.clip(r0 - NB_KH // 2, 0, rows - NB_KROWS))
        qa = r0 + qr
        ka = k0 + kr
        rs = np.clip(qa - NB_KH // 2, 0, rows - NB_KH)
        row_ok = (ka[None, :] >= rs[:, None]) & (ka[None, :] < rs[:, None] + NB_KH)
        drow = np.clip(ka[None, :] - qa[:, None] + NB_KH - 1, 0, 2 * NB_KH - 2)
        ok = row_ok & col_ok
        tabs.append(jnp.where(ok[None], rpb[:, drow, dcol], NEG))
    return jnp.stack(tabs, axis=1)


def _nbr_attention(q, k, v, kc, vc, bias):
    n, qw = q.shape
    nc = kc.shape[0]
    tq = NB_QROWS * GRID_W
    nq = n // tq
    npair = qw // LANES

    def bias_map(p, i):
        return (p, jnp.where(i == 0, 0, jnp.where(i == nq - 1, 2, 1)), 0, 0)

    return pl.pallas_call(
        _nbr_attn_kernel,
        grid=(npair, nq),
        in_specs=[pl.BlockSpec((tq, LANES), lambda p, i: (i, p)),
                  pl.BlockSpec((n, LANES), lambda p, i: (0, p)),
                  pl.BlockSpec((n, LANES), lambda p, i: (0, p)),
                  pl.BlockSpec((nc, LANES), lambda p, i: (0, p)),
                  pl.BlockSpec((nc, LANES), lambda p, i: (0, p)),
                  pl.BlockSpec((2, 1, tq, NB_KROWS * GRID_W), bias_map)],
        out_specs=pl.BlockSpec((tq, LANES), lambda p, i: (i, p)),
        out_shape=jax.ShapeDtypeStruct((n, qw), BF16),
        compiler_params=_params(("parallel", "arbitrary")),
        name="neighbourhood_attention",
    )(q, k, v, kc, vc, bias)


WIN_TQ = 256
WIN_SPAN = WIN_TQ + 2 * C_WINDOW


def _window_attn_kernel(sink_ref, q_ref, k_ref, v_ref, kc_ref, vc_ref, o_ref):
    i = pl.program_id(0)
    n = k_ref.shape[0]
    tq = q_ref.shape[0]
    nblk = q_ref.shape[1] // LANES
    q0 = i * tq
    ks = pl.multiple_of(jnp.clip(q0 - C_WINDOW, 0, n - WIN_SPAN), C_WINDOW)
    kw = k_ref[pl.ds(ks, WIN_SPAN), :]
    vw = v_ref[pl.ds(ks, WIN_SPAN), :]
    kc = kc_ref[...]
    vc = vc_ref[...]
    qpos = q0 + lax.broadcasted_iota(jnp.int32, (tq, WIN_SPAN), 0)
    kpos = ks + lax.broadcasted_iota(jnp.int32, (tq, WIN_SPAN), 1)
    valid = jnp.abs(kpos - qpos) <= C_WINDOW
    lo = _lane_lo((tq, LANES))
    for j in range(nblk):
        q = q_ref[:, j * LANES:(j + 1) * LANES]
        zero = jnp.zeros_like(q)
        outs = []
        for half in range(2):
            qm = jnp.where(lo, q, zero) if half == 0 else jnp.where(lo, zero, q)
            sink = sink_ref[j + nblk * half]
            s_w = jnp.where(valid, lax.dot_general(qm, kw, _NT, preferred_element_type=F32), NEG)
            s_c = lax.dot_general(qm, kc, _NT, preferred_element_type=F32)
            m = jnp.maximum(jnp.maximum(jnp.max(s_w, axis=-1, keepdims=True),
                                        jnp.max(s_c, axis=-1, keepdims=True)), sink)
            p_w = jnp.exp(s_w - m)
            p_c = jnp.exp(s_c - m)
            den = (jnp.sum(p_w, axis=-1, keepdims=True) + jnp.sum(p_c, axis=-1, keepdims=True)
                   + jnp.exp(sink - m))
            o = (jnp.dot(p_w.astype(BF16), vw, preferred_element_type=F32)
                 + jnp.dot(p_c.astype(BF16), vc, preferred_element_type=F32))
            outs.append(o / den)
        o_ref[:, j * LANES:(j + 1) * LANES] = jnp.where(lo, outs[0], outs[1]).astype(o_ref.dtype)


def _window_attention(q, k, v, kc, vc, sink):
    n, qw = q.shape
    nc = kc.shape[0]
    return pl.pallas_call(
        _window_attn_kernel,
        grid=(n // WIN_TQ,),
        in_specs=[pl.BlockSpec(memory_space=pltpu.SMEM),
                  pl.BlockSpec((WIN_TQ, qw), lambda i: (i, 0)),
                  _full((n, LANES)), _full((n, LANES)),
                  _full((nc, LANES)), _full((nc, LANES))],
        out_specs=pl.BlockSpec((WIN_TQ, qw), lambda i: (i, 0)),
        out_shape=jax.ShapeDtypeStruct((n, qw), BF16),
        compiler_params=_params(("parallel",)),
        name="window_attention",
    )(sink, q, k, v, kc, vc)


def _ctx_attn_kernel(sink_ref, q_ref, k_ref, v_ref, o_ref):
    j = pl.program_id(0)
    nblk = pl.num_programs(0)
    q = q_ref[...]
    k = k_ref[...]
    v = v_ref[...]
    lo = _lane_lo(q.shape)
    zero = jnp.zeros_like(q)
    outs = []
    for half in range(2):
        qm = jnp.where(lo, q, zero) if half == 0 else jnp.where(lo, zero, q)
        sink = sink_ref[j + nblk * half]
        s = lax.dot_general(qm, k, _NT, preferred_element_type=F32)
        m = jnp.maximum(jnp.max(s, axis=-1, keepdims=True), sink)
        p = jnp.exp(s - m)
        den = jnp.sum(p, axis=-1, keepdims=True) + jnp.exp(sink - m)
        outs.append(jnp.dot(p.astype(BF16), v, preferred_element_type=F32) / den)
    o_ref[...] = jnp.where(lo, outs[0], outs[1]).astype(o_ref.dtype)


def _ctx_attention(q, k, v, sink, kv_per_block):
    c, qw = q.shape
    nblk = qw // LANES
    return pl.pallas_call(
        _ctx_attn_kernel,
        grid=(nblk,),
        in_specs=[pl.BlockSpec(memory_space=pltpu.SMEM),
                  pl.BlockSpec((c, LANES), lambda j: (0, j)),
                  pl.BlockSpec((c, LANES), lambda j: (0, j * kv_per_block)),
                  pl.BlockSpec((c, LANES), lambda j: (0, j * kv_per_block))],
        out_specs=pl.BlockSpec((c, LANES), lambda j: (0, j)),
        out_shape=jax.ShapeDtypeStruct((c, qw), BF16),
        compiler_params=_params(("parallel",)),
        name="context_attention",
    )(sink, q, k, v)


def _residual_layer_norm(x, f, gate, g, b):
    z = DN_ALPHA * x + gate * f
    mu = jnp.mean(z, axis=-1, keepdims=True)
    zc = z - mu
    var = jnp.mean(zc * zc, axis=-1, keepdims=True)
    return zc * lax.rsqrt(var + LN_EPS) * g + b


def _outproj_kernel(x_ref, y1_ref, y2_ref, w1_ref, w2_ref, mod_ref, g_ref, b_ref, o_ref):
    f = (jnp.dot(y1_ref[...], w1_ref[...], preferred_element_type=F32)
         + jnp.dot(y2_ref[...], w2_ref[...], preferred_element_type=F32))
    o_ref[...] = _residual_layer_norm(x_ref[...], f, mod_ref[2:3, :], g_ref[...], b_ref[...])


def _out_project(x, y1, y2, c2, w, mod, g, b, tm):
    n = x.shape[0]
    half = D_MODEL // 2
    return pl.pallas_call(
        _outproj_kernel,
        grid=(n // tm,),
        in_specs=[pl.BlockSpec((tm, D_MODEL), lambda i: (i, 0)),
                  pl.BlockSpec((tm, half), lambda i: (i, 0)),
                  pl.BlockSpec((tm, half), lambda i: (i, c2)),
                  pl.BlockSpec((half, D_MODEL), lambda i: (0, 0)),
                  pl.BlockSpec((half, D_MODEL), lambda i: (1, 0)),
                  _full((6, D_MODEL)), _full((1, D_MODEL)), _full((1, D_MODEL))],
        out_specs=pl.BlockSpec((tm, D_MODEL), lambda i: (i, 0)),
        out_shape=jax.ShapeDtypeStruct((n, D_MODEL), F32),
        compiler_params=_params(("parallel",)),
        name="out_project_ln",
    )(x, y1, y2, w, w, mod, g, b)


FFN_CHUNKS = 2
FFN_FC = D_FF // FFN_CHUNKS


def _ffn_kernel(x_ref, xp_ref, xn_ref, mod_ref, wa_ref, wg_ref, cwa_ref, cwg_ref, cba_ref, cbg_ref, wd_ref,
                g_ref, b_ref, o_ref, h_ref, ua_ref, ug_ref, acc_ref):
    i = pl.program_id(0)
    c = pl.program_id(1)
    tm = x_ref.shape[0]

    @pl.when(c == 0)
    def _():
        shift = mod_ref[3:4, :]
        scale = 1.0 + mod_ref[4:5, :]
        hp = jnp.where(i > 0, xp_ref[...] * scale + shift, 0.0)
        hn = jnp.where(i < pl.num_programs(0) - 1, xn_ref[...] * scale + shift, 0.0)
        hm = x_ref[...] * scale + shift
        h_ref[...] = jnp.concatenate([hp, hm, hn], axis=0).astype(BF16)
        acc_ref[...] = jnp.zeros(acc_ref.shape, F32)

    h = h_ref[...]
    ua_ref[...] = jnp.dot(h, wa_ref[...], preferred_element_type=F32)
    ug_ref[...] = jnp.dot(h, wg_ref[...], preferred_element_type=F32)

    def conv(u_ref, cw_ref, cb_ref):
        return (cb_ref[...]
                + u_ref[HALO - 1:HALO - 1 + tm, :] * cw_ref[0:1, :]
                + u_ref[HALO:HALO + tm, :] * cw_ref[1:2, :]
                + u_ref[HALO + 1:HALO + 1 + tm, :] * cw_ref[2:3, :])

    a = conv(ua_ref, cwa_ref, cba_ref)
    gt = conv(ug_ref, cwg_ref, cbg_ref)
    act = (gt * jax.nn.sigmoid(gt) * a).astype(BF16)
    acc_ref[...] += jnp.dot(act, wd_ref[...], preferred_element_type=F32)

    @pl.when(c == pl.num_programs(1) - 1)
    def _():
        o_ref[...] = _residual_layer_norm(x_ref[...], acc_ref[...], mod_ref[5:6, :], g_ref[...], b_ref[...])


def _conv_ffn(x, mod, w_up, conv_w, conv_b, w_down, g, b, tm):
    n = x.shape[0]
    nrow = n // tm
    nh = n // HALO
    per = tm // HALO
    nc = FFN_CHUNKS
    return pl.pallas_call(
        _ffn_kernel,
        grid=(nrow, nc),
        in_specs=[pl.BlockSpec((tm, D_MODEL), lambda i, c: (i, 0)),
                  pl.BlockSpec((HALO, D_MODEL), lambda i, c: (jnp.maximum(i * per - 1, 0), 0)),
                  pl.BlockSpec((HALO, D_MODEL), lambda i, c: (jnp.minimum((i + 1) * per, nh - 1), 0)),
                  _full((6, D_MODEL)),
                  pl.BlockSpec((D_MODEL, FFN_FC), lambda i, c: (0, c)),
                  pl.BlockSpec((D_MODEL, FFN_FC), lambda i, c: (0, nc + c)),
                  pl.BlockSpec((3, FFN_FC), lambda i, c: (0, c)),
                  pl.BlockSpec((3, FFN_FC), lambda i, c: (0, nc + c)),
                  pl.BlockSpec((1, FFN_FC), lambda i, c: (0, c)),
                  pl.BlockSpec((1, FFN_FC), lambda i, c: (0, nc + c)),
                  pl.BlockSpec((FFN_FC, D_MODEL), lambda i, c: (c, 0)),
                  _full((1, D_MODEL)), _full((1, D_MODEL))],
        out_specs=pl.BlockSpec((tm, D_MODEL), lambda i, c: (i, 0)),
        out_shape=jax.ShapeDtypeStruct((n, D_MODEL), F32),
        scratch_shapes=[pltpu.VMEM((tm + 2 * HALO, D_MODEL), BF16),
                        pltpu.VMEM((tm + 2 * HALO, FFN_FC), F32),
                        pltpu.VMEM((tm + 2 * HALO, FFN_FC), F32),
                        pltpu.VMEM((tm, D_MODEL), F32)],
        compiler_params=_params(("parallel", "arbitrary")),
        name="conv_ffn_ln",
    )(x, x, x, mod, w_up, w_up, conv_w, conv_w, conv_b, conv_b, w_down, g, b)


def _pair_perm(n_heads, kv_heads):
    per = n_heads // kv_heads
    order = []
    for j in range(per):
        for g in range(kv_heads):
            order.append(g * per + j)
    cols = np.concatenate([np.arange(h * HEAD_DIM, (h + 1) * HEAD_DIM) for h in order])
    return cols


_PERM_A = _pair_perm(A_HEADS, A_KV_HEADS)
_PERM_C = _pair_perm(C_HEADS, C_KV_HEADS)


def _rope_tables(n):
    t = np.arange(n)
    row = (t // GRID_W).astype(np.float32)
    col = (t % GRID_W).astype(np.float32)
    half = HEAD_DIM // 2
    inv = jnp.asarray(ROPE_THETA, F32) ** (-jnp.arange(0, half, 2, dtype=F32) / half)
    ang_r = jnp.asarray(row)[:, None] * inv
    ang_c = jnp.asarray(col)[:, None] * inv
    ang = jnp.concatenate([ang_r, ang_r, ang_c, ang_c], -1)
    ang = jnp.concatenate([ang, ang], -1)
    sign = np.where(np.arange(LANES) % 32 < 16, -1.0, 1.0).astype(np.float32)
    return jnp.cos(ang), jnp.sin(ang) * sign


_EVEN_GROUPS = (
    (0, A_Q, 0, True, QK_SCALE),
    (A_Q, A_KV, A_Q, True, 1.0),
    (A_Q + A_KV, A_KV, None, False, 1.0),
    (A_Q + 2 * A_KV, B_W, None, False, QK_SCALE),
    (A_Q + 2 * A_KV + B_W, B_W, None, False, 1.0),
    (A_Q + 2 * A_KV + 2 * B_W, B_W, None, False, 1.0),
)
_EVEN_GROUPS_CTX = tuple((c0, w, g0, False, s) for (c0, w, g0, _, s) in _EVEN_GROUPS)
_ODD_GROUPS = (
    (0, C_Q, None, True, QK_SCALE),
    (C_Q, C_KV, None, True, 1.0),
    (C_Q + C_KV, C_KV, None, False, 1.0),
)
_ODD_GROUPS_CTX = tuple((c0, w, g0, False, s) for (c0, w, g0, _, s) in _ODD_GROUPS)

LAT_TM = 512


def kernel(x, c, ctx, c_ctx, ada_w, ada_b, ln_g, ln_b, ev_w_in, ev_w_out, ev_q_gain, ev_k_gain, ev_rpb,
           od_w_in, od_w_out, od_sink, ffn_w_up, ffn_conv_w, ffn_conv_b, ffn_w_down):
    n = x.shape[1]
    nc = ctx.shape[1]
    rows = n // GRID_W
    x_lat = x[0]
    x_ctx = ctx[0]

    cvec = jnp.zeros((8, D_MODEL), F32).at[0].set(c[0]).at[1].set(c_ctx)
    mods = _modulation(cvec, ada_w, ada_b)

    cos, sin = _rope_tables(n)
    cos_c, sin_c = cos[:nc], sin[:nc]
    lane = np.arange(LANES)
    gmat = jnp.asarray((lane[:, None] // HEAD_DIM) == (lane[None, :] // HEAD_DIM), BF16)
    no_sink = jnp.full((C_HEADS,), NEG, F32)

    for l in range(DEPTH):
        i = l // 2
        ctx_out = l < DEPTH - 1
        m_lat = mods[l, 0].reshape(6, D_MODEL)
        m_ctx = mods[l, 1].reshape(6, D_MODEL)
        g0, b0 = ln_g[l, 0][None], ln_b[l, 0][None]
        g1, b1 = ln_g[l, 1][None], ln_b[l, 1][None]
        if l % 2 == 0:
            w_in = ev_w_in[i]
            w_in = jnp.concatenate([w_in[:, :A_Q][:, _PERM_A], w_in[:, A_Q:]], axis=1).astype(BF16)
            w_out = ev_w_out[i]
            w_out = jnp.concatenate([w_out[:A_Q][_PERM_A], w_out[A_Q:]], axis=0).astype(BF16)
            gains = jnp.concatenate([jnp.tile(ev_q_gain[i], A_HEADS), jnp.tile(ev_k_gain[i], A_KV_HEADS)])[None]
            qa, ka, va, qb, kb, vb = _project(x_lat, m_lat, w_in, gains, cos, sin, gmat, _EVEN_GROUPS, LAT_TM)
            qa_c, ka_c, va_c, qb_c, kb_c, vb_c = _project(x_ctx, m_ctx, w_in, gains, cos_c, sin_c, gmat,
                                                           _EVEN_GROUPS_CTX, nc)
            ya = _global_attention(qa, ka, va, ka_c, va_c, 512, 1024)
            bias = _nbr_bias_tables(ev_rpb[i], rows)
            yb = _nbr_attention(qb, kb, vb, kb_c, vb_c, bias)
            x_lat = _out_project(x_lat, ya, yb, 0, w_out, m_lat, g0, b0, LAT_TM)
            if ctx_out:
                ya_c = _ctx_attention(qa_c, ka_c, va_c, no_sink, 0)
                yb_c = _ctx_attention(qb_c, kb_c, vb_c, no_sink, 1)
                x_ctx = _out_project(x_ctx, ya_c, yb_c, 0, w_out, m_ctx, g0, b0, nc)
        else:
            w_in = od_w_in[i]
            w_in = jnp.concatenate([w_in[:, :C_Q][:, _PERM_C], w_in[:, C_Q:]], axis=1).astype(BF16)
            w_out = od_w_out[i][_PERM_C].astype(BF16)
            sink = od_sink[i]
            gains = jnp.ones((1, A_Q + A_KV), F32)
            q, k, v = _project(x_lat, m_lat, w_in, gains, cos, sin, gmat, _ODD_GROUPS, LAT_TM)
            q_c, k_c, v_c = _project(x_ctx, m_ctx, w_in, gains, cos_c, sin_c, gmat, _ODD_GROUPS_CTX, nc)
            y = _window_attention(q, k, v, k_c, v_c, sink)
            x_lat = _out_project(x_lat, y, y, 1, w_out, m_lat, g0, b0, LAT_TM)
            if ctx_out:
                y_c = _ctx_attention(q_c, k_c, v_c, sink, 0)
                x_ctx = _out_project(x_ctx, y_c, y_c, 1, w_out, m_ctx, g0, b0, nc)
        w_up = ffn_w_up[l].astype(BF16)
        w_down = ffn_w_down[l].astype(BF16)
        cb = ffn_conv_b[l][None]
        x_lat = _conv_ffn(x_lat, m_lat, w_up, ffn_conv_w[l], cb, w_down, g1, b1, LAT_TM)
        if ctx_out:
            x_ctx = _conv_ffn(x_ctx, m_ctx, w_up, ffn_conv_w[l], cb, w_down, g1, b1, nc)
    return x_lat[None]
```

```python
import functools

import numpy as np
import jax
import jax.numpy as jnp
from jax import lax
from jax.experimental import pallas as pl
from jax.experimental.pallas import tpu as pltpu

D_MODEL = 1024
DEPTH = 4
GRID_W = 64
HEAD_DIM = 64
A_HEADS = 8
A_KV_HEADS = 2
B_HEADS = 8
NB_KH = 8
NB_KW = 16
C_HEADS = 16
C_KV_HEADS = 2
C_WINDOW = 128
D_FF = 2816
ROPE_THETA = 10000.0
LN_EPS = 1e-5
RMS_EPS = 1e-6
NEG = -1e30
DN_ALPHA = (2 * DEPTH) ** 0.25
A_Q = A_HEADS * HEAD_DIM
A_KV = A_KV_HEADS * HEAD_DIM
B_W = B_HEADS * HEAD_DIM
C_Q = C_HEADS * HEAD_DIM
C_KV = C_KV_HEADS * HEAD_DIM
LOG2E = 1.4426950408889634
QK_SCALE = HEAD_DIM ** -0.5 * LOG2E

LANES = 128
HALO = 8
VMEM_LIMIT = 52 * 1024 * 1024

BF16 = jnp.bfloat16
F32 = jnp.float32

_NT = (((1,), (1,)), ((), ()))


def _params(sem):
    return pltpu.CompilerParams(dimension_semantics=sem, vmem_limit_bytes=VMEM_LIMIT)


def _full(shape):
    return pl.BlockSpec(shape, lambda *_: (0,) * len(shape))


def _lane_lo(shape):
    return lax.broadcasted_iota(jnp.int32, shape, len(shape) - 1) < HEAD_DIM


def _mod_kernel(c_ref, w_ref, b_ref, o_ref):
    s = c_ref[...]
    s = s * jax.nn.sigmoid(s)
    o_ref[0] = jnp.dot(s, w_ref[0], preferred_element_type=F32, precision=lax.Precision.HIGHEST) + b_ref[0]


def _modulation(cvec, ada_w, ada_b):
    nb = 6
    return pl.pallas_call(
        _mod_kernel,
        grid=(DEPTH, nb),
        in_specs=[pl.BlockSpec((8, D_MODEL), lambda l, j: (0, 0)),
                  pl.BlockSpec((1, D_MODEL, D_MODEL), lambda l, j: (l, 0, j)),
                  pl.BlockSpec((1, 1, D_MODEL), lambda l, j: (l, 0, j))],
        out_specs=pl.BlockSpec((1, 8, D_MODEL), lambda l, j: (l, 0, j)),
        out_shape=jax.ShapeDtypeStruct((DEPTH, 8, 6 * D_MODEL), F32),
        compiler_params=_params(("arbitrary", "arbitrary")),
        name="modulation",
    )(cvec, ada_w, ada_b.reshape(DEPTH, 1, 6 * D_MODEL))


def _group_sumsq(z, gmat):
    x2 = z * z
    hi = x2.astype(BF16)
    lo = (x2 - hi.astype(F32)).astype(BF16)
    return (jnp.dot(hi, gmat, preferred_element_type=F32) + jnp.dot(lo, gmat, preferred_element_type=F32))


def _proj_kernel(groups, x_ref, mod_ref, w_ref, gain_ref, cos_ref, sin_ref, gmat_ref, *out_refs):
    shift = mod_ref[0:1, :]
    scale = mod_ref[1:2, :]
    h = (x_ref[...] * (1.0 + scale) + shift).astype(BF16)
    cos = cos_ref[...]
    sin = sin_ref[...]
    gmat = gmat_ref[...]
    first = lax.broadcasted_iota(jnp.int32, cos.shape, 1) % 32 < 16
    for (c0, width, g0, rope, qscale), o_ref in zip(groups, out_refs):
        z = jnp.dot(h, w_ref[:, c0:c0 + width], preferred_element_type=F32)
        for b in range(width // LANES):
            zb = z[:, b * LANES:(b + 1) * LANES]
            if g0 is not None:
                ms = _group_sumsq(zb, gmat) * (1.0 / HEAD_DIM)
                zb = zb * lax.rsqrt(ms + RMS_EPS) * gain_ref[:, g0 + b * LANES:g0 + (b + 1) * LANES]
            if rope:
                rot = jnp.where(first, pltpu.roll(zb, LANES - 16, 1), pltpu.roll(zb, 16, 1))
                zb = zb * cos + rot * sin
            if qscale != 1.0:
                zb = zb * qscale
            o_ref[:, b * LANES:(b + 1) * LANES] = zb.astype(BF16)


def _project(x, mod, w, gains, cos, sin, gmat, groups, tm):
    n = x.shape[0]
    win = w.shape[1]
    out_shape = [jax.ShapeDtypeStruct((n, g[1]), BF16) for g in groups]
    out_specs = [pl.BlockSpec((tm, g[1]), lambda i: (i, 0)) for g in groups]
    return pl.pallas_call(
        functools.partial(_proj_kernel, groups),
        grid=(n // tm,),
        in_specs=[pl.BlockSpec((tm, D_MODEL), lambda i: (i, 0)),
                  _full((6, D_MODEL)),
                  _full((D_MODEL, win)),
                  _full(gains.shape),
                  pl.BlockSpec((tm, LANES), lambda i: (i, 0)),
                  pl.BlockSpec((tm, LANES), lambda i: (i, 0)),
                  _full((LANES, LANES))],
        out_specs=out_specs,
        out_shape=out_shape,
        compiler_params=_params(("parallel",)),
        name="qkv_project",
    )(x, mod, w, gains, cos, sin, gmat)


GA_RC = 512


def _v_with_ones(v):
    lane = lax.broadcasted_iota(jnp.int32, v.shape, 1)
    vf = v.astype(F32)
    v_lo = jnp.where(lane < HEAD_DIM, vf, jnp.where(lane == HEAD_DIM, 1.0, 0.0))
    v_hi = jnp.where(lane >= HEAD_DIM, vf, jnp.where(lane == 0, 1.0, 0.0))
    return v_lo.astype(v.dtype), v_hi.astype(v.dtype)


def _global_attn_kernel(q_ref, kc_ref, vc_ref, k_ref, v_ref, o_ref, qm_ref, m_ref, acc_ref):
    kk = pl.program_id(1)
    tq = q_ref.shape[0]
    nblk = q_ref.shape[1] // LANES
    nh = 2 * nblk

    def update(kb, vb):
        vx = _v_with_ones(vb)
        for h in range(nh):
            for r in range(0, tq, GA_RC):
                rs = slice(r, r + GA_RC)
                s = lax.dot_general(qm_ref[h, rs, :], kb, _NT, preferred_element_type=F32)
                m_prev = m_ref[h, rs, :]
                m_new = jnp.maximum(m_prev, jnp.max(s, axis=-1, keepdims=True))
                a = jnp.exp2(m_prev - m_new)
                p = jnp.exp2(s - m_new)
                acc_ref[h, rs, :] = a * acc_ref[h, rs, :] + jnp.dot(p.astype(BF16), vx[h % 2],
                                                                  preferred_element_type=F32)
                m_ref[h, rs, :] = m_new

    @pl.when(kk == 0)
    def _():
        lo = _lane_lo((tq, LANES))
        for j in range(nblk):
            qj = q_ref[:, j * LANES:(j + 1) * LANES]
            zero = jnp.zeros_like(qj)
            qm_ref[2 * j] = jnp.where(lo, qj, zero)
            qm_ref[2 * j + 1] = jnp.where(lo, zero, qj)
        m_ref[...] = jnp.full(m_ref.shape, NEG, F32)
        acc_ref[...] = jnp.zeros(acc_ref.shape, F32)
        update(kc_ref[...], vc_ref[...])

    update(k_ref[...], v_ref[...])

    @pl.when(kk == pl.num_programs(1) - 1)
    def _():
        lo = _lane_lo((tq, LANES))
        for j in range(nblk):
            a_lo = acc_ref[2 * j]
            a_hi = acc_ref[2 * j + 1]
            o_lo = a_lo / a_lo[:, HEAD_DIM:HEAD_DIM + 1]
            o_hi = a_hi / a_hi[:, 0:1]
            o_ref[:, j * LANES:(j + 1) * LANES] = jnp.where(lo, o_lo, o_hi).astype(o_ref.dtype)


def _global_attention(q, k, v, kc, vc, tq, tk):
    n, qw = q.shape
    nc = kc.shape[0]
    nh = 2 * (qw // LANES)
    return pl.pallas_call(
        _global_attn_kernel,
        grid=(n // tq, n // tk),
        in_specs=[pl.BlockSpec((tq, qw), lambda i, j: (i, 0)),
                  _full((nc, LANES)), _full((nc, LANES)),
                  pl.BlockSpec((tk, LANES), lambda i, j: (j, 0)),
                  pl.BlockSpec((tk, LANES), lambda i, j: (j, 0))],
        out_specs=pl.BlockSpec((tq, qw), lambda i, j: (i, 0)),
        out_shape=jax.ShapeDtypeStruct((n, qw), BF16),
        scratch_shapes=[pltpu.VMEM((nh, tq, LANES), BF16),
                        pltpu.VMEM((nh, tq, 1), F32),
                        pltpu.VMEM((nh, tq, LANES), F32)],
        compiler_params=_params(("parallel", "arbitrary")),
        name="global_attention",
    )(q, kc, vc, k, v)


NB_QROWS = 8
NB_KROWS = NB_QROWS + NB_KH


def _nbr_attn_kernel(q_ref, k_ref, v_ref, kc_ref, vc_ref, bias_ref, o_ref):
    i = pl.program_id(1)
    rows = k_ref.shape[0] // GRID_W
    r0 = i * NB_QROWS
    ks = jnp.clip(r0 - NB_KH // 2, 0, rows - NB_KROWS) * GRID_W
    ks = pl.multiple_of(ks, GRID_W)
    kw = k_ref[pl.ds(ks, NB_KROWS * GRID_W), :]
    vw = v_ref[pl.ds(ks, NB_KROWS * GRID_W), :]
    kc = kc_ref[...]
    vc = vc_ref[...]
    q = q_ref[...]
    lo = _lane_lo(q.shape)
    zero = jnp.zeros_like(q)
    outs = []
    for half in range(2):
        qm = jnp.where(lo, q, zero) if half == 0 else jnp.where(lo, zero, q)
        s_nb = lax.dot_general(qm, kw, _NT, preferred_element_type=F32) + bias_ref[half, 0]
        s_cx = lax.dot_general(qm, kc, _NT, preferred_element_type=F32)
        m = jnp.maximum(jnp.max(s_nb, axis=-1, keepdims=True), jnp.max(s_cx, axis=-1, keepdims=True))
        p_nb = jnp.exp2(s_nb - m)
        p_cx = jnp.exp2(s_cx - m)
        den = jnp.sum(p_nb, axis=-1, keepdims=True) + jnp.sum(p_cx, axis=-1, keepdims=True)
        o = (jnp.dot(p_nb.astype(BF16), vw, preferred_element_type=F32)
             + jnp.dot(p_cx.astype(BF16), vc, preferred_element_type=F32))
        outs.append(o / den)
    o_ref[...] = jnp.where(lo, outs[0], outs[1]).astype(o_ref.dtype)


def _nbr_bias_tables(rpb, rows):
    nq = rows // NB_QROWS
    nh = rpb.shape[0]
    col = np.arange(GRID_W)
    cs = np.clip(col - NB_KW // 2, 0, GRID_W - NB_KW)
    col_ok = (col[None, :] >= cs[:, None]) & (col[None, :] < cs[:, None] + NB_KW)
    dcol = col[None, :] - col[:, None] + NB_KW - 1
    sel_col = (dcol[:, :, None] == np.arange(2 * NB_KW - 1)) & col_ok[:, :, None]
    qr = np.arange(NB_QROWS)
    kr = np.arange(NB_KROWS)
    sel_row = np.zeros((3, NB_QROWS, NB_KROWS, 2 * NB_KH - 1), bool)
    for v, blk in enumerate((0, 1, nq - 1)):
        r0 = blk * NB_QROWS
        k0 = int(np.clip(r0 - NB_KH // 2, 0, rows - NB_KROWS))
        qa = r0 + qr
        ka = k0 + kr
        rs = np.clip(qa - NB_KH // 2, 0, rows - NB_KH)
        row_ok = (ka[None, :] >= rs[:, None]) & (ka[None, :] < rs[:, None] + NB_KH)
        drow = ka[None, :] - qa[:, None] + NB_KH - 1
        sel_row[v] = (drow[:, :, None] == np.arange(2 * NB_KH - 1)) & row_ok[:, :, None]
    hi = lax.Precision.HIGHEST
    toeplitz = jnp.einsum('qkj,hdj->hdqk', jnp.asarray(sel_col, F32), rpb, precision=hi)
    bias = jnp.einsum('vrsd,hdqk->hvrqsk', jnp.asarray(sel_row, F32), toeplitz, precision=hi)
    ok = sel_row.any(-1)[:, :, None, :, None] & col_ok[None, None, :, None, :]
    bias = jnp.where(jnp.asarray(ok)[None], bias, NEG)
    return bias.reshape(nh, 3, NB_QROWS * GRID_W, NB_KROWS * GRID_W)


def _nbr_attention(q, k, v, kc, vc, bias):
    n, qw = q.shape
    nc = kc.shape[0]
    tq = NB_QROWS * GRID_W
    nq = n // tq
    npair = qw // LANES

    def bias_map(p, i):
        return (p, jnp.where(i == 0, 0, jnp.where(i == nq - 1, 2, 1)), 0, 0)

    return pl.pallas_call(
        _nbr_attn_kernel,
        grid=(npair, nq),
        in_specs=[pl.BlockSpec((tq, LANES), lambda p, i: (i, p)),
                  pl.BlockSpec((n, LANES), lambda p, i: (0, p)),
                  pl.BlockSpec((n, LANES), lambda p, i: (0, p)),
                  pl.BlockSpec((nc, LANES), lambda p, i: (0, p)),
                  pl.BlockSpec((nc, LANES), lambda p, i: (0, p)),
                  pl.BlockSpec((2, 1, tq, NB_KROWS * GRID_W), bias_map)],
        out_specs=pl.BlockSpec((tq, LANES), lambda p, i: (i, p)),
        out_shape=jax.ShapeDtypeStruct((n, qw), BF16),
        compiler_params=_params(("parallel", "arbitrary")),
        name="neighbourhood_attention",
    )(q, k, v, kc, vc, bias)


WIN_TQ = 256
WIN_SPAN = WIN_TQ + 2 * C_WINDOW


def _window_attn_kernel(sink_ref, q_ref, k_ref, v_ref, kc_ref, vc_ref, o_ref):
    i = pl.program_id(0)
    n = k_ref.shape[0]
    tq = q_ref.shape[0]
    nblk = q_ref.shape[1] // LANES
    q0 = i * tq
    ks = pl.multiple_of(jnp.clip(q0 - C_WINDOW, 0, n - WIN_SPAN), C_WINDOW)
    kw = k_ref[pl.ds(ks, WIN_SPAN), :]
    vw = v_ref[pl.ds(ks, WIN_SPAN), :]
    kc = kc_ref[...]
    vc = vc_ref[...]
    qpos = q0 + lax.broadcasted_iota(jnp.int32, (tq, WIN_SPAN), 0)
    kpos = ks + lax.broadcasted_iota(jnp.int32, (tq, WIN_SPAN), 1)
    valid = jnp.abs(kpos - qpos) <= C_WINDOW
    lo = _lane_lo((tq, LANES))
    for j in range(nblk):
        q = q_ref[:, j * LANES:(j + 1) * LANES]
        zero = jnp.zeros_like(q)
        outs = []
        for half in range(2):
            qm = jnp.where(lo, q, zero) if half == 0 else jnp.where(lo, zero, q)
            sink = sink_ref[j + nblk * half]
            s_w = jnp.where(valid, lax.dot_general(qm, kw, _NT, preferred_element_type=F32), NEG)
            s_c = lax.dot_general(qm, kc, _NT, preferred_element_type=F32)
            m = jnp.maximum(jnp.maximum(jnp.max(s_w, axis=-1, keepdims=True),
                                        jnp.max(s_c, axis=-1, keepdims=True)), sink)
            p_w = jnp.exp2(s_w - m)
            p_c = jnp.exp2(s_c - m)
            den = (jnp.sum(p_w, axis=-1, keepdims=True) + jnp.sum(p_c, axis=-1, keepdims=True)
                   + jnp.exp2(sink - m))
            o = (jnp.dot(p_w.astype(BF16), vw, preferred_element_type=F32)
                 + jnp.dot(p_c.astype(BF16), vc, preferred_element_type=F32))
            outs.append(o / den)
        o_ref[:, j * LANES:(j + 1) * LANES] = jnp.where(lo, outs[0], outs[1]).astype(o_ref.dtype)


def _window_attention(q, k, v, kc, vc, sink):
    n, qw = q.shape
    nc = kc.shape[0]
    return pl.pallas_call(
        _window_attn_kernel,
        grid=(n // WIN_TQ,),
        in_specs=[pl.BlockSpec(memory_space=pltpu.SMEM),
                  pl.BlockSpec((WIN_TQ, qw), lambda i: (i, 0)),
                  _full((n, LANES)), _full((n, LANES)),
                  _full((nc, LANES)), _full((nc, LANES))],
        out_specs=pl.BlockSpec((WIN_TQ, qw), lambda i: (i, 0)),
        out_shape=jax.ShapeDtypeStruct((n, qw), BF16),
        compiler_params=_params(("parallel",)),
        name="window_attention",
    )(sink, q, k, v, kc, vc)


def _ctx_attn_kernel(sink_ref, q_ref, k_ref, v_ref, o_ref):
    j = pl.program_id(0)
    nblk = pl.num_programs(0)
    q = q_ref[...]
    k = k_ref[...]
    v = v_ref[...]
    lo = _lane_lo(q.shape)
    zero = jnp.zeros_like(q)
    outs = []
    for half in range(2):
        qm = jnp.where(lo, q, zero) if half == 0 else jnp.where(lo, zero, q)
        sink = sink_ref[j + nblk * half]
        s = lax.dot_general(qm, k, _NT, preferred_element_type=F32)
        m = jnp.maximum(jnp.max(s, axis=-1, keepdims=True), sink)
        p = jnp.exp2(s - m)
        den = jnp.sum(p, axis=-1, keepdims=True) + jnp.exp2(sink - m)
        outs.append(jnp.dot(p.astype(BF16), v, preferred_element_type=F32) / den)
    o_ref[...] = jnp.where(lo, outs[0], outs[1]).astype(o_ref.dtype)


def _ctx_attention(q, k, v, sink, kv_per_block):
    c, qw = q.shape
    nblk = qw // LANES
    return pl.pallas_call(
        _ctx_attn_kernel,
        grid=(nblk,),
        in_specs=[pl.BlockSpec(memory_space=pltpu.SMEM),
                  pl.BlockSpec((c, LANES), lambda j: (0, j)),
                  pl.BlockSpec((c, LANES), lambda j: (0, j * kv_per_block)),
                  pl.BlockSpec((c, LANES), lambda j: (0, j * kv_per_block))],
        out_specs=pl.BlockSpec((c, LANES), lambda j: (0, j)),
        out_shape=jax.ShapeDtypeStruct((c, qw), BF16),
        compiler_params=_params(("parallel",)),
        name="context_attention",
    )(sink, q, k, v)


def _residual_layer_norm(x, f, gate, g, b):
    z = DN_ALPHA * x + gate * f
    mu = jnp.mean(z, axis=-1, keepdims=True)
    zc = z - mu
    var = jnp.mean(zc * zc, axis=-1, keepdims=True)
    return zc * lax.rsqrt(var + LN_EPS) * g + b


def _outproj_kernel(x_ref, y1_ref, y2_ref, w1_ref, w2_ref, mod_ref, g_ref, b_ref, o_ref):
    f = (jnp.dot(y1_ref[...], w1_ref[...], preferred_element_type=F32)
         + jnp.dot(y2_ref[...], w2_ref[...], preferred_element_type=F32))
    o_ref[...] = _residual_layer_norm(x_ref[...], f, mod_ref[2:3, :], g_ref[...], b_ref[...])


def _out_project(x, y1, y2, c2, w, mod, g, b, tm):
    n = x.shape[0]
    half = D_MODEL // 2
    return pl.pallas_call(
        _outproj_kernel,
        grid=(n // tm,),
        in_specs=[pl.BlockSpec((tm, D_MODEL), lambda i: (i, 0)),
                  pl.BlockSpec((tm, half), lambda i: (i, 0)),
                  pl.BlockSpec((tm, half), lambda i: (i, c2)),
                  pl.BlockSpec((half, D_MODEL), lambda i: (0, 0)),
                  pl.BlockSpec((half, D_MODEL), lambda i: (1, 0)),
                  _full((6, D_MODEL)), _full((1, D_MODEL)), _full((1, D_MODEL))],
        out_specs=pl.BlockSpec((tm, D_MODEL), lambda i: (i, 0)),
        out_shape=jax.ShapeDtypeStruct((n, D_MODEL), F32),
        compiler_params=_params(("parallel",)),
        name="out_project_ln",
    )(x, y1, y2, w, w, mod, g, b)


FFN_CHUNKS = 2
FFN_FC = D_FF // FFN_CHUNKS


def _ffn_kernel(x_ref, xp_ref, xn_ref, mod_ref, wa_ref, wg_ref, cwa_ref, cwg_ref, cba_ref, cbg_ref, wd_ref,
                g_ref, b_ref, o_ref, h_ref, ua_ref, ug_ref, acc_ref):
    i = pl.program_id(0)
    c = pl.program_id(1)
    tm = x_ref.shape[0]

    @pl.when(c == 0)
    def _():
        shift = mod_ref[3:4, :]
        scale = 1.0 + mod_ref[4:5, :]
        hp = jnp.where(i > 0, xp_ref[...] * scale + shift, 0.0)
        hn = jnp.where(i < pl.num_programs(0) - 1, xn_ref[...] * scale + shift, 0.0)
        hm = x_ref[...] * scale + shift
        h_ref[...] = jnp.concatenate([hp, hm, hn], axis=0).astype(BF16)
        acc_ref[...] = jnp.zeros(acc_ref.shape, F32)

    h = h_ref[...]
    ua_ref[...] = jnp.dot(h, wa_ref[...], preferred_element_type=F32)
    ug_ref[...] = jnp.dot(h, wg_ref[...], preferred_element_type=F32)

    def conv(u_ref, cw_ref, cb_ref):
        return (cb_ref[...]
                + u_ref[HALO - 1:HALO - 1 + tm, :] * cw_ref[0:1, :]
                + u_ref[HALO:HALO + tm, :] * cw_ref[1:2, :]
                + u_ref[HALO + 1:HALO + 1 + tm, :] * cw_ref[2:3, :])

    a = conv(ua_ref, cwa_ref, cba_ref)
    gt = conv(ug_ref, cwg_ref, cbg_ref)
    act = (gt * jax.nn.sigmoid(gt) * a).astype(BF16)
    acc_ref[...] += jnp.dot(act, wd_ref[...], preferred_element_type=F32)

    @pl.when(c == pl.num_programs(1) - 1)
    def _():
        o_ref[...] = _residual_layer_norm(x_ref[...], acc_ref[...], mod_ref[5:6, :], g_ref[...], b_ref[...])


def _conv_ffn(x, mod, w_up, conv_w, conv_b, w_down, g, b, tm):
    n = x.shape[0]
    nrow = n // tm
    nh = n // HALO
    per = tm // HALO
    nc = FFN_CHUNKS
    return pl.pallas_call(
        _ffn_kernel,
        grid=(nrow, nc),
        in_specs=[pl.BlockSpec((tm, D_MODEL), lambda i, c: (i, 0)),
                  pl.BlockSpec((HALO, D_MODEL), lambda i, c: (jnp.maximum(i * per - 1, 0), 0)),
                  pl.BlockSpec((HALO, D_MODEL), lambda i, c: (jnp.minimum((i + 1) * per, nh - 1), 0)),
                  _full((6, D_MODEL)),
                  pl.BlockSpec((D_MODEL, FFN_FC), lambda i, c: (0, c)),
                  pl.BlockSpec((D_MODEL, FFN_FC), lambda i, c: (0, nc + c)),
                  pl.BlockSpec((3, FFN_FC), lambda i, c: (0, c)),
                  pl.BlockSpec((3, FFN_FC), lambda i, c: (0, nc + c)),
                  pl.BlockSpec((1, FFN_FC), lambda i, c: (0, c)),
                  pl.BlockSpec((1, FFN_FC), lambda i, c: (0, nc + c)),
                  pl.BlockSpec((FFN_FC, D_MODEL), lambda i, c: (c, 0)),
                  _full((1, D_MODEL)), _full((1, D_MODEL))],
        out_specs=pl.BlockSpec((tm, D_MODEL), lambda i, c: (i, 0)),
        out_shape=jax.ShapeDtypeStruct((n, D_MODEL), F32),
        scratch_shapes=[pltpu.VMEM((tm + 2 * HALO, D_MODEL), BF16),
                        pltpu.VMEM((tm + 2 * HALO, FFN_FC), F32),
                        pltpu.VMEM((tm + 2 * HALO, FFN_FC), F32),
                        pltpu.VMEM((tm, D_MODEL), F32)],
        compiler_params=_params(("parallel", "arbitrary")),
        name="conv_ffn_ln",
    )(x, x, x, mod, w_up, w_up, conv_w, conv_w, conv_b, conv_b, w_down, g, b)


def _pair_perm(n_heads, kv_heads):
    per = n_heads // kv_heads
    order = []
    for j in range(per):
        for g in range(kv_heads):
            order.append(g * per + j)
    cols = np.concatenate([np.arange(h * HEAD_DIM, (h + 1) * HEAD_DIM) for h in order])
    return cols


_PERM_A = _pair_perm(A_HEADS, A_KV_HEADS)
_PERM_C = _pair_perm(C_HEADS, C_KV_HEADS)


def _rope_tables(n):
    t = np.arange(n)
    row = (t // GRID_W).astype(np.float32)
    col = (t % GRID_W).astype(np.float32)
    half = HEAD_DIM // 2
    inv = jnp.asarray(ROPE_THETA, F32) ** (-jnp.arange(0, half, 2, dtype=F32) / half)
    ang_r = jnp.asarray(row)[:, None] * inv
    ang_c = jnp.asarray(col)[:, None] * inv
    ang = jnp.concatenate([ang_r, ang_r, ang_c, ang_c], -1)
    ang = jnp.concatenate([ang, ang], -1)
    sign = np.where(np.arange(LANES) % 32 < 16, -1.0, 1.0).astype(np.float32)
    return jnp.cos(ang), jnp.sin(ang) * sign


_EVEN_GROUPS = (
    (0, A_Q, 0, True, QK_SCALE),
    (A_Q, A_KV, A_Q, True, 1.0),
    (A_Q + A_KV, A_KV, None, False, 1.0),
    (A_Q + 2 * A_KV, B_W, None, False, QK_SCALE),
    (A_Q + 2 * A_KV + B_W, B_W, None, False, 1.0),
    (A_Q + 2 * A_KV + 2 * B_W, B_W, None, False, 1.0),
)
_EVEN_GROUPS_CTX = tuple((c0, w, g0, False, s) for (c0, w, g0, _, s) in _EVEN_GROUPS)
_ODD_GROUPS = (
    (0, C_Q, None, True, QK_SCALE),
    (C_Q, C_KV, None, True, 1.0),
    (C_Q + C_KV, C_KV, None, False, 1.0),
)
_ODD_GROUPS_CTX = tuple((c0, w, g0, False, s) for (c0, w, g0, _, s) in _ODD_GROUPS)

LAT_TM = 512


def kernel(x, c, ctx, c_ctx, ada_w, ada_b, ln_g, ln_b, ev_w_in, ev_w_out, ev_q_gain, ev_k_gain, ev_rpb,
           od_w_in, od_w_out, od_sink, ffn_w_up, ffn_conv_w, ffn_conv_b, ffn_w_down):
    n = x.shape[1]
    nc = ctx.shape[1]
    rows = n // GRID_W
    x_lat = x[0]
    x_ctx = ctx[0]

    cvec =jnp.zeros((8, D_MODEL), F32).at[0].set(c[0]).at[1].set(c_ctx)
    mods = _modulation(cvec, ada_w, ada_b)

    cos, sin = _rope_tables(n)
    cos_c, sin_c = cos[:nc], sin[:nc]
    lane = np.arange(LANES)
    gmat = jnp.asarray((lane[:, None] // HEAD_DIM) == (lane[None, :] // HEAD_DIM), BF16)
    no_sink = jnp.full((C_HEADS,), NEG, F32)

    for l in range(DEPTH):
        i = l // 2
        ctx_out = l < DEPTH - 1
        m_lat = mods[l, 0].reshape(6, D_MODEL)
        m_ctx = mods[l, 1].reshape(6, D_MODEL)
        g0, b0 = ln_g[l, 0][None], ln_b[l, 0][None]
        g1, b1 = ln_g[l, 1][None], ln_b[l, 1][None]
        if l % 2 == 0:
            w_in = ev_w_in[i]
            w_in = jnp.concatenate([w_in[:, :A_Q][:, _PERM_A], w_in[:, A_Q:]], axis=1).astype(BF16)
            w_out = ev_w_out[i]
            w_out = jnp.concatenate([w_out[:A_Q][_PERM_A], w_out[A_Q:]], axis=0).astype(BF16)
            gains = jnp.concatenate([jnp.tile(ev_q_gain[i], A_HEADS), jnp.tile(ev_k_gain[i], A_KV_HEADS)])[None]
            qa, ka, va, qb, kb, vb = _project(x_lat, m_lat, w_in, gains, cos, sin, gmat, _EVEN_GROUPS, LAT_TM)
            qa_c, ka_c, va_c, qb_c, kb_c, vb_c = _project(x_ctx, m_ctx, w_in, gains, cos_c, sin_c, gmat,
                                                           _EVEN_GROUPS_CTX, nc)
            ya = _global_attention(qa, ka, va, ka_c, va_c, 512, 1024)
            bias = _nbr_bias_tables(ev_rpb[i] * LOG2E, rows)
            yb = _nbr_attention(qb, kb, vb, kb_c, vb_c, bias)
            x_lat = _out_project(x_lat, ya, yb, 0, w_out, m_lat, g0, b0, LAT_TM)
            if ctx_out:
                ya_c = _ctx_attention(qa_c, ka_c, va_c, no_sink, 0)
                yb_c = _ctx_attention(qb_c, kb_c, vb_c, no_sink, 1)
                x_ctx = _out_project(x_ctx, ya_c, yb_c, 0, w_out, m_ctx, g0, b0, nc)
        else:
            w_in = od_w_in[i]
            w_in = jnp.concatenate([w_in[:, :C_Q][:, _PERM_C], w_in[:, C_Q:]], axis=1).astype(BF16)
            w_out = od_w_out[i][_PERM_C].astype(BF16)
            sink = od_sink[i] * LOG2E
            gains = jnp.ones((1, A_Q + A_KV), F32)
            q, k, v = _project(x_lat, m_lat, w_in, gains, cos, sin, gmat, _ODD_GROUPS, LAT_TM)
            q_c, k_c, v_c = _project(x_ctx, m_ctx, w_in, gains, cos_c, sin_c, gmat, _ODD_GROUPS_CTX, nc)
            y = _window_attention(q, k, v, k_c, v_c, sink)
            x_lat = _out_project(x_lat, y, y, 1, w_out, m_lat, g0, b0, LAT_TM)
            if ctx_out:
                y_c = _ctx_attention(q_c, k_c, v_c, sink, 0)
                x_ctx = _out_project(x_ctx, y_c, y_c, 1, w_out, m_ctx, g0, b0, nc)
        w_up = ffn_w_up[l].astype(BF16)
        w_down = ffn_w_down[l].astype(BF16)
        cb = ffn_conv_b[l][None]
        x_lat = _conv_ffn(x_lat, m_lat, w_up, ffn_conv_w[l], cb, w_down, g1, b1, LAT_TM)
        if ctx_out:
            x_ctx = _conv_ffn(x_ctx, m_ctx, w_up, ffn_conv_w[l], cb, w_down, g1, b1, nc)
    return x_lat[None]
```

```python
import functools

import numpy as np
import jax
import jax.numpy as jnp
from jax import lax
from jax.experimental import pallas as pl
from jax.experimental.pallas import tpu as pltpu

D_MODEL = 1024
DEPTH = 4
GRID_W = 64
HEAD_DIM = 64
A_HEADS = 8
A_KV_HEADS = 2
B_HEADS = 8
NB_KH = 8
NB_KW = 16
C_HEADS = 16
C_KV_HEADS = 2
C_WINDOW = 128
D_FF = 2816
ROPE_THETA = 10000.0
LN_EPS = 1e-5
RMS_EPS = 1e-6
NEG = -1e30
DN_ALPHA = (2 * DEPTH) ** 0.25
A_Q = A_HEADS * HEAD_DIM
A_KV = A_KV_HEADS * HEAD_DIM
B_W = B_HEADS * HEAD_DIM
C_Q = C_HEADS * HEAD_DIM
C_KV = C_KV_HEADS * HEAD_DIM
LOG2E = 1.4426950408889634
QK_SCALE = HEAD_DIM ** -0.5 * LOG2E

LANES = 128
HALO = 8
VMEM_LIMIT = 52 * 1024 * 1024

BF16 = jnp.bfloat16
F32 = jnp.float32

_NT = (((1,), (1,)), ((), ()))


def _params(sem):
    return pltpu.CompilerParams(dimension_semantics=sem, vmem_limit_bytes=VMEM_LIMIT)


def _full(shape):
    return pl.BlockSpec(shape, lambda *_: (0,) * len(shape))


def _lane_lo(shape):
    return lax.broadcasted_iota(jnp.int32, shape, len(shape) - 1) < HEAD_DIM


def _mod_kernel(c_ref, w_ref, b_ref, o_ref):
    s = c_ref[...]
    s = s * jax.nn.sigmoid(s)
    o_ref[0] = jnp.dot(s, w_ref[0], preferred_element_type=F32, precision=lax.Precision.HIGHEST) + b_ref[0]


def _modulation(cvec, ada_w, ada_b):
    nb = 6
    return pl.pallas_call(
        _mod_kernel,
        grid=(DEPTH, nb),
        in_specs=[pl.BlockSpec((8, D_MODEL), lambda l, j: (0, 0)),
                  pl.BlockSpec((1, D_MODEL, D_MODEL), lambda l, j: (l, 0, j)),
                  pl.BlockSpec((1, 1, D_MODEL), lambda l, j: (l, 0, j))],
        out_specs=pl.BlockSpec((1, 8, D_MODEL), lambda l, j: (l, 0, j)),
        out_shape=jax.ShapeDtypeStruct((DEPTH, 8, 6 * D_MODEL), F32),
        compiler_params=_params(("arbitrary", "arbitrary")),
        name="modulation",
    )(cvec, ada_w, ada_b.reshape(DEPTH, 1, 6 * D_MODEL))


def _group_sumsq(z, gmat):
    x2 = z * z
    hi = x2.astype(BF16)
    lo = (x2 - hi.astype(F32)).astype(BF16)
    return (jnp.dot(hi, gmat, preferred_element_type=F32) + jnp.dot(lo, gmat, preferred_element_type=F32))


def _proj_kernel(groups, x_ref, mod_ref, w_ref, gain_ref, cos_ref, sin_ref, gmat_ref, *out_refs):
    shift = mod_ref[0:1, :]
    scale = mod_ref[1:2, :]
    h = (x_ref[...] * (1.0 + scale) + shift).astype(BF16)
    cos = cos_ref[...]
    sin = sin_ref[...]
    gmat = gmat_ref[...]
    first = lax.broadcasted_iota(jnp.int32, cos.shape, 1) % 32 < 16
    for (c0, width, g0, rope, qscale), o_ref in zip(groups, out_refs):
        z = jnp.dot(h, w_ref[:, c0:c0 + width], preferred_element_type=F32)
        for b in range(width // LANES):
            zb = z[:, b * LANES:(b + 1) * LANES]
            if g0 is not None:
                ms = _group_sumsq(zb, gmat) * (1.0 / HEAD_DIM)
                zb = zb * lax.rsqrt(ms + RMS_EPS) * gain_ref[:, g0 + b * LANES:g0 + (b + 1) * LANES]
            if rope:
                rot = jnp.where(first, pltpu.roll(zb, LANES - 16, 1), pltpu.roll(zb, 16, 1))
                zb = zb * cos + rot * sin
            if qscale != 1.0:
                zb = zb * qscale
            o_ref[:, b * LANES:(b + 1) * LANES] = zb.astype(BF16)


def _project(x, mod, w, gains, cos, sin, gmat, groups, tm):
    n = x.shape[0]
    win = w.shape[1]
    out_shape = [jax.ShapeDtypeStruct((n, g[1]), BF16) for g in groups]
    out_specs = [pl.BlockSpec((tm, g[1]), lambda i: (i, 0)) for g in groups]
    return pl.pallas_call(
        functools.partial(_proj_kernel, groups),
        grid=(n // tm,),
        in_specs=[pl.BlockSpec((tm, D_MODEL), lambda i: (i, 0)),
                  _full((6, D_MODEL)),
                  _full((D_MODEL, win)),
                  _full(gains.shape),
                  pl.BlockSpec((tm, LANES), lambda i: (i, 0)),
                  pl.BlockSpec((tm, LANES), lambda i: (i, 0)),
                  _full((LANES, LANES))],
        out_specs=out_specs,
        out_shape=out_shape,
        compiler_params=_params(("parallel",)),
        name="qkv_project",
    )(x, mod, w, gains, cos, sin, gmat)


GA_REBASE = 24.0


def _with_unit_lane(x):
    lane = lax.broadcasted_iota(jnp.int32, x.shape, 1)
    xf = x.astype(F32)
    x_lo = jnp.where(lane < HEAD_DIM, xf, jnp.where(lane == HEAD_DIM, 1.0, 0.0))
    x_hi = jnp.where(lane >= HEAD_DIM, xf, jnp.where(lane == 0, 1.0, 0.0))
    return x_lo.astype(x.dtype), x_hi.astype(x.dtype)


def _global_attn_kernel(q_ref, kc_ref, vc_ref, k_ref, v_ref, o_ref, qx_ref, c_ref, acc_ref, pv_ref):
    kk = pl.program_id(1)
    tq = q_ref.shape[0]
    nblk = q_ref.shape[1] // LANES
    nh = 2 * nblk
    lane = lax.broadcasted_iota(jnp.int32, (tq, LANES), 1)

    def rounded(x):
        return x.astype(BF16).astype(F32)

    def exact_update(kx, vx, first):
        for h in range(nh):
            half = h % 2
            sp = lax.dot_general(qx_ref[h], kx[half], _NT, preferred_element_type=F32)
            mb = jnp.max(sp, axis=-1, keepdims=True)
            c_old = c_ref[h]
            c_new = rounded(mb) if first else jnp.maximum(c_old, rounded(c_old + mb))
            d = c_new - c_old
            pv = jnp.dot(jnp.exp2(sp - d).astype(BF16), vx[half], preferred_element_type=F32)
            acc_ref[h] = pv if first else acc_ref[h] * jnp.exp2(-d) + pv
            c_ref[h] = c_new
            unit = HEAD_DIM if half == 0 else 0
            qx_ref[h] = jnp.where(lane == unit, -c_new, qx_ref[h].astype(F32)).astype(BF16)

    def speculative_block(kx, vx):
        top = jnp.full((8, LANES), NEG, F32)
        for h in range(nh):
            half = h % 2
            sp = lax.dot_general(qx_ref[h], kx[half], _NT, preferred_element_type=F32)
            cm = sp[:, 0:LANES]
            for j in range(1, sp.shape[1] // LANES):
                cm = jnp.maximum(cm, sp[:, j * LANES:(j + 1) * LANES])
            top = jnp.maximum(top, jnp.max(cm.reshape(tq // 8, 8, LANES), axis=0))
            pv_ref[h] = jnp.dot(jnp.exp2(sp).astype(BF16), vx[half], preferred_element_type=F32)
        return jnp.max(top)

    @pl.when(kk == 0)
    def _():
        lo = _lane_lo((tq, LANES))
        for j in range(nblk):
            qj = q_ref[:, j * LANES:(j + 1) * LANES]
            zero = jnp.zeros_like(qj)
            qx_ref[2 * j] = jnp.where(lo, qj, zero)
            qx_ref[2 * j + 1] = jnp.where(lo, zero, qj)
        c_ref[...] = jnp.zeros(c_ref.shape, F32)
        exact_update(_with_unit_lane(kc_ref[...]), _with_unit_lane(vc_ref[...]), True)

    kx = _with_unit_lane(k_ref[...])
    vx = _with_unit_lane(v_ref[...])
    top = speculative_block(kx, vx)

    @pl.when(top <= GA_REBASE)
    def _():
        acc_ref[...] += pv_ref[...]

    @pl.when(top > GA_REBASE)
    def _():
        exact_update(kx, vx, False)

    @pl.when(kk == pl.num_programs(1) - 1)
    def _():
        lo = _lane_lo((tq, LANES))
        for j in range(nblk):
            a_lo = acc_ref[2 * j]
            a_hi = acc_ref[2 * j + 1]
            o_lo = a_lo / a_lo[:, HEAD_DIM:HEAD_DIM + 1]
            o_hi = a_hi / a_hi[:, 0:1]
            o_ref[:, j * LANES:(j + 1) * LANES] = jnp.where(lo, o_lo, o_hi).astype(o_ref.dtype)


def _global_attention(q, k, v, kc, vc, tq, tk):
    n, qw = q.shape
    nc = kc.shape[0]
    nh = 2 * (qw // LANES)
    return pl.pallas_call(
        _global_attn_kernel,
        grid=(n // tq, n // tk),
        in_specs=[pl.BlockSpec((tq, qw), lambda i, j: (i, 0)),
                  _full((nc, LANES)), _full((nc, LANES)),
                  pl.BlockSpec((tk, LANES), lambda i, j: (j, 0)),
                  pl.BlockSpec((tk, LANES), lambda i, j: (j, 0))],
        out_specs=pl.BlockSpec((tq, qw), lambda i, j: (i, 0)),
        out_shape=jax.ShapeDtypeStruct((n, qw), BF16),
        scratch_shapes=[pltpu.VMEM((nh, tq, LANES), BF16),
                        pltpu.VMEM((nh, tq, 1), F32),
                        pltpu.VMEM((nh, tq, LANES), F32),
                        pltpu.VMEM((nh, tq, LANES), F32)],
        compiler_params=_params(("parallel", "arbitrary")),
        name="global_attention",
    )(q, kc, vc, k, v)


NB_QROWS = 8
NB_KROWS = NB_QROWS + NB_KH


def _nbr_attn_kernel(q_ref, k_ref, v_ref, kc_ref, vc_ref, bias_ref, o_ref):
    i = pl.program_id(1)
    rows = k_ref.shape[0] // GRID_W
    r0 = i * NB_QROWS
    ks = jnp.clip(r0 - NB_KH // 2, 0, rows - NB_KROWS) * GRID_W
    ks = pl.multiple_of(ks, GRID_W)
    kw = k_ref[pl.ds(ks, NB_KROWS * GRID_W), :]
    vw = v_ref[pl.ds(ks, NB_KROWS * GRID_W), :]
    kc = kc_ref[...]
    vc = vc_ref[...]
    q = q_ref[...]
    lo = _lane_lo(q.shape)
    zero = jnp.zeros_like(q)
    outs = []
    for half in range(2):
        qm = jnp.where(lo, q, zero) if half == 0 else jnp.where(lo, zero, q)
        s_nb = lax.dot_general(qm, kw, _NT, preferred_element_type=F32) + bias_ref[half, 0]
        s_cx = lax.dot_general(qm, kc, _NT, preferred_element_type=F32)
        m = jnp.maximum(jnp.max(s_nb, axis=-1, keepdims=True), jnp.max(s_cx, axis=-1, keepdims=True))
        p_nb = jnp.exp2(s_nb - m)
        p_cx = jnp.exp2(s_cx - m)
        den = jnp.sum(p_nb, axis=-1, keepdims=True) + jnp.sum(p_cx, axis=-1, keepdims=True)
        o = (jnp.dot(p_nb.astype(BF16), vw, preferred_element_type=F32)
             + jnp.dot(p_cx.astype(BF16), vc, preferred_element_type=F32))
        outs.append(o / den)
    o_ref[...] = jnp.where(lo, outs[0], outs[1]).astype(o_ref.dtype)


def _nbr_bias_tables(rpb, rows):
    nq = rows // NB_QROWS
    nh = rpb.shape[0]
    col = np.arange(GRID_W)
    cs = np.clip(col - NB_KW // 2, 0, GRID_W - NB_KW)
    col_ok = (col[None, :] >= cs[:, None]) & (col[None, :] < cs[:, None] + NB_KW)
    dcol = col[None, :] - col[:, None] + NB_KW - 1
    sel_col = (dcol[:, :, None] == np.arange(2 * NB_KW - 1)) & col_ok[:, :, None]
    qr = np.arange(NB_QROWS)
    kr = np.arange(NB_KROWS)
    sel_row = np.zeros((3, NB_QROWS, NB_KROWS, 2 * NB_KH - 1), bool)
    for v, blk in enumerate((0, 1, nq - 1)):
        r0 = blk * NB_QROWS
        k0 = int(np.clip(r0 - NB_KH // 2, 0, rows - NB_KROWS))
        qa = r0 + qr
        ka = k0 + kr
        rs = np.clip(qa - NB_KH // 2, 0, rows - NB_KH)
        row_ok = (ka[None, :] >= rs[:, None]) & (ka[None, :] < rs[:, None] + NB_KH)
        drow = ka[None, :] - qa[:, None] + NB_KH - 1
        sel_row[v] = (drow[:, :, None] == np.arange(2 * NB_KH - 1)) & row_ok[:, :, None]
    hi = lax.Precision.HIGHEST
    toeplitz = jnp.einsum('qkj,hdj->hdqk', jnp.asarray(sel_col, F32), rpb, precision=hi)
    bias = jnp.einsum('vrsd,hdqk->hvrqsk', jnp.asarray(sel_row, F32), toeplitz, precision=hi)
    ok = sel_row.any(-1)[:, :, None, :, None] & col_ok[None, None, :, None, :]
    bias = jnp.where(jnp.asarray(ok)[None], bias, NEG)
    return bias.reshape(nh, 3, NB_QROWS * GRID_W, NB_KROWS * GRID_W)


def _nbr_attention(q, k, v, kc, vc, bias):
    n, qw = q.shape
    nc = kc.shape[0]
    tq = NB_QROWS * GRID_W
    nq = n // tq
    npair = qw // LANES

    def bias_map(p, i):
        return (p, jnp.where(i == 0, 0, jnp.where(i == nq - 1, 2, 1)), 0, 0)

    return pl.pallas_call(
        _nbr_attn_kernel,
        grid=(npair, nq),
        in_specs=[pl.BlockSpec((tq, LANES), lambda p, i: (i, p)),
                  pl.BlockSpec((n, LANES), lambda p, i: (0, p)),
                  pl.BlockSpec((n, LANES), lambda p, i: (0, p)),
                  pl.BlockSpec((nc, LANES), lambda p, i: (0, p)),
                  pl.BlockSpec((nc, LANES), lambda p, i: (0, p)),
                  pl.BlockSpec((2, 1, tq, NB_KROWS * GRID_W), bias_map)],
        out_specs=pl.BlockSpec((tq, LANES), lambda p, i: (i, p)),
        out_shape=jax.ShapeDtypeStruct((n, qw), BF16),
        compiler_params=_params(("parallel", "arbitrary")),
        name="neighbourhood_attention",
    )(q, k, v, kc, vc, bias)


WIN_TQ = 256
WIN_SPAN = WIN_TQ + 2 * C_WINDOW


def _window_attn_kernel(sink_ref, q_ref, k_ref, v_ref, kc_ref, vc_ref, o_ref):
    i = pl.program_id(0)
    n = k_ref.shape[0]
    tq = q_ref.shape[0]
    nblk = q_ref.shape[1] // LANES
    q0 = i * tq
    ks = pl.multiple_of(jnp.clip(q0 - C_WINDOW, 0, n - WIN_SPAN), C_WINDOW)
    kw = k_ref[pl.ds(ks, WIN_SPAN), :]
    vw = v_ref[pl.ds(ks, WIN_SPAN), :]
    kc = kc_ref[...]
    vc = vc_ref[...]
    qpos = q0 + lax.broadcasted_iota(jnp.int32, (tq, WIN_SPAN), 0)
    kpos = ks + lax.broadcasted_iota(jnp.int32, (tq, WIN_SPAN), 1)
    valid = jnp.abs(kpos - qpos) <= C_WINDOW
    lo = _lane_lo((tq, LANES))
    for j in range(nblk):
        q = q_ref[:, j * LANES:(j + 1) * LANES]
        zero = jnp.zeros_like(q)
        outs = []
        for half in range(2):
            qm = jnp.where(lo, q, zero) if half == 0 else jnp.where(lo, zero, q)
            sink = sink_ref[j + nblk * half]
            s_w = jnp.where(valid, lax.dot_general(qm, kw, _NT, preferred_element_type=F32), NEG)
            s_c = lax.dot_general(qm, kc, _NT, preferred_element_type=F32)
            m = jnp.maximum(jnp.maximum(jnp.max(s_w, axis=-1, keepdims=True),
                                        jnp.max(s_c, axis=-1, keepdims=True)), sink)
            p_w = jnp.exp2(s_w - m)
            p_c = jnp.exp2(s_c - m)
            den = (jnp.sum(p_w, axis=-1, keepdims=True) + jnp.sum(p_c, axis=-1, keepdims=True)
                   + jnp.exp2(sink - m))
            o = (jnp.dot(p_w.astype(BF16), vw, preferred_element_type=F32)
                 + jnp.dot(p_c.astype(BF16), vc, preferred_element_type=F32))
            outs.append(o / den)
        o_ref[:, j * LANES:(j + 1) * LANES] = jnp.where(lo, outs[0], outs[1]).astype(o_ref.dtype)


def _window_attention(q, k, v, kc, vc, sink):
    n, qw = q.shape
    nc = kc.shape[0]
    return pl.pallas_call(
        _window_attn_kernel,
        grid=(n // WIN_TQ,),
        in_specs=[pl.BlockSpec(memory_space=pltpu.SMEM),
                  pl.BlockSpec((WIN_TQ, qw), lambda i: (i, 0)),
                  _full((n, LANES)), _full((n, LANES)),
                  _full((nc, LANES)), _full((nc, LANES))],
        out_specs=pl.BlockSpec((WIN_TQ, qw), lambda i: (i, 0)),
        out_shape=jax.ShapeDtypeStruct((n, qw), BF16),
        compiler_params=_params(("parallel",)),
        name="window_attention",
    )(sink, q, k, v, kc, vc)


def _ctx_attn_kernel(sink_ref, q_ref, k_ref, v_ref, o_ref):
    j = pl.program_id(0)
    nblk = pl.num_programs(0)
    q = q_ref[...]
    k = k_ref[...]
    v = v_ref[...]
    lo = _lane_lo(q.shape)
    zero = jnp.zeros_like(q)
    outs = []
    for half in range(2):
        qm = jnp.where(lo, q, zero) if half == 0 else jnp.where(lo, zero, q)
        sink = sink_ref[j + nblk * half]
        s = lax.dot_general(qm, k, _NT, preferred_element_type=F32)
        m = jnp.maximum(jnp.max(s, axis=-1, keepdims=True), sink)
        p = jnp.exp2(s - m)
        den = jnp.sum(p, axis=-1, keepdims=True) + jnp.exp2(sink - m)
        outs.append(jnp.dot(p.astype(BF16), v, preferred_element_type=F32) / den)
    o_ref[...] = jnp.where(lo, outs[0], outs[1]).astype(o_ref.dtype)


def _ctx_attention(q, k, v, sink, kv_per_block):
    c, qw = q.shape
    nblk = qw // LANES
    return pl.pallas_call(
        _ctx_attn_kernel,
        grid=(nblk,),
        in_specs=[pl.BlockSpec(memory_space=pltpu.SMEM),
                  pl.BlockSpec((c, LANES), lambda j: (0, j)),
                  pl.BlockSpec((c, LANES), lambda j: (0, j * kv_per_block)),
                  pl.BlockSpec((c, LANES), lambda j: (0, j * kv_per_block))],
        out_specs=pl.BlockSpec((c, LANES), lambda j: (0, j)),
        out_shape=jax.ShapeDtypeStruct((c, qw), BF16),
        compiler_params=_params(("parallel",)),
        name="context_attention",
    )(sink, q, k, v)


def _residual_layer_norm(x, f, gate, g, b):
    z = DN_ALPHA * x + gate * f
    mu = jnp.mean(z, axis=-1, keepdims=True)
    zc = z - mu
    var = jnp.mean(zc * zc, axis=-1, keepdims=True)
    return zc * lax.rsqrt(var + LN_EPS) * g + b


def _outproj_kernel(x_ref, y1_ref, y2_ref, w1_ref, w2_ref, mod_ref, g_ref, b_ref, o_ref):
    f = (jnp.dot(y1_ref[...], w1_ref[...], preferred_element_type=F32)
         + jnp.dot(y2_ref[...], w2_ref[...], preferred_element_type=F32))
    o_ref[...] = _residual_layer_norm(x_ref[...], f, mod_ref[2:3, :], g_ref[...], b_ref[...])


def _out_project(x, y1, y2, c2, w, mod, g, b, tm):
    n = x.shape[0]
    half = D_MODEL // 2
    return pl.pallas_call(
        _outproj_kernel,
        grid=(n // tm,),
        in_specs=[pl.BlockSpec((tm, D_MODEL), lambda i: (i, 0)),
                  pl.BlockSpec((tm, half), lambda i: (i, 0)),
                  pl.BlockSpec((tm, half), lambda i: (i, c2)),
                  pl.BlockSpec((half, D_MODEL), lambda i: (0, 0)),
                  pl.BlockSpec((half, D_MODEL), lambda i: (1, 0)),
                  _full((6, D_MODEL)), _full((1, D_MODEL)), _full((1, D_MODEL))],
        out_specs=pl.BlockSpec((tm, D_MODEL), lambda i: (i, 0)),
        out_shape=jax.ShapeDtypeStruct((n, D_MODEL), F32),
        compiler_params=_params(("parallel",)),
        name="out_project_ln",
    )(x, y1, y2, w, w, mod, g, b)


FFN_CHUNKS = 2
FFN_FC = D_FF // FFN_CHUNKS


def _ffn_kernel(x_ref, xp_ref, xn_ref, mod_ref, wa_ref, wg_ref, cwa_ref, cwg_ref, cba_ref, cbg_ref, wd_ref,
                g_ref, b_ref, o_ref, h_ref, ua_ref, ug_ref, acc_ref):
    i = pl.program_id(0)
    c = pl.program_id(1)
    tm = x_ref.shape[0]

    @pl.when(c == 0)
    def _():
        shift = mod_ref[3:4, :]
        scale = 1.0 + mod_ref[4:5, :]
        hp = jnp.where(i > 0, xp_ref[...] * scale + shift, 0.0)
        hn = jnp.where(i < pl.num_programs(0) - 1, xn_ref[...] * scale + shift, 0.0)
        hm = x_ref[...] * scale + shift
        h_ref[...] = jnp.concatenate([hp, hm, hn], axis=0).astype(BF16)
        acc_ref[...] = jnp.zeros(acc_ref.shape, F32)

    h = h_ref[...]
    ua_ref[...] = jnp.dot(h, wa_ref[...], preferred_element_type=F32)
    ug_ref[...] = jnp.dot(h, wg_ref[...], preferred_element_type=F32)

    def conv(u_ref, cw_ref, cb_ref):
        return (cb_ref[...]
                + u_ref[HALO - 1:HALO - 1 + tm, :] * cw_ref[0:1, :]
                + u_ref[HALO:HALO + tm, :] * cw_ref[1:2, :]
                + u_ref[HALO + 1:HALO + 1 + tm, :] * cw_ref[2:3, :])

    a = conv(ua_ref, cwa_ref, cba_ref)
    gt = conv(ug_ref, cwg_ref, cbg_ref)
    act = (gt * jax.nn.sigmoid(gt) * a).astype(BF16)
    acc_ref[...] += jnp.dot(act, wd_ref[...], preferred_element_type=F32)

    @pl.when(c == pl.num_programs(1) - 1)
    def _():
        o_ref[...] = _residual_layer_norm(x_ref[...], acc_ref[...], mod_ref[5:6, :], g_ref[...], b_ref[...])


def _conv_ffn(x, mod, w_up, conv_w, conv_b, w_down, g, b, tm):
    n = x.shape[0]
    nrow = n // tm
    nh = n // HALO
    per = tm // HALO
    nc = FFN_CHUNKS
    return pl.pallas_call(
        _ffn_kernel,
        grid=(nrow, nc),
        in_specs=[pl.BlockSpec((tm, D_MODEL), lambda i, c: (i, 0)),
                  pl.BlockSpec((HALO, D_MODEL), lambda i, c: (jnp.maximum(i * per - 1, 0), 0)),
                  pl.BlockSpec((HALO, D_MODEL), lambda i, c: (jnp.minimum((i + 1) * per, nh - 1), 0)),
                  _full((6, D_MODEL)),
                  pl.BlockSpec((D_MODEL, FFN_FC), lambda i, c: (0, c)),
                  pl.BlockSpec((D_MODEL, FFN_FC), lambda i, c: (0, nc + c)),
                  pl.BlockSpec((3, FFN_FC), lambda i, c: (0, c)),
                  pl.BlockSpec((3, FFN_FC), lambda i, c: (0, nc + c)),
                  pl.BlockSpec((1, FFN_FC), lambda i, c: (0, c)),
                  pl.BlockSpec((1, FFN_FC), lambda i, c: (0, nc + c)),
                  pl.BlockSpec((FFN_FC, D_MODEL), lambda i, c: (c, 0)),
                  _full((1, D_MODEL)), _full((1, D_MODEL))],
        out_specs=pl.BlockSpec((tm, D_MODEL), lambda i, c: (i, 0)),
        out_shape=jax.ShapeDtypeStruct((n, D_MODEL), F32),
        scratch_shapes=[pltpu.VMEM((tm + 2 * HALO, D_MODEL), BF16),
                        pltpu.VMEM((tm + 2 * HALO, FFN_FC), F32),
                        pltpu.VMEM((tm + 2 * HALO, FFN_FC), F32),
                        pltpu.VMEM((tm, D_MODEL), F32)],
        compiler_params=_params(("parallel", "arbitrary")),
        name="conv_ffn_ln",
    )(x, x, x, mod, w_up, w_up, conv_w, conv_w, conv_b, conv_b, w_down, g, b)


def _pair_perm(n_heads, kv_heads):
    per = n_heads // kv_heads
    order = []
    for j in range(per):
        for g in range(kv_heads):
            order.append(g * per + j)
    cols = np.concatenate([np.arange(h * HEAD_DIM, (h + 1) * HEAD_DIM) for h in order])
    return cols


_PERM_A = _pair_perm(A_HEADS, A_KV_HEADS)
_PERM_C = _pair_perm(C_HEADS, C_KV_HEADS)


def _rope_tables(n):
    t = np.arange(n)
    row = (t // GRID_W).astype(np.float32)
    col = (t % GRID_W).astype(np.float32)
    half = HEAD_DIM // 2
    inv = jnp.asarray(ROPE_THETA, F32) ** (-jnp.arange(0, half, 2, dtype=F32) / half)
    ang_r = jnp.asarray(row)[:, None] * inv
    ang_c = jnp.asarray(col)[:, None] * inv
    ang = jnp.concatenate([ang_r, ang_r, ang_c, ang_c], -1)
    ang = jnp.concatenate([ang, ang], -1)
    sign = np.where(np.arange(LANES) % 32 < 16, -1.0, 1.0).astype(np.float32)
    return jnp.cos(ang), jnp.sin(ang) * sign


_EVEN_GROUPS = (
    (0, A_Q, 0, True, QK_SCALE),
    (A_Q, A_KV, A_Q, True, 1.0),
    (A_Q + A_KV, A_KV, None, False, 1.0),
    (A_Q + 2 * A_KV, B_W, None, False, QK_SCALE),
    (A_Q + 2 * A_KV + B_W, B_W, None, False, 1.0),
    (A_Q + 2 * A_KV + 2 * B_W, B_W, None, False, 1.0),
)
_EVEN_GROUPS_CTX = tuple((c0, w, g0, False, s) for (c0, w, g0, _, s) in _EVEN_GROUPS)
_ODD_GROUPS = (
    (0, C_Q, None, True, QK_SCALE),
    (C_Q, C_KV, None, True, 1.0),
    (C_Q + C_KV, C_KV, None, False, 1.0),
)
_ODD_GROUPS_CTX = tuple((c0, w, g0, False, s) for (c0, w, g0, _, s) in _ODD_GROUPS)

LAT_TM = 512


def kernel(x, c, ctx, c_ctx, ada_w, ada_b, ln_g, ln_b, ev_w_in, ev_w_out, ev_q_gain, ev_k_gain, ev_rpb,
           od_w_in, od_w_out, od_sink, ffn_w_up, ffn_conv_w, ffn_conv_b, ffn_w_down):
    n = x.shape[1]
    nc = ctx.shape[1]
    rows = n // GRID_W
    x_lat = x[0]
    x_ctx = ctx[0]

    cvec =jnp.zeros((8, D_MODEL), F32).at[0].set(c[0]).at[1].set(c_ctx)
    mods = _modulation(cvec, ada_w, ada_b)

    cos, sin = _rope_tables(n)
    cos_c, sin_c = cos[:nc], sin[:nc]
    lane = np.arange(LANES)
    gmat = jnp.asarray((lane[:, None] // HEAD_DIM) == (lane[None, :] // HEAD_DIM), BF16)
    no_sink = jnp.full((C_HEADS,), NEG, F32)

    for l in range(DEPTH):
        i = l // 2
        ctx_out = l < DEPTH - 1
        m_lat = mods[l, 0].reshape(6, D_MODEL)
        m_ctx = mods[l, 1].reshape(6, D_MODEL)
        g0, b0 = ln_g[l, 0][None], ln_b[l, 0][None]
        g1, b1 = ln_g[l, 1][None], ln_b[l, 1][None]
        if l % 2 == 0:
            w_in = ev_w_in[i]
            w_in = jnp.concatenate([w_in[:, :A_Q][:, _PERM_A], w_in[:, A_Q:]], axis=1).astype(BF16)
            w_out = ev_w_out[i]
            w_out = jnp.concatenate([w_out[:A_Q][_PERM_A], w_out[A_Q:]], axis=0).astype(BF16)
            gains = jnp.concatenate([jnp.tile(ev_q_gain[i], A_HEADS), jnp.tile(ev_k_gain[i], A_KV_HEADS)])[None]
            qa, ka, va, qb, kb, vb = _project(x_lat, m_lat, w_in, gains, cos, sin, gmat, _EVEN_GROUPS, LAT_TM)
            qa_c, ka_c, va_c, qb_c, kb_c, vb_c = _project(x_ctx, m_ctx, w_in, gains, cos_c, sin_c, gmat,
                                                           _EVEN_GROUPS_CTX, nc)
            ya = _global_attention(qa, ka, va, ka_c, va_c, 512, 1024)
            bias = _nbr_bias_tables(ev_rpb[i] * LOG2E, rows)
            yb = _nbr_attention(qb, kb, vb, kb_c, vb_c, bias)
            x_lat = _out_project(x_lat, ya, yb, 0, w_out, m_lat, g0, b0, LAT_TM)
            if ctx_out:
                ya_c = _ctx_attention(qa_c, ka_c, va_c, no_sink, 0)
                yb_c = _ctx_attention(qb_c, kb_c, vb_c, no_sink, 1)
                x_ctx = _out_project(x_ctx, ya_c, yb_c, 0, w_out, m_ctx, g0, b0, nc)
        else:
            w_in = od_w_in[i]
            w_in = jnp.concatenate([w_in[:, :C_Q][:, _PERM_C], w_in[:, C_Q:]], axis=1).astype(BF16)
            w_out = od_w_out[i][_PERM_C].astype(BF16)
            sink = od_sink[i] * LOG2E
            gains = jnp.ones((1, A_Q + A_KV), F32)
            q, k, v = _project(x_lat, m_lat, w_in, gains, cos, sin, gmat, _ODD_GROUPS, LAT_TM)
            q_c, k_c, v_c = _project(x_ctx, m_ctx, w_in, gains, cos_c, sin_c, gmat, _ODD_GROUPS_CTX, nc)
            y = _window_attention(q, k, v, k_c, v_c, sink)
            x_lat = _out_project(x_lat, y, y, 1, w_out, m_lat, g0, b0, LAT_TM)
            if ctx_out:
                y_c = _ctx_attention(q_c, k_c, v_c, sink, 0)
                x_ctx = _out_project(x_ctx, y_c, y_c, 1, w_out, m_ctx, g0, b0, nc)
        w_up = ffn_w_up[l].astype(BF16)
        w_down = ffn_w_down[l].astype(BF16)
        cb = ffn_conv_b[l][None]
        x_lat = _conv_ffn(x_lat, m_lat, w_up, ffn_conv_w[l], cb, w_down, g1, b1, LAT_TM)
        if ctx_out:
            x_ctx = _conv_ffn(x_ctx, m_ctx, w_up, ffn_conv_w[l], cb, w_down, g1, b1, nc)
    return x_lat[None]
```

```python
import functools

import numpy as np
import jax
import jax.numpy as jnp
from jax import lax
from jax.experimental import pallas as pl
from jax.experimental.pallas import tpu as pltpu

D_MODEL = 1024
DEPTH = 4
GRID_W = 64
HEAD_DIM = 64
A_HEADS = 8
A_KV_HEADS = 2
B_HEADS = 8
NB_KH = 8
NB_KW = 16
C_HEADS = 16
C_KV_HEADS = 2
C_WINDOW = 128
D_FF = 2816
ROPE_THETA = 10000.0
LN_EPS = 1e-5
RMS_EPS = 1e-6
NEG = -1e30
DN_ALPHA = (2 * DEPTH) ** 0.25
A_Q = A_HEADS * HEAD_DIM
A_KV = A_KV_HEADS * HEAD_DIM
B_W = B_HEADS * HEAD_DIM
C_Q = C_HEADS * HEAD_DIM
C_KV = C_KV_HEADS * HEAD_DIM
LOG2E = 1.4426950408889634
QK_SCALE = HEAD_DIM ** -0.5 * LOG2E

LANES = 128
HALO = 8
VMEM_LIMIT = 52 * 1024 * 1024

BF16 = jnp.bfloat16
F32 = jnp.float32

_NT = (((1,), (1,)), ((), ()))


def _params(sem):
    return pltpu.CompilerParams(dimension_semantics=sem, vmem_limit_bytes=VMEM_LIMIT)


def _full(shape):
    return pl.BlockSpec(shape, lambda *_: (0,) * len(shape))


def _lane_lo(shape):
    return lax.broadcasted_iota(jnp.int32, shape, len(shape) - 1) < HEAD_DIM


def _mod_kernel(c_ref, w_ref, b_ref, o_ref):
    s = c_ref[...]
    s = s * jax.nn.sigmoid(s)
    o_ref[0] = jnp.dot(s, w_ref[0], preferred_element_type=F32, precision=lax.Precision.HIGHEST) + b_ref[0]


def _modulation(cvec, ada_w, ada_b):
    nb = 6
    return pl.pallas_call(
        _mod_kernel,
        grid=(DEPTH, nb),
        in_specs=[pl.BlockSpec((8, D_MODEL), lambda l, j: (0, 0)),
                  pl.BlockSpec((1, D_MODEL, D_MODEL), lambda l, j: (l, 0, j)),
                  pl.BlockSpec((1, 1, D_MODEL), lambda l, j: (l, 0, j))],
        out_specs=pl.BlockSpec((1, 8, D_MODEL), lambda l, j: (l, 0, j)),
        out_shape=jax.ShapeDtypeStruct((DEPTH, 8, 6 * D_MODEL), F32),
        compiler_params=_params(("arbitrary", "arbitrary")),
        name="modulation",
    )(cvec, ada_w, ada_b.reshape(DEPTH, 1, 6 * D_MODEL))


def _group_sumsq(z, gmat):
    x2 = z * z
    hi = x2.astype(BF16)
    lo = (x2 - hi.astype(F32)).astype(BF16)
    return (jnp.dot(hi, gmat, preferred_element_type=F32) + jnp.dot(lo, gmat, preferred_element_type=F32))


def _proj_kernel(groups, x_ref, mod_ref, w_ref, gain_ref, cos_ref, sin_ref, gmat_ref, *out_refs):
    shift = mod_ref[0:1, :]
    scale = mod_ref[1:2, :]
    h = (x_ref[...] * (1.0 + scale) + shift).astype(BF16)
    cos = cos_ref[...]
    sin = sin_ref[...]
    gmat = gmat_ref[...]
    first = lax.broadcasted_iota(jnp.int32, cos.shape, 1) % 32 < 16
    for (c0, width, g0, rope, qscale), o_ref in zip(groups, out_refs):
        z = jnp.dot(h, w_ref[:, c0:c0 + width], preferred_element_type=F32)
        for b in range(width // LANES):
            zb = z[:, b * LANES:(b + 1) * LANES]
            if g0 is not None:
                ms = _group_sumsq(zb, gmat) * (1.0 / HEAD_DIM)
                zb = zb * lax.rsqrt(ms + RMS_EPS) * gain_ref[:, g0 + b * LANES:g0 + (b + 1) * LANES]
            if rope:
                rot = jnp.where(first, pltpu.roll(zb, LANES - 16, 1), pltpu.roll(zb, 16, 1))
                zb = zb * cos + rot * sin
            if qscale != 1.0:
                zb = zb * qscale
            o_ref[:, b * LANES:(b + 1) * LANES] = zb.astype(BF16)


def _project(x, mod, w, gains, cos, sin, gmat, groups, tm):
    n = x.shape[0]
    win = w.shape[1]
    out_shape = [jax.ShapeDtypeStruct((n, g[1]), BF16) for g in groups]
    out_specs = [pl.BlockSpec((tm, g[1]), lambda i: (i, 0)) for g in groups]
    return pl.pallas_call(
        functools.partial(_proj_kernel, groups),
        grid=(n // tm,),
        in_specs=[pl.BlockSpec((tm, D_MODEL), lambda i: (i, 0)),
                  _full((6, D_MODEL)),
                  _full((D_MODEL, win)),
                  _full(gains.shape),
                  pl.BlockSpec((tm, LANES), lambda i: (i, 0)),
                  pl.BlockSpec((tm, LANES), lambda i: (i, 0)),
                  _full((LANES, LANES))],
        out_specs=out_specs,
        out_shape=out_shape,
        compiler_params=_params(("parallel",)),
        name="qkv_project",
    )(x, mod, w, gains, cos, sin, gmat)


GA_LIMIT = 64.0


def _unit_lane_variants(x):
    lane = jnp.arange(LANES)
    one = jnp.ones((), x.dtype)
    zero = jnp.zeros((), x.dtype)
    lo = jnp.where(lane < HEAD_DIM, x, jnp.where(lane == HEAD_DIM, one, zero))
    hi = jnp.where(lane >= HEAD_DIM, x, jnp.where(lane == 0, one, zero))
    return jnp.stack([lo, hi])


def _global_attn_kernel(exact, q_ref, kc_ref, vc_ref, k_ref, v_ref, o_ref, top_ref, qx_ref, c_ref, acc_ref):
    kk = pl.program_id(1)
    tq = q_ref.shape[0]
    nblk = q_ref.shape[1] // LANES
    nh = 2 * nblk
    lane = lax.broadcasted_iota(jnp.int32, (tq, LANES), 1)

    def rounded(x):
        return x.astype(BF16).astype(F32)

    def moving_reference_update(kx_ref, vx_ref, first):
        for h in range(nh):
            half = h % 2
            sp = lax.dot_general(qx_ref[h], kx_ref[half], _NT, preferred_element_type=F32)
            mb = jnp.max(sp, axis=-1, keepdims=True)
            c_old = c_ref[h]
            c_new = rounded(mb) if first else jnp.maximum(c_old, rounded(c_old + mb))
            d = c_new - c_old
            pv = jnp.dot(jnp.exp2(sp - d).astype(BF16), vx_ref[half], preferred_element_type=F32)
            acc_ref[h] = pv if first else acc_ref[h] * jnp.exp2(-d) + pv
            c_ref[h] = c_new
            unit = HEAD_DIM if half == 0 else 0
            qx_ref[h] = jnp.where(lane == unit, -c_new, qx_ref[h].astype(F32)).astype(BF16)

    def fixed_reference_update(kx_ref, vx_ref):
        top = top_ref[...]
        for h in range(nh):
            half = h % 2
            sp = lax.dot_general(qx_ref[h], kx_ref[half], _NT, preferred_element_type=F32)
            cm = sp[:, 0:LANES]
            for j in range(1, sp.shape[1] // LANES):
                cm = jnp.maximum(cm, sp[:, j * LANES:(j + 1) * LANES])
            top = jnp.maximum(top, jnp.max(cm.reshape(tq // 8, 8, LANES), axis=0))
            acc_ref[h] += jnp.dot(jnp.exp2(sp).astype(BF16), vx_ref[half], preferred_element_type=F32)
        top_ref[...] = top

    @pl.when(kk == 0)
    def _():
        lo = _lane_lo((tq, LANES))
        for j in range(nblk):
            qj = q_ref[:, j * LANES:(j + 1) * LANES]
            zero = jnp.zeros_like(qj)
            qx_ref[2 * j] = jnp.where(lo, qj, zero)
            qx_ref[2 * j + 1] = jnp.where(lo, zero, qj)
        c_ref[...] = jnp.zeros(c_ref.shape, F32)
        top_ref[...] = jnp.full(top_ref.shape, NEG, F32)
        moving_reference_update(kc_ref, vc_ref, True)

    if exact:
        moving_reference_update(k_ref, v_ref, False)
    else:
        fixed_reference_update(k_ref, v_ref)

    @pl.when(kk == pl.num_programs(1) - 1)
    def _():
        lo = _lane_lo((tq, LANES))
        for j in range(nblk):
            a_lo = acc_ref[2 * j]
            a_hi = acc_ref[2 * j + 1]
            o_lo = a_lo / a_lo[:, HEAD_DIM:HEAD_DIM + 1]
            o_hi = a_hi / a_hi[:, 0:1]
            o_ref[:, j * LANES:(j + 1) * LANES] = jnp.where(lo, o_lo, o_hi).astype(o_ref.dtype)


def _global_attention(q, k, v, kc, vc, tq, tk):
    n, qw = q.shape
    nc = kc.shape[0]
    nh = 2 * (qw // LANES)
    nq = n // tq
    operands = (q, _unit_lane_variants(kc), _unit_lane_variants(vc), _unit_lane_variants(k), _unit_lane_variants(v))

    def run(exact):
        return pl.pallas_call(
            functools.partial(_global_attn_kernel, exact),
            grid=(nq, n // tk),
            in_specs=[pl.BlockSpec((tq, qw), lambda i, j: (i, 0)),
                      _full((2, nc, LANES)), _full((2, nc, LANES)),
                      pl.BlockSpec((2, tk, LANES), lambda i, j: (0, j, 0)),
                      pl.BlockSpec((2, tk, LANES), lambda i, j: (0, j, 0))],
            out_specs=[pl.BlockSpec((tq, qw), lambda i, j: (i, 0)),
                       pl.BlockSpec((8, LANES), lambda i, j: (i, 0))],
            out_shape=[jax.ShapeDtypeStruct((n, qw), BF16),
                       jax.ShapeDtypeStruct((nq * 8, LANES), F32)],
            scratch_shapes=[pltpu.VMEM((nh, tq, LANES), BF16),
                            pltpu.VMEM((nh, tq, 1), F32),
                            pltpu.VMEM((nh, tq, LANES), F32)],
            compiler_params=_params(("parallel", "arbitrary")),
            name="global_attention_exact" if exact else "global_attention",
        )(*operands)

    y, top = run(False)
    return lax.cond(jnp.max(top) > GA_LIMIT, lambda: run(True)[0], lambda: y)


NB_QROWS = 8
NB_KROWS = NB_QROWS + NB_KH


def _nbr_attn_kernel(q_ref, k_ref, v_ref, kc_ref, vc_ref, bias_ref, o_ref):
    i = pl.program_id(1)
    rows = k_ref.shape[0] // GRID_W
    r0 = i * NB_QROWS
    ks = jnp.clip(r0 - NB_KH // 2, 0, rows - NB_KROWS) * GRID_W
    ks = pl.multiple_of(ks, GRID_W)
    kw = k_ref[pl.ds(ks, NB_KROWS * GRID_W), :]
    vw = v_ref[pl.ds(ks, NB_KROWS * GRID_W), :]
    kc = kc_ref[...]
    vc = vc_ref[...]
    q = q_ref[...]
    lo = _lane_lo(q.shape)
    zero = jnp.zeros_like(q)
    outs = []
    for half in range(2):
        qm = jnp.where(lo, q, zero) if half == 0 else jnp.where(lo, zero, q)
        s_nb = lax.dot_general(qm, kw, _NT, preferred_element_type=F32) + bias_ref[half, 0]
        s_cx = lax.dot_general(qm, kc, _NT, preferred_element_type=F32)
        m = jnp.maximum(jnp.max(s_nb, axis=-1, keepdims=True), jnp.max(s_cx, axis=-1, keepdims=True))
        p_nb = jnp.exp2(s_nb - m)
        p_cx = jnp.exp2(s_cx - m)
        den = jnp.sum(p_nb, axis=-1, keepdims=True) + jnp.sum(p_cx, axis=-1, keepdims=True)
        o = (jnp.dot(p_nb.astype(BF16), vw, preferred_element_type=F32)
             + jnp.dot(p_cx.astype(BF16), vc, preferred_element_type=F32))
        outs.append(o / den)
    o_ref[...] = jnp.where(lo, outs[0], outs[1]).astype(o_ref.dtype)


def _nbr_bias_tables(rpb, rows):
    nq = rows // NB_QROWS
    nh = rpb.shape[0]
    col = np.arange(GRID_W)
    cs = np.clip(col - NB_KW // 2, 0, GRID_W - NB_KW)
    col_ok = (col[None, :] >= cs[:, None]) & (col[None, :] < cs[:, None] + NB_KW)
    dcol = col[None, :] - col[:, None] + NB_KW - 1
    sel_col = (dcol[:, :, None] == np.arange(2 * NB_KW - 1)) & col_ok[:, :, None]
    qr = np.arange(NB_QROWS)
    kr = np.arange(NB_KROWS)
    sel_row = np.zeros((3, NB_QROWS, NB_KROWS, 2 * NB_KH - 1), bool)
    for v, blk in enumerate((0, 1, nq - 1)):
        r0 = blk * NB_QROWS
        k0 = int(np.clip(r0 - NB_KH // 2, 0, rows - NB_KROWS))
        qa = r0 + qr
        ka = k0 + kr
        rs = np.clip(qa - NB_KH // 2, 0, rows - NB_KH)
        row_ok = (ka[None, :] >= rs[:, None]) & (ka[None, :] < rs[:, None] + NB_KH)
        drow = ka[None, :] - qa[:, None] + NB_KH - 1
        sel_row[v] = (drow[:, :, None] == np.arange(2 * NB_KH - 1)) & row_ok[:, :, None]
    hi = lax.Precision.HIGHEST
    toeplitz = jnp.einsum('qkj,hdj->hdqk', jnp.asarray(sel_col, F32), rpb, precision=hi)
    bias = jnp.einsum('vrsd,hdqk->hvrqsk', jnp.asarray(sel_row, F32), toeplitz, precision=hi)
    ok = sel_row.any(-1)[:, :, None, :, None] & col_ok[None, None, :, None, :]
    bias = jnp.where(jnp.asarray(ok)[None], bias, NEG)
    return bias.reshape(nh, 3, NB_QROWS * GRID_W, NB_KROWS * GRID_W)


def _nbr_attention(q, k, v, kc, vc, bias):
    n, qw = q.shape
    nc = kc.shape[0]
    tq = NB_QROWS * GRID_W
    nq = n // tq
    npair = qw // LANES

    def bias_map(p, i):
        return (p, jnp.where(i == 0, 0, jnp.where(i == nq - 1, 2, 1)), 0, 0)

    return pl.pallas_call(
        _nbr_attn_kernel,
        grid=(npair, nq),
        in_specs=[pl.BlockSpec((tq, LANES), lambda p, i: (i, p)),
                  pl.BlockSpec((n, LANES), lambda p, i: (0, p)),
                  pl.BlockSpec((n, LANES), lambda p, i: (0, p)),
                  pl.BlockSpec((nc, LANES), lambda p, i: (0, p)),
                  pl.BlockSpec((nc, LANES), lambda p, i: (0, p)),
                  pl.BlockSpec((2, 1, tq, NB_KROWS * GRID_W), bias_map)],
        out_specs=pl.BlockSpec((tq, LANES), lambda p, i: (i, p)),
        out_shape=jax.ShapeDtypeStruct((n, qw), BF16),
        compiler_params=_params(("parallel", "arbitrary")),
        name="neighbourhood_attention",
    )(q, k, v, kc, vc, bias)


WIN_TQ = 256
WIN_SPAN = WIN_TQ + 2 * C_WINDOW


def _window_attn_kernel(sink_ref, q_ref, k_ref, v_ref, kc_ref, vc_ref, o_ref):
    i = pl.program_id(0)
    n = k_ref.shape[0]
    tq = q_ref.shape[0]
    nblk = q_ref.shape[1] // LANES
    q0 = i * tq
    ks = pl.multiple_of(jnp.clip(q0 - C_WINDOW, 0, n - WIN_SPAN), C_WINDOW)
    kw = k_ref[pl.ds(ks, WIN_SPAN), :]
    vw = v_ref[pl.ds(ks, WIN_SPAN), :]
    kc = kc_ref[...]
    vc = vc_ref[...]
    qpos = q0 + lax.broadcasted_iota(jnp.int32, (tq, WIN_SPAN), 0)
    kpos = ks + lax.broadcasted_iota(jnp.int32, (tq, WIN_SPAN), 1)
    valid = jnp.abs(kpos - qpos) <= C_WINDOW
    lo = _lane_lo((tq, LANES))
    for j in range(nblk):
        q = q_ref[:, j * LANES:(j + 1) * LANES]
        zero = jnp.zeros_like(q)
        outs = []
        for half in range(2):
            qm = jnp.where(lo, q, zero) if half == 0 else jnp.where(lo, zero, q)
            sink = sink_ref[j + nblk * half]
            s_w = jnp.where(valid, lax.dot_general(qm, kw, _NT, preferred_element_type=F32), NEG)
            s_c = lax.dot_general(qm, kc, _NT, preferred_element_type=F32)
            m = jnp.maximum(jnp.maximum(jnp.max(s_w, axis=-1, keepdims=True),
                                        jnp.max(s_c, axis=-1, keepdims=True)), sink)
            p_w = jnp.exp2(s_w - m)
            p_c = jnp.exp2(s_c - m)
            den = (jnp.sum(p_w, axis=-1, keepdims=True) + jnp.sum(p_c, axis=-1, keepdims=True)
                   + jnp.exp2(sink - m))
            o = (jnp.dot(p_w.astype(BF16), vw, preferred_element_type=F32)
                 + jnp.dot(p_c.astype(BF16), vc, preferred_element_type=F32))
            outs.append(o / den)
        o_ref[:, j * LANES:(j + 1) * LANES] = jnp.where(lo, outs[0], outs[1]).astype(o_ref.dtype)


def _window_attention(q, k, v, kc, vc, sink):
    n, qw = q.shape
    nc = kc.shape[0]
    return pl.pallas_call(
        _window_attn_kernel,
        grid=(n // WIN_TQ,),
        in_specs=[pl.BlockSpec(memory_space=pltpu.SMEM),
                  pl.BlockSpec((WIN_TQ, qw), lambda i: (i, 0)),
                  _full((n, LANES)), _full((n, LANES)),
                  _full((nc, LANES)), _full((nc, LANES))],
        out_specs=pl.BlockSpec((WIN_TQ, qw), lambda i: (i, 0)),
        out_shape=jax.ShapeDtypeStruct((n, qw), BF16),
        compiler_params=_params(("parallel",)),
        name="window_attention",
    )(sink, q, k, v, kc, vc)


def _ctx_attn_kernel(sink_ref, q_ref, k_ref, v_ref, o_ref):
    j = pl.program_id(0)
    nblk = pl.num_programs(0)
    q = q_ref[...]
    k = k_ref[...]
    v = v_ref[...]
    lo = _lane_lo(q.shape)
    zero = jnp.zeros_like(q)
    outs = []
    for half in range(2):
        qm = jnp.where(lo, q, zero) if half == 0 else jnp.where(lo, zero, q)
        sink = sink_ref[j + nblk * half]
        s = lax.dot_general(qm, k, _NT, preferred_element_type=F32)
        m = jnp.maximum(jnp.max(s, axis=-1, keepdims=True), sink)
        p = jnp.exp2(s - m)
        den = jnp.sum(p, axis=-1, keepdims=True) + jnp.exp2(sink - m)
        outs.append(jnp.dot(p.astype(BF16), v, preferred_element_type=F32) / den)
    o_ref[...] = jnp.where(lo, outs[0], outs[1]).astype(o_ref.dtype)


def _ctx_attention(q, k, v, sink, kv_per_block):
    c, qw = q.shape
    nblk = qw // LANES
    return pl.pallas_call(
        _ctx_attn_kernel,
        grid=(nblk,),
        in_specs=[pl.BlockSpec(memory_space=pltpu.SMEM),
                  pl.BlockSpec((c, LANES), lambda j: (0, j)),
                  pl.BlockSpec((c, LANES), lambda j: (0, j * kv_per_block)),
                  pl.BlockSpec((c, LANES), lambda j: (0, j * kv_per_block))],
        out_specs=pl.BlockSpec((c, LANES), lambda j: (0, j)),
        out_shape=jax.ShapeDtypeStruct((c, qw), BF16),
        compiler_params=_params(("parallel",)),
        name="context_attention",
    )(sink, q, k, v)


def _residual_layer_norm(x, f, gate, g, b):
    z = DN_ALPHA * x + gate * f
    mu = jnp.mean(z, axis=-1, keepdims=True)
    zc = z - mu
    var = jnp.mean(zc * zc, axis=-1, keepdims=True)
    return zc * lax.rsqrt(var + LN_EPS) * g + b


def _outproj_kernel(x_ref, y1_ref, y2_ref, w1_ref, w2_ref, mod_ref, g_ref, b_ref, o_ref):
    f = (jnp.dot(y1_ref[...], w1_ref[...], preferred_element_type=F32)
         + jnp.dot(y2_ref[...], w2_ref[...], preferred_element_type=F32))
    o_ref[...] = _residual_layer_norm(x_ref[...], f, mod_ref[2:3, :], g_ref[...], b_ref[...])


def _out_project(x, y1, y2, c2, w, mod, g, b, tm):
    n = x.shape[0]
    half = D_MODEL // 2
    return pl.pallas_call(
        _outproj_kernel,
        grid=(n // tm,),
        in_specs=[pl.BlockSpec((tm, D_MODEL), lambda i: (i, 0)),
                  pl.BlockSpec((tm, half), lambda i: (i, 0)),
                  pl.BlockSpec((tm, half), lambda i: (i, c2)),
                  pl.BlockSpec((half, D_MODEL), lambda i: (0, 0)),
                  pl.BlockSpec((half, D_MODEL), lambda i: (1, 0)),
                  _full((6, D_MODEL)), _full((1, D_MODEL)), _full((1, D_MODEL))],
        out_specs=pl.BlockSpec((tm, D_MODEL), lambda i: (i, 0)),
        out_shape=jax.ShapeDtypeStruct((n, D_MODEL), F32),
        compiler_params=_params(("parallel",)),
        name="out_project_ln",
    )(x, y1, y2, w, w, mod, g, b)


FFN_CHUNKS = 2
FFN_FC = D_FF // FFN_CHUNKS


def _ffn_kernel(x_ref, xp_ref, xn_ref, mod_ref, wa_ref, wg_ref, cwa_ref, cwg_ref, cba_ref, cbg_ref, wd_ref,
                g_ref, b_ref, o_ref, h_ref, ua_ref, ug_ref, acc_ref):
    i = pl.program_id(0)
    c = pl.program_id(1)
    tm = x_ref.shape[0]

    @pl.when(c == 0)
    def _():
        shift = mod_ref[3:4, :]
        scale = 1.0 + mod_ref[4:5, :]
        hp = jnp.where(i > 0, xp_ref[...] * scale + shift, 0.0)
        hn = jnp.where(i < pl.num_programs(0) - 1, xn_ref[...] * scale + shift, 0.0)
        hm = x_ref[...] * scale + shift
        h_ref[...] = jnp.concatenate([hp, hm, hn], axis=0).astype(BF16)
        acc_ref[...] = jnp.zeros(acc_ref.shape, F32)

    h = h_ref[...]
    ua_ref[...] = jnp.dot(h, wa_ref[...], preferred_element_type=F32)
    ug_ref[...] = jnp.dot(h, wg_ref[...], preferred_element_type=F32)

    def conv(u_ref, cw_ref, cb_ref):
        return (cb_ref[...]
                + u_ref[HALO - 1:HALO - 1 + tm, :] * cw_ref[0:1, :]
                + u_ref[HALO:HALO + tm, :] * cw_ref[1:2, :]
                + u_ref[HALO + 1:HALO + 1 + tm, :] * cw_ref[2:3, :])

    a = conv(ua_ref, cwa_ref, cba_ref)
    gt = conv(ug_ref, cwg_ref, cbg_ref)
    act = (gt * jax.nn.sigmoid(gt) * a).astype(BF16)
    acc_ref[...] += jnp.dot(act, wd_ref[...], preferred_element_type=F32)

    @pl.when(c == pl.num_programs(1) - 1)
    def _():
        o_ref[...] = _residual_layer_norm(x_ref[...], acc_ref[...], mod_ref[5:6, :], g_ref[...], b_ref[...])


def _conv_ffn(x, mod, w_up, conv_w, conv_b, w_down, g, b, tm):
    n = x.shape[0]
    nrow = n // tm
    nh = n // HALO
    per = tm // HALO
    nc = FFN_CHUNKS
    return pl.pallas_call(
        _ffn_kernel,
        grid=(nrow, nc),
        in_specs=[pl.BlockSpec((tm, D_MODEL), lambda i, c: (i, 0)),
                  pl.BlockSpec((HALO, D_MODEL), lambda i, c: (jnp.maximum(i * per - 1, 0), 0)),
                  pl.BlockSpec((HALO, D_MODEL), lambda i, c: (jnp.minimum((i + 1) * per, nh - 1), 0)),
                  _full((6, D_MODEL)),
                  pl.BlockSpec((D_MODEL, FFN_FC), lambda i, c: (0, c)),
                  pl.BlockSpec((D_MODEL, FFN_FC), lambda i, c: (0, nc + c)),
                  pl.BlockSpec((3, FFN_FC), lambda i, c: (0, c)),
                  pl.BlockSpec((3, FFN_FC), lambda i, c: (0, nc + c)),
                  pl.BlockSpec((1, FFN_FC), lambda i, c: (0, c)),
                  pl.BlockSpec((1, FFN_FC), lambda i, c: (0, nc + c)),
                  pl.BlockSpec((FFN_FC, D_MODEL), lambda i, c: (c, 0)),
                  _full((1, D_MODEL)), _full((1, D_MODEL))],
        out_specs=pl.BlockSpec((tm, D_MODEL), lambda i, c: (i, 0)),
        out_shape=jax.ShapeDtypeStruct((n, D_MODEL), F32),
        scratch_shapes=[pltpu.VMEM((tm + 2 * HALO, D_MODEL), BF16),
                        pltpu.VMEM((tm + 2 * HALO, FFN_FC), F32),
                        pltpu.VMEM((tm + 2 * HALO, FFN_FC), F32),
                        pltpu.VMEM((tm, D_MODEL), F32)],
        compiler_params=_params(("parallel", "arbitrary")),
        name="conv_ffn_ln",
    )(x, x, x, mod, w_up, w_up, conv_w, conv_w, conv_b, conv_b, w_down, g, b)


def _pair_perm(n_heads, kv_heads):
    per = n_heads // kv_heads
    order = []
    for j in range(per):
        for g in range(kv_heads):
            order.append(g * per + j)
    cols = np.concatenate([np.arange(h * HEAD_DIM, (h + 1) * HEAD_DIM) for h in order])
    return cols


_PERM_A = _pair_perm(A_HEADS, A_KV_HEADS)
_PERM_C = _pair_perm(C_HEADS, C_KV_HEADS)


def _rope_tables(n):
    t = np.arange(n)
    row = (t // GRID_W).astype(np.float32)
    col = (t % GRID_W).astype(np.float32)
    half = HEAD_DIM // 2
    inv = jnp.asarray(ROPE_THETA, F32) ** (-jnp.arange(0, half, 2, dtype=F32) / half)
    ang_r = jnp.asarray(row)[:, None] * inv
    ang_c = jnp.asarray(col)[:, None] * inv
    ang = jnp.concatenate([ang_r, ang_r, ang_c, ang_c], -1)
    ang = jnp.concatenate([ang, ang], -1)
    sign = np.where(np.arange(LANES) % 32 < 16, -1.0, 1.0).astype(np.float32)
    return jnp.cos(ang), jnp.sin(ang) * sign


_EVEN_GROUPS = (
    (0, A_Q, 0, True, QK_SCALE),
    (A_Q, A_KV, A_Q, True, 1.0),
    (A_Q + A_KV, A_KV, None, False, 1.0),
    (A_Q + 2 * A_KV, B_W, None, False, QK_SCALE),
    (A_Q + 2 * A_KV + B_W, B_W, None, False, 1.0),
    (A_Q + 2 * A_KV + 2 * B_W, B_W, None, False, 1.0),
)
_EVEN_GROUPS_CTX = tuple((c0, w, g0, False, s) for (c0, w, g0, _, s) in _EVEN_GROUPS)
_ODD_GROUPS = (
    (0, C_Q, None, True, QK_SCALE),
    (C_Q, C_KV, None, True, 1.0),
    (C_Q + C_KV, C_KV, None, False, 1.0),
)
_ODD_GROUPS_CTX = tuple((c0, w, g0, False, s) for (c0, w, g0, _, s) in _ODD_GROUPS)

LAT_TM = 512


def kernel(x, c, ctx, c_ctx, ada_w, ada_b, ln_g, ln_b, ev_w_in, ev_w_out, ev_q_gain, ev_k_gain, ev_rpb,
           od_w_in, od_w_out, od_sink, ffn_w_up, ffn_conv_w, ffn_conv_b, ffn_w_down):
    n = x.shape[1]
    nc = ctx.shape[1]
    rows = n // GRID_W
    x_lat = x[0]
    x_ctx = ctx[0]

    cvec =jnp.zeros((8, D_MODEL), F32).at[0].set(c[0]).at[1].set(c_ctx)
    mods = _modulation(cvec, ada_w, ada_b)

    cos, sin = _rope_tables(n)
    cos_c, sin_c = cos[:nc], sin[:nc]
    lane = np.arange(LANES)
    gmat = jnp.asarray((lane[:, None] // HEAD_DIM) == (lane[None, :] // HEAD_DIM), BF16)
    no_sink = jnp.full((C_HEADS,), NEG, F32)

    for l in range(DEPTH):
        i = l // 2
        ctx_out = l < DEPTH - 1
        m_lat = mods[l, 0].reshape(6, D_MODEL)
        m_ctx = mods[l, 1].reshape(6, D_MODEL)
        g0, b0 = ln_g[l, 0][None], ln_b[l, 0][None]
        g1, b1 = ln_g[l, 1][None], ln_b[l, 1][None]
        if l % 2 == 0:
            w_in = ev_w_in[i]
            w_in = jnp.concatenate([w_in[:, :A_Q][:, _PERM_A], w_in[:, A_Q:]], axis=1).astype(BF16)
            w_out = ev_w_out[i]
            w_out = jnp.concatenate([w_out[:A_Q][_PERM_A], w_out[A_Q:]], axis=0).astype(BF16)
            gains = jnp.concatenate([jnp.tile(ev_q_gain[i], A_HEADS), jnp.tile(ev_k_gain[i], A_KV_HEADS)])[None]
            qa, ka, va, qb, kb, vb = _project(x_lat, m_lat, w_in, gains, cos, sin, gmat, _EVEN_GROUPS, LAT_TM)
            qa_c, ka_c, va_c, qb_c, kb_c, vb_c = _project(x_ctx, m_ctx, w_in, gains, cos_c, sin_c, gmat,
                                                           _EVEN_GROUPS_CTX, nc)
            ya = _global_attention(qa, ka, va, ka_c, va_c, 512, 1024)
            bias = _nbr_bias_tables(ev_rpb[i] * LOG2E, rows)
            yb = _nbr_attention(qb, kb, vb, kb_c, vb_c, bias)
            x_lat = _out_project(x_lat, ya, yb, 0, w_out, m_lat, g0, b0, LAT_TM)
            if ctx_out:
                ya_c = _ctx_attention(qa_c, ka_c, va_c, no_sink, 0)
                yb_c = _ctx_attention(qb_c, kb_c, vb_c, no_sink, 1)
                x_ctx = _out_project(x_ctx, ya_c, yb_c, 0, w_out, m_ctx, g0, b0, nc)
        else:
            w_in = od_w_in[i]
            w_in = jnp.concatenate([w_in[:, :C_Q][:, _PERM_C], w_in[:, C_Q:]], axis=1).astype(BF16)
            w_out = od_w_out[i][_PERM_C].astype(BF16)
            sink = od_sink[i] * LOG2E
            gains = jnp.ones((1, A_Q + A_KV), F32)
            q, k, v = _project(x_lat, m_lat, w_in, gains, cos, sin, gmat, _ODD_GROUPS, LAT_TM)
            q_c, k_c, v_c = _project(x_ctx, m_ctx, w_in, gains, cos_c, sin_c, gmat, _ODD_GROUPS_CTX, nc)
            y = _window_attention(q, k, v, k_c, v_c, sink)
            x_lat = _out_project(x_lat, y, y, 1, w_out, m_lat, g0, b0, LAT_TM)
            if ctx_out:
                y_c = _ctx_attention(q_c, k_c, v_c, sink, 0)
                x_ctx = _out_project(x_ctx, y_c, y_c, 1, w_out, m_ctx, g0, b0, nc)
        w_up = ffn_w_up[l].astype(BF16)
        w_down = ffn_w_down[l].astype(BF16)
        cb = ffn_conv_b[l][None]
        x_lat = _conv_ffn(x_lat, m_lat, w_up, ffn_conv_w[l], cb, w_down, g1, b1, LAT_TM)
        if ctx_out:
            x_ctx = _conv_ffn(x_ctx, m_ctx, w_up, ffn_conv_w[l], cb, w_down, g1, b1, nc)
    return x_lat[None]
```

```python
import functools

import numpy as np
import jax
import jax.numpy as jnp
from jax import lax
from jax.experimental import pallas as pl
from jax.experimental.pallas import tpu as pltpu

D_MODEL = 1024
DEPTH = 4
GRID_W = 64
HEAD_DIM = 64
A_HEADS = 8
A_KV_HEADS = 2
B_HEADS = 8
NB_KH = 8
NB_KW = 16
C_HEADS = 16
C_KV_HEADS = 2
C_WINDOW = 128
D_FF = 2816
ROPE_THETA = 10000.0
LN_EPS = 1e-5
RMS_EPS = 1e-6
NEG = -1e30
BIG = 1e30
DN_ALPHA = (2 * DEPTH) ** 0.25
A_Q = A_HEADS * HEAD_DIM
A_KV = A_KV_HEADS * HEAD_DIM
B_W = B_HEADS * HEAD_DIM
C_Q = C_HEADS * HEAD_DIM
C_KV = C_KV_HEADS * HEAD_DIM
LOG2E = 1.4426950408889634
QK_SCALE = HEAD_DIM ** -0.5 * LOG2E
SAFE_LOG2 = 64.0

LANES = 128
SUBLANES = 8
HALO = SUBLANES
VMEM_LIMIT = 52 * 1024 * 1024

BF16 = jnp.bfloat16
F32 = jnp.float32

_NT = (((1,), (1,)), ((), ()))


def _params(sem):
    return pltpu.CompilerParams(dimension_semantics=sem, vmem_limit_bytes=VMEM_LIMIT)


def _full(shape):
    return pl.BlockSpec(shape, lambda *_: (0,) * len(shape))


def _lane_lo(shape):
    return lax.broadcasted_iota(jnp.int32, shape, len(shape) - 1) < HEAD_DIM


def _unit_lane(half):
    return HEAD_DIM if half == 0 else 0


def _unit_lane_variants(x):
    lane = jnp.arange(x.shape[-1]) % LANES
    one = jnp.ones((), x.dtype)
    zero = jnp.zeros((), x.dtype)
    lo = jnp.where(lane < HEAD_DIM, x, jnp.where(lane == HEAD_DIM, one, zero))
    hi = jnp.where(lane >= HEAD_DIM, x, jnp.where(lane == 0, one, zero))
    return jnp.stack([lo, hi])


def _running_max(top, s):
    cm = s[:, 0:LANES]
    for j in range(1, s.shape[1] // LANES):
        cm = jnp.maximum(cm, s[:, j * LANES:(j + 1) * LANES])
    return jnp.maximum(top, jnp.max(cm.reshape(s.shape[0] // SUBLANES, SUBLANES, LANES), axis=0))


def _running_min_of_lane(low, x, unit):
    lane = lax.broadcasted_iota(jnp.int32, x.shape, 1)
    picked = jnp.where(lane == unit, x, BIG)
    return jnp.minimum(low, jnp.min(picked.reshape(x.shape[0] // SUBLANES, SUBLANES, LANES), axis=0))


def _outside_safe_range(flags):
    f = flags.reshape(-1, 2, SUBLANES, LANES)
    return jnp.logical_not((jnp.max(f[:, 0]) < SAFE_LOG2) & (jnp.min(f[:, 1]) > 2.0 ** -SAFE_LOG2))


def _mod_kernel(c_ref, w_ref, b_ref, o_ref):
    s = c_ref[...]
    s = s * jax.nn.sigmoid(s)
    o_ref[0] = jnp.dot(s, w_ref[0], preferred_element_type=F32, precision=lax.Precision.HIGHEST) + b_ref[0]


def _modulation(cvec, ada_w, ada_b):
    nb = 6
    return pl.pallas_call(
        _mod_kernel,
        grid=(DEPTH, nb),
        in_specs=[pl.BlockSpec((8, D_MODEL), lambda l, j: (0, 0)),
                  pl.BlockSpec((1, D_MODEL, D_MODEL), lambda l, j: (l, 0, j)),
                  pl.BlockSpec((1, 1, D_MODEL), lambda l, j: (l, 0, j))],
        out_specs=pl.BlockSpec((1, 8, D_MODEL), lambda l, j: (l, 0, j)),
        out_shape=jax.ShapeDtypeStruct((DEPTH, 8, 6 * D_MODEL), F32),
        compiler_params=_params(("arbitrary", "arbitrary")),
        name="modulation",
    )(cvec, ada_w, ada_b.reshape(DEPTH, 1, 6 * D_MODEL))


def _group_sumsq(z, gmat):
    x2 = z * z
    hi = x2.astype(BF16)
    lo = (x2 - hi.astype(F32)).astype(BF16)
    return (jnp.dot(hi, gmat, preferred_element_type=F32) + jnp.dot(lo, gmat, preferred_element_type=F32))


def _proj_kernel(groups, x_ref, mod_ref, w_ref, gain_ref, cos_ref, sin_ref, gmat_ref, *out_refs):
    shift = mod_ref[0:1, :]
    scale = mod_ref[1:2, :]
    h = (x_ref[...] * (1.0 + scale) + shift).astype(BF16)
    cos = cos_ref[...]
    sin = sin_ref[...]
    gmat = gmat_ref[...]
    first = lax.broadcasted_iota(jnp.int32, cos.shape, 1) % 32 < 16
    for (c0, width, g0, rope, qscale), o_ref in zip(groups, out_refs):
        z = jnp.dot(h, w_ref[:, c0:c0 + width], preferred_element_type=F32)
        for b in range(width // LANES):
            zb = z[:, b * LANES:(b + 1) * LANES]
            if g0 is not None:
                ms = _group_sumsq(zb, gmat) * (1.0 / HEAD_DIM)
                zb = zb * lax.rsqrt(ms + RMS_EPS) * gain_ref[:, g0 + b * LANES:g0 + (b + 1) * LANES]
            if rope:
                rot = jnp.where(first, pltpu.roll(zb, LANES - 16, 1), pltpu.roll(zb, 16, 1))
                zb = zb * cos + rot * sin
            if qscale != 1.0:
                zb = zb * qscale
            o_ref[:, b * LANES:(b + 1) * LANES] = zb.astype(BF16)


def _project(x, mod, w, gains, cos, sin, gmat, groups, tm):
    n = x.shape[0]
    win = w.shape[1]
    out_shape = [jax.ShapeDtypeStruct((n, g[1]), BF16) for g in groups]
    out_specs = [pl.BlockSpec((tm, g[1]), lambda i: (i, 0)) for g in groups]
    return pl.pallas_call(
        functools.partial(_proj_kernel, groups),
        grid=(n // tm,),
        in_specs=[pl.BlockSpec((tm, D_MODEL), lambda i: (i, 0)),
                  _full((6, D_MODEL)),
                  _full((D_MODEL, win)),
                  _full(gains.shape),
                  pl.BlockSpec((tm, LANES), lambda i: (i, 0)),
                  pl.BlockSpec((tm, LANES), lambda i: (i, 0)),
                  _full((LANES, LANES))],
        out_specs=out_specs,
        out_shape=out_shape,
        compiler_params=_params(("parallel",)),
        name="qkv_project",
    )(x, mod, w, gains, cos, sin, gmat)


GA_TQ = 512
GA_TK = 2048


def _global_attn_kernel(exact, q_ref, kc_ref, vc_ref, k_ref, v_ref, o_ref, flag_ref, qx_ref, c_ref, acc_ref):
    kk = pl.program_id(1)
    tq = q_ref.shape[0]
    nblk = q_ref.shape[1] // LANES
    nh = 2 * nblk
    lane = lax.broadcasted_iota(jnp.int32, (tq, LANES), 1)

    def moving_reference_update(kx_ref, vx_ref, first):
        for h in range(nh):
            half = h % 2
            unit = _unit_lane(half)
            sp = lax.dot_general(qx_ref[h], kx_ref[half], _NT, preferred_element_type=F32)
            mb = jnp.max(sp, axis=-1, keepdims=True)
            c_old = c_ref[h]
            target = mb if first else jnp.maximum(c_old, c_old + mb)
            qx_ref[h] = jnp.where(lane == unit, -target, qx_ref[h].astype(F32)).astype(BF16)
            c_new = -qx_ref[h][:, unit:unit + 1].astype(F32)
            d = c_new - c_old
            pv = jnp.dot(jnp.exp2(sp - d).astype(BF16), vx_ref[half], preferred_element_type=F32)
            acc_ref[h] = pv if first else acc_ref[h] * jnp.exp2(-d) + pv
            c_ref[h] = c_new

    def unshifted_update(kx_ref, vx_ref):
        top = flag_ref[0:SUBLANES, :]
        for h in range(nh):
            half = h % 2
            s = lax.dot_general(qx_ref[h], kx_ref[half], _NT, preferred_element_type=F32)
            top = _running_max(top, s)
            acc_ref[h] += jnp.dot(jnp.exp2(s).astype(BF16), vx_ref[half], preferred_element_type=F32)
        flag_ref[0:SUBLANES, :] = top

    @pl.when(kk == 0)
    def _():
        lo = _lane_lo((tq, LANES))
        for j in range(nblk):
            qj = q_ref[:, j * LANES:(j + 1) * LANES]
            zero = jnp.zeros_like(qj)
            qx_ref[2 * j] = jnp.where(lo, qj, zero)
            qx_ref[2 * j + 1] = jnp.where(lo, zero, qj)
        c_ref[...] = jnp.zeros(c_ref.shape, F32)
        flag_ref[0:SUBLANES, :] = jnp.full((SUBLANES, LANES), NEG, F32)
        flag_ref[SUBLANES:2 * SUBLANES, :] = jnp.full((SUBLANES, LANES), BIG, F32)
        if exact:
            moving_reference_update(kc_ref, vc_ref, True)
        else:
            acc_ref[...] = jnp.zeros(acc_ref.shape, F32)
            unshifted_update(kc_ref, vc_ref)

    if exact:
        moving_reference_update(k_ref, v_ref, False)
    else:
        unshifted_update(k_ref, v_ref)

    @pl.when(kk == pl.num_programs(1) - 1)
    def _():
        lo = _lane_lo((tq, LANES))
        low = flag_ref[SUBLANES:2 * SUBLANES, :]
        for j in range(nblk):
            a_lo = acc_ref[2 * j]
            a_hi = acc_ref[2 * j + 1]
            low = _running_min_of_lane(_running_min_of_lane(low, a_lo, _unit_lane(0)), a_hi, _unit_lane(1))
            o_lo = a_lo / a_lo[:, HEAD_DIM:HEAD_DIM + 1]
            o_hi = a_hi / a_hi[:, 0:1]
            o_ref[:, j * LANES:(j + 1) * LANES] = jnp.where(lo, o_lo, o_hi).astype(o_ref.dtype)
        flag_ref[SUBLANES:2 * SUBLANES, :] = low


def _global_attention(q, k, v, kc, vc):
    n, qw = q.shape
    nc = kc.shape[0]
    nh = 2 * (qw // LANES)
    nq = n // GA_TQ
    operands = (q, _unit_lane_variants(kc), _unit_lane_variants(vc), _unit_lane_variants(k), _unit_lane_variants(v))

    def run(exact):
        return pl.pallas_call(
            functools.partial(_global_attn_kernel, exact),
            grid=(nq, n // GA_TK),
            in_specs=[pl.BlockSpec((GA_TQ, qw), lambda i, j: (i, 0)),
                      _full((2, nc, LANES)), _full((2, nc, LANES)),
                      pl.BlockSpec((2, GA_TK, LANES), lambda i, j: (0, j, 0)),
                      pl.BlockSpec((2, GA_TK, LANES), lambda i, j: (0, j, 0))],
            out_specs=[pl.BlockSpec((GA_TQ, qw), lambda i, j: (i, 0)),
                       pl.BlockSpec((2 * SUBLANES, LANES), lambda i, j: (i, 0))],
            out_shape=[jax.ShapeDtypeStruct((n, qw), BF16),
                       jax.ShapeDtypeStruct((nq * 2 * SUBLANES, LANES), F32)],
            scratch_shapes=[pltpu.VMEM((nh, GA_TQ, LANES), BF16),
                            pltpu.VMEM((nh, GA_TQ, 1), F32),
                            pltpu.VMEM((nh, GA_TQ, LANES), F32)],
            compiler_params=_params(("parallel", "arbitrary")),
            name="global_attention_exact" if exact else "global_attention",
        )(*operands)

    y, flags = run(False)
    return lax.cond(_outside_safe_range(flags), lambda: run(True)[0], lambda: y)


NB_QROWS = 8
NB_KROWS = NB_QROWS + NB_KH
NB_MASKED = 2 * NB_KH - 1
NB_TILES = 3 * (NB_MASKED + 1)


def _nbr_bias_tiles(rpb):
    nh = rpb.shape[0]
    col = np.arange(GRID_W)
    cs = np.clip(col - NB_KW // 2, 0, GRID_W - NB_KW)
    col_ok = (col[None, :] >= cs[:, None]) & (col[None, :] < cs[:, None] + NB_KW)
    dcol = col[None, :] - col[:, None] + NB_KW - 1
    sel_col = (dcol[:, :, None] == np.arange(2 * NB_KW - 1)) & col_ok[:, :, None]
    base = jnp.einsum('qkj,hdj->hdqk', jnp.asarray(sel_col, F32), rpb, precision=lax.Precision.HIGHEST)
    base = jnp.where(jnp.asarray(col_ok), base, NEG)
    masked = jnp.full((nh, 1, GRID_W, GRID_W), NEG, F32)
    base = jnp.concatenate([base, masked], axis=1)
    nxt = jnp.concatenate([base[:, 1:], masked], axis=1)
    allmasked = jnp.full_like(base, NEG)
    return jnp.concatenate([jnp.concatenate([base, nxt], -1),
                            jnp.concatenate([base, allmasked], -1),
                            jnp.concatenate([allmasked, base], -1)], axis=1)


def _nbr_tile_index(rows):
    nq = rows // NB_QROWS
    idx = np.zeros((3, NB_QROWS, NB_KROWS // 2), np.int64)
    for v, blk in enumerate((0, 1, nq - 1)):
        r0 = blk * NB_QROWS
        k0 = int(np.clip(r0 - NB_KH // 2, 0, rows - NB_KROWS))
        for qr in range(NB_QROWS):
            qa = r0 + qr
            rs = int(np.clip(qa - NB_KH // 2, 0, rows - NB_KH))

            def code(kr):
                ka = k0 + kr
                return ka - qa + NB_KH - 1 if rs <= ka < rs + NB_KH else NB_MASKED

            for kp in range(NB_KROWS // 2):
                left, right = code(2 * kp), code(2 * kp + 1)
                if left < NB_MASKED and right < NB_MASKED:
                    assert right == left + 1
                    idx[v, qr, kp] = left
                elif left < NB_MASKED:
                    idx[v, qr, kp] = (NB_MASKED + 1) + left
                else:
                    idx[v, qr, kp] = 2 * (NB_MASKED + 1) + right
    return tuple(tuple(tuple(int(t) for t in row) for row in var) for var in idx)


def _nbr_attn_kernel(exact, tile_index, q_ref, k_ref, v_ref, kc_ref, vc_ref, tiles_ref, o_ref, flag_ref, bias_ref):
    i = pl.program_id(1)
    nq = pl.num_programs(1)
    rows = k_ref.shape[0] // GRID_W
    span = NB_KROWS * GRID_W

    def fill_bias(variant):
        for half in range(2):
            for qr in range(NB_QROWS):
                for kp in range(NB_KROWS // 2):
                    bias_ref[half, qr * GRID_W:(qr + 1) * GRID_W, kp * LANES:(kp + 1) * LANES] = (
                        tiles_ref[half, tile_index[variant][qr][kp]])

    for variant, at in enumerate((0, 1, nq - 1)):
        pl.when(i == at)(functools.partial(fill_bias, variant))

    r0 = i * NB_QROWS
    ks = pl.multiple_of(jnp.clip(r0 - NB_KH // 2, 0, rows - NB_KROWS) * GRID_W, GRID_W)
    kw = k_ref[pl.ds(ks, span), :]
    kc = kc_ref[...]
    q = q_ref[...]
    lo = _lane_lo(q.shape)
    zero = jnp.zeros_like(q)
    top = jnp.full((SUBLANES, LANES), NEG, F32)
    low = jnp.full((SUBLANES, LANES), BIG, F32)
    outs = []
    for half in range(2):
        qm = jnp.where(lo, q, zero) if half == 0 else jnp.where(lo, zero, q)
        s_nb = lax.dot_general(qm, kw, _NT, preferred_element_type=F32) + bias_ref[half]
        s_cx = lax.dot_general(qm, kc, _NT, preferred_element_type=F32)
        if exact:
            m = jnp.maximum(jnp.max(s_nb, axis=-1, keepdims=True), jnp.max(s_cx, axis=-1, keepdims=True))
            s_nb = s_nb - m
            s_cx = s_cx - m
        else:
            top = _running_max(_running_max(top, s_nb), s_cx)
        pv = (jnp.dot(jnp.exp2(s_nb).astype(BF16), v_ref[half, pl.ds(ks, span), :], preferred_element_type=F32)
              + jnp.dot(jnp.exp2(s_cx).astype(BF16), vc_ref[half], preferred_element_type=F32))
        unit = _unit_lane(half)
        low = _running_min_of_lane(low, pv, unit)
        outs.append(pv / pv[:, unit:unit + 1])
    o_ref[...] = jnp.where(lo, outs[0], outs[1]).astype(o_ref.dtype)
    flag_ref[0:SUBLANES, :] = top
    flag_ref[SUBLANES:2 * SUBLANES, :] = low


def _nbr_attention(q, k, v, kc, vc, tiles):
    n, qw = q.shape
    nc = kc.shape[0]
    tq = NB_QROWS * GRID_W
    nq = n // tq
    assert nq >= 3
    npair = qw // LANES
    tile_index = _nbr_tile_index(n // GRID_W)
    operands = (q, k, _unit_lane_variants(v), kc, _unit_lane_variants(vc), tiles)

    def run(exact):
        return pl.pallas_call(
            functools.partial(_nbr_attn_kernel, exact, tile_index),
            grid=(npair, nq),
            in_specs=[pl.BlockSpec((tq, LANES), lambda p, i: (i, p)),
                      pl.BlockSpec((n, LANES), lambda p, i: (0, p)),
                      pl.BlockSpec((2, n, LANES), lambda p, i: (0, 0, p)),
                      pl.BlockSpec((nc, LANES), lambda p, i: (0, p)),
                      pl.BlockSpec((2, nc, LANES), lambda p, i: (0, 0, p)),
                      pl.BlockSpec((2, NB_TILES, GRID_W, LANES), lambda p, i: (p, 0, 0, 0))],
            out_specs=[pl.BlockSpec((tq, LANES), lambda p, i: (i, p)),
                       pl.BlockSpec((2 * SUBLANES, LANES), lambda p, i: (p * nq + i, 0))],
            out_shape=[jax.ShapeDtypeStruct((n, qw), BF16),
                       jax.ShapeDtypeStruct((npair * nq * 2 * SUBLANES, LANES), F32)],
            scratch_shapes=[pltpu.VMEM((2, tq, NB_KROWS * GRID_W), F32)],
            compiler_params=_params(("parallel", "arbitrary")),
            name="neighbourhood_attention_exact" if exact else "neighbourhood_attention",
        )(*operands)

    y, flags = run(False)
    return lax.cond(_outside_safe_range(flags), lambda: run(True)[0], lambda: y)


WIN_TQ = 256
WIN_SPAN = WIN_TQ + 2 * C_WINDOW


def _window_attn_kernel(exact, sink_ref, q_ref, k_ref, v_ref, kc_ref, vc_ref, o_ref, flag_ref):
    i = pl.program_id(0)
    n = k_ref.shape[0]
    tq = q_ref.shape[0]
    nblk = q_ref.shape[1] // LANES
    q0 = i * tq
    ks = pl.multiple_of(jnp.clip(q0 - C_WINDOW, 0, n - WIN_SPAN), C_WINDOW)
    kw = k_ref[pl.ds(ks, WIN_SPAN), :]
    kc = kc_ref[...]
    qpos = q0 + lax.broadcasted_iota(jnp.int32, (tq, WIN_SPAN), 0)
    kpos = ks + lax.broadcasted_iota(jnp.int32, (tq, WIN_SPAN), 1)
    valid = jnp.abs(kpos - qpos) <= C_WINDOW
    lo = _lane_lo((tq, LANES))
    top = jnp.full((SUBLANES, LANES), NEG, F32)
    low = jnp.full((SUBLANES, LANES), BIG, F32)
    for j in range(nblk):
        q = q_ref[:, j * LANES:(j + 1) * LANES]
        zero = jnp.zeros_like(q)
        outs = []
        for half in range(2):
            qm = jnp.where(lo, q, zero) if half == 0 else jnp.where(lo, zero, q)
            sink = jnp.full((tq, 1), sink_ref[j + nblk * half], F32)
            s_w = jnp.where(valid, lax.dot_general(qm, kw, _NT, preferred_element_type=F32), NEG)
            s_c = lax.dot_general(qm, kc, _NT, preferred_element_type=F32)
            if exact:
                m = jnp.maximum(jnp.maximum(jnp.max(s_w, axis=-1, keepdims=True),
                                            jnp.max(s_c, axis=-1, keepdims=True)), sink)
                s_w = s_w - m
                s_c = s_c - m
                sink = sink - m
            else:
                top = jnp.maximum(_running_max(_running_max(top, s_w), s_c), sink_ref[j + nblk * half])
            pv = (jnp.dot(jnp.exp2(s_w).astype(BF16), v_ref[half, pl.ds(ks, WIN_SPAN), :],
                          preferred_element_type=F32)
                  + jnp.dot(jnp.exp2(s_c).astype(BF16), vc_ref[half], preferred_element_type=F32))
            unit = _unit_lane(half)
            den = pv[:, unit:unit + 1] + jnp.exp2(sink)
            low = _running_min_of_lane(low, jnp.broadcast_to(den, pv.shape), unit)
            outs.append(pv / den)
        o_ref[:, j * LANES:(j + 1) * LANES] = jnp.where(lo, outs[0], outs[1]).astype(o_ref.dtype)
    flag_ref[0:SUBLANES, :] = top
    flag_ref[SUBLANES:2 * SUBLANES, :] = low


def _window_attention(q, k, v, kc, vc, sink):
    n, qw = q.shape
    nc = kc.shape[0]
    nq = n // WIN_TQ
    operands = (sink, q, k, _unit_lane_variants(v), kc, _unit_lane_variants(vc))

    def run(exact):
        return pl.pallas_call(
            functools.partial(_window_attn_kernel, exact),
            grid=(nq,),
            in_specs=[pl.BlockSpec(memory_space=pltpu.SMEM),
                      pl.BlockSpec((WIN_TQ, qw), lambda i: (i, 0)),
                      _full((n, LANES)), _full((2, n, LANES)),
                      _full((nc, LANES)), _full((2, nc, LANES))],
            out_specs=[pl.BlockSpec((WIN_TQ, qw), lambda i: (i, 0)),
                       pl.BlockSpec((2 * SUBLANES, LANES), lambda i: (i, 0))],
            out_shape=[jax.ShapeDtypeStruct((n, qw), BF16),
                       jax.ShapeDtypeStruct((nq * 2 * SUBLANES, LANES), F32)],
            compiler_params=_params(("parallel",)),
            name="window_attention_exact" if exact else "window_attention",
        )(*operands)

    y, flags = run(False)
    return lax.cond(_outside_safe_range(flags), lambda: run(True)[0], lambda: y)


def _ctx_attn_kernel(sink_ref, q_ref, k_ref, v_ref, o_ref):
    j = pl.program_id(0)
    nblk = pl.num_programs(0)
    q = q_ref[...]
    k = k_ref[...]
    v = v_ref[...]
    lo = _lane_lo(q.shape)
    zero = jnp.zeros_like(q)
    outs = []
    for half in range(2):
        qm = jnp.where(lo, q, zero) if half == 0 else jnp.where(lo, zero, q)
        sink = sink_ref[j + nblk * half]
        s = lax.dot_general(qm, k, _NT, preferred_element_type=F32)
        m = jnp.maximum(jnp.max(s, axis=-1, keepdims=True), sink)
        p = jnp.exp2(s - m)
        den = jnp.sum(p, axis=-1, keepdims=True) + jnp.exp2(sink - m)
        outs.append(jnp.dot(p.astype(BF16), v, preferred_element_type=F32) / den)
    o_ref[...] = jnp.where(lo, outs[0], outs[1]).astype(o_ref.dtype)


def _ctx_attention(q, k, v, sink, kv_per_block):
    c, qw = q.shape
    nblk = qw // LANES
    return pl.pallas_call(
        _ctx_attn_kernel,
        grid=(nblk,),
        in_specs=[pl.BlockSpec(memory_space=pltpu.SMEM),
                  pl.BlockSpec((c, LANES), lambda j: (0, j)),
                  pl.BlockSpec((c, LANES), lambda j: (0, j * kv_per_block)),
                  pl.BlockSpec((c, LANES), lambda j: (0, j * kv_per_block))],
        out_specs=pl.BlockSpec((c, LANES), lambda j: (0, j)),
        out_shape=jax.ShapeDtypeStruct((c, qw), BF16),
        compiler_params=_params(("parallel",)),
        name="context_attention",
    )(sink, q, k, v)


def _residual_layer_norm(x, f, gate, g, b):
    z = DN_ALPHA * x + gate * f
    mu = jnp.mean(z, axis=-1, keepdims=True)
    zc = z - mu
    var = jnp.mean(zc * zc, axis=-1, keepdims=True)
    return zc * lax.rsqrt(var + LN_EPS) * g + b


def _outproj_kernel(x_ref, y1_ref, y2_ref, w1_ref, w2_ref, mod_ref, g_ref, b_ref, o_ref):
    f = (jnp.dot(y1_ref[...], w1_ref[...], preferred_element_type=F32)
         + jnp.dot(y2_ref[...], w2_ref[...], preferred_element_type=F32))
    o_ref[...] = _residual_layer_norm(x_ref[...], f, mod_ref[2:3, :], g_ref[...], b_ref[...])


def _out_project(x, y1, y2, c2, w, mod, g, b, tm):
    n = x.shape[0]
    half = D_MODEL // 2
    return pl.pallas_call(
        _outproj_kernel,
        grid=(n // tm,),
        in_specs=[pl.BlockSpec((tm, D_MODEL), lambda i: (i, 0)),
                  pl.BlockSpec((tm, half), lambda i: (i, 0)),
                  pl.BlockSpec((tm, half), lambda i: (i, c2)),
                  pl.BlockSpec((half, D_MODEL), lambda i: (0, 0)),
                  pl.BlockSpec((half, D_MODEL), lambda i: (1, 0)),
                  _full((6, D_MODEL)), _full((1, D_MODEL)), _full((1, D_MODEL))],
        out_specs=pl.BlockSpec((tm, D_MODEL), lambda i: (i, 0)),
        out_shape=jax.ShapeDtypeStruct((n, D_MODEL), F32),
        compiler_params=_params(("parallel",)),
        name="out_project_ln",
    )(x, y1, y2, w, w, mod, g, b)


FFN_CHUNKS = 2
FFN_FC = D_FF // FFN_CHUNKS


def _ffn_kernel(x_ref, xp_ref, xn_ref, mod_ref, wa_ref, wg_ref, cwa_ref, cwg_ref, cba_ref, cbg_ref, wd_ref,
                g_ref, b_ref, o_ref, h_ref, ua_ref, ug_ref, acc_ref):
    i = pl.program_id(0)
    c = pl.program_id(1)
    tm = x_ref.shape[0]

    @pl.when(c == 0)
    def _():
        shift = mod_ref[3:4, :]
        scale = 1.0 + mod_ref[4:5, :]
        hp = jnp.where(i > 0, xp_ref[...] * scale + shift, 0.0)
        hn = jnp.where(i < pl.num_programs(0) - 1, xn_ref[...] * scale + shift, 0.0)
        hm = x_ref[...] * scale + shift
        h_ref[...] = jnp.concatenate([hp, hm, hn], axis=0).astype(BF16)
        acc_ref[...] = jnp.zeros(acc_ref.shape, F32)

    h = h_ref[...]
    ua_ref[...] = jnp.dot(h, wa_ref[...], preferred_element_type=F32)
    ug_ref[...] = jnp.dot(h, wg_ref[...], preferred_element_type=F32)

    def conv(u_ref, cw_ref, cb_ref):
        return (cb_ref[...]
                + u_ref[HALO - 1:HALO - 1 + tm, :] * cw_ref[0:1, :]
                + u_ref[HALO:HALO + tm, :] * cw_ref[1:2, :]
                + u_ref[HALO + 1:HALO + 1 + tm, :] * cw_ref[2:3, :])

    a = conv(ua_ref, cwa_ref, cba_ref)
    gt = conv(ug_ref, cwg_ref, cbg_ref)
    act = (gt * jax.nn.sigmoid(gt) * a).astype(BF16)
    acc_ref[...] += jnp.dot(act, wd_ref[...], preferred_element_type=F32)

    @pl.when(c == pl.num_programs(1) - 1)
    def _():
        o_ref[...] = _residual_layer_norm(x_ref[...], acc_ref[...], mod_ref[5:6, :], g_ref[...], b_ref[...])


def _conv_ffn(x, mod, w_up, conv_w, conv_b, w_down, g, b, tm):
    n = x.shape[0]
    nrow = n // tm
    nh = n // HALO
    per = tm // HALO
    nc = FFN_CHUNKS
    return pl.pallas_call(
        _ffn_kernel,
        grid=(nrow, nc),
        in_specs=[pl.BlockSpec((tm, D_MODEL), lambda i, c: (i, 0)),
                  pl.BlockSpec((HALO, D_MODEL), lambda i, c: (jnp.maximum(i * per - 1, 0), 0)),
                  pl.BlockSpec((HALO, D_MODEL), lambda i, c: (jnp.minimum((i + 1) * per, nh - 1), 0)),
                  _full((6, D_MODEL)),
                  pl.BlockSpec((D_MODEL, FFN_FC), lambda i, c: (0, c)),
                  pl.BlockSpec((D_MODEL, FFN_FC), lambda i, c: (0, nc + c)),
                  pl.BlockSpec((3, FFN_FC), lambda i, c: (0, c)),
                  pl.BlockSpec((3, FFN_FC), lambda i, c: (0, nc + c)),
                  pl.BlockSpec((1, FFN_FC), lambda i, c: (0, c)),
                  pl.BlockSpec((1, FFN_FC), lambda i, c: (0, nc + c)),
                  pl.BlockSpec((FFN_FC, D_MODEL), lambda i, c: (c, 0)),
                  _full((1, D_MODEL)), _full((1, D_MODEL))],
        out_specs=pl.BlockSpec((tm, D_MODEL), lambda i, c: (i, 0)),
        out_shape=jax.ShapeDtypeStruct((n, D_MODEL), F32),
        scratch_shapes=[pltpu.VMEM((tm + 2 * HALO, D_MODEL), BF16),
                        pltpu.VMEM((tm + 2 * HALO, FFN_FC), F32),
                        pltpu.VMEM((tm + 2 * HALO, FFN_FC), F32),
                        pltpu.VMEM((tm, D_MODEL), F32)],
        compiler_params=_params(("parallel", "arbitrary")),
        name="conv_ffn_ln",
    )(x, x, x, mod, w_up, w_up, conv_w, conv_w, conv_b, conv_b, w_down, g, b)


def _pair_perm(n_heads, kv_heads):
    per = n_heads // kv_heads
    order = []
    for j in range(per):
        for g in range(kv_heads):
            order.append(g * per + j)
    cols = np.concatenate([np.arange(h * HEAD_DIM, (h + 1) * HEAD_DIM) for h in order])
    return cols


_PERM_A = _pair_perm(A_HEADS, A_KV_HEADS)
_PERM_C = _pair_perm(C_HEADS, C_KV_HEADS)


def _rope_tables(n):
    t = np.arange(n)
    row = (t // GRID_W).astype(np.float32)
    col = (t % GRID_W).astype(np.float32)
    half = HEAD_DIM // 2
    inv = jnp.asarray(ROPE_THETA, F32) ** (-jnp.arange(0, half, 2, dtype=F32) / half)
    ang_r = jnp.asarray(row)[:, None] * inv
    ang_c = jnp.asarray(col)[:, None] * inv
    ang = jnp.concatenate([ang_r, ang_r, ang_c, ang_c], -1)
    ang = jnp.concatenate([ang, ang], -1)
    sign = np.where(np.arange(LANES) % 32 < 16, -1.0, 1.0).astype(np.float32)
    return jnp.cos(ang), jnp.sin(ang) * sign


_EVEN_GROUPS = (
    (0, A_Q, 0, True, QK_SCALE),
    (A_Q, A_KV, A_Q, True, 1.0),
    (A_Q + A_KV, A_KV, None, False, 1.0),
    (A_Q + 2 * A_KV, B_W, None, False, QK_SCALE),
    (A_Q + 2 * A_KV + B_W, B_W, None, False, 1.0),
    (A_Q + 2 * A_KV + 2 * B_W, B_W, None, False, 1.0),
)
_EVEN_GROUPS_CTX = tuple((c0, w, g0, False, s) for (c0, w, g0, _, s) in _EVEN_GROUPS)
_ODD_GROUPS = (
    (0, C_Q, None, True, QK_SCALE),
    (C_Q, C_KV, None, True, 1.0),
    (C_Q + C_KV, C_KV, None, False, 1.0),
)
_ODD_GROUPS_CTX = tuple((c0, w, g0, False, s) for (c0, w, g0, _, s) in _ODD_GROUPS)

LAT_TM = 512


def kernel(x, c, ctx, c_ctx, ada_w, ada_b, ln_g, ln_b, ev_w_in, ev_w_out, ev_q_gain, ev_k_gain, ev_rpb,
           od_w_in, od_w_out, od_sink, ffn_w_up, ffn_conv_w, ffn_conv_b, ffn_w_down):
    n = x.shape[1]
    nc = ctx.shape[1]
    x_lat = x[0]
    x_ctx = ctx[0]

    cvec = jnp.zeros((8, D_MODEL), F32).at[0].set(c[0]).at[1].set(c_ctx)
    mods = _modulation(cvec, ada_w, ada_b)

    cos, sin = _rope_tables(n)
    cos_c, sin_c = cos[:nc], sin[:nc]
    lane = np.arange(LANES)
    gmat = jnp.asarray((lane[:, None] // HEAD_DIM) == (lane[None, :] // HEAD_DIM), BF16)
    no_sink = jnp.full((C_HEADS,), NEG, F32)

    for l in range(DEPTH):
        i = l // 2
        ctx_out = l < DEPTH - 1
        m_lat = mods[l, 0].reshape(6, D_MODEL)
        m_ctx = mods[l, 1].reshape(6, D_MODEL)
        g0, b0 = ln_g[l, 0][None], ln_b[l, 0][None]
        g1, b1 = ln_g[l, 1][None], ln_b[l, 1][None]
        if l % 2 == 0:
            w_in = ev_w_in[i]
            w_in = jnp.concatenate([w_in[:, :A_Q][:, _PERM_A], w_in[:, A_Q:]], axis=1).astype(BF16)
            w_out = ev_w_out[i]
            w_out = jnp.concatenate([w_out[:A_Q][_PERM_A], w_out[A_Q:]], axis=0).astype(BF16)
            gains = jnp.concatenate([jnp.tile(ev_q_gain[i], A_HEADS), jnp.tile(ev_k_gain[i], A_KV_HEADS)])[None]
            qa, ka, va, qb, kb, vb = _project(x_lat, m_lat, w_in, gains, cos, sin, gmat, _EVEN_GROUPS, LAT_TM)
            qa_c, ka_c, va_c, qb_c, kb_c, vb_c = _project(x_ctx, m_ctx, w_in, gains, cos_c, sin_c, gmat,
                                                           _EVEN_GROUPS_CTX, nc)
            ya = _global_attention(qa, ka, va, ka_c, va_c)
            yb = _nbr_attention(qb, kb, vb, kb_c, vb_c, _nbr_bias_tiles(ev_rpb[i] * LOG2E))
            x_lat = _out_project(x_lat, ya, yb, 0, w_out, m_lat, g0, b0, LAT_TM)
            if ctx_out:
                ya_c = _ctx_attention(qa_c, ka_c, va_c, no_sink, 0)
                yb_c = _ctx_attention(qb_c, kb_c, vb_c, no_sink, 1)
                x_ctx = _out_project(x_ctx, ya_c, yb_c, 0, w_out, m_ctx, g0, b0, nc)
        else:
            w_in = od_w_in[i]
            w_in = jnp.concatenate([w_in[:, :C_Q][:, _PERM_C], w_in[:, C_Q:]], axis=1).astype(BF16)
            w_out = od_w_out[i][_PERM_C].astype(BF16)
            sink = od_sink[i] * LOG2E
            gains = jnp.ones((1, A_Q + A_KV), F32)
            q, k, v = _project(x_lat, m_lat, w_in, gains, cos, sin, gmat, _ODD_GROUPS, LAT_TM)
            q_c, k_c, v_c = _project(x_ctx, m_ctx, w_in, gains, cos_c, sin_c, gmat, _ODD_GROUPS_CTX, nc)
            y = _window_attention(q, k, v, k_c, v_c, sink)
            x_lat = _out_project(x_lat, y, y, 1, w_out, m_lat, g0, b0, LAT_TM)
            if ctx_out:
                y_c = _ctx_attention(q_c, k_c, v_c, sink, 0)
                x_ctx = _out_project(x_ctx, y_c, y_c, 1, w_out, m_ctx, g0, b0, nc)
        w_up = ffn_w_up[l].astype(BF16)
        w_down = ffn_w_down[l].astype(BF16)
        cb = ffn_conv_b[l][None]
        x_lat = _conv_ffn(x_lat, m_lat, w_up, ffn_conv_w[l], cb, w_down, g1, b1, LAT_TM)
        if ctx_out:
            x_ctx = _conv_ffn(x_ctx, m_ctx, w_up, ffn_conv_w[l], cb, w_down, g1, b1, nc)
    return x_lat[None]
```

```python
import functools

import numpy as np
import jax
import jax.numpy as jnp
from jax import lax
from jax.experimental import pallas as pl
from jax.experimental.pallas import tpu as pltpu

D_MODEL = 1024
DEPTH = 4
GRID_W = 64
HEAD_DIM = 64
A_HEADS = 8
A_KV_HEADS = 2
B_HEADS = 8
NB_KH = 8
NB_KW = 16
C_HEADS = 16
C_KV_HEADS = 2
C_WINDOW = 128
D_FF = 2816
ROPE_THETA = 10000.0
LN_EPS = 1e-5
RMS_EPS = 1e-6
NEG = -1e30
BIG = 1e30
DN_ALPHA = (2 * DEPTH) ** 0.25
A_Q = A_HEADS * HEAD_DIM
A_KV = A_KV_HEADS * HEAD_DIM
B_W = B_HEADS * HEAD_DIM
C_Q = C_HEADS * HEAD_DIM
C_KV = C_KV_HEADS * HEAD_DIM
LOG2E = 1.4426950408889634
QK_SCALE = HEAD_DIM ** -0.5 * LOG2E
SAFE_LOG2 = 64.0

LANES = 128
SUBLANES = 8
HALO = SUBLANES
VMEM_LIMIT = 52 * 1024 * 1024

BF16 = jnp.bfloat16
F32 = jnp.float32

_NT = (((1,), (1,)), ((), ()))


def _params(sem):
    return pltpu.CompilerParams(dimension_semantics=sem, vmem_limit_bytes=VMEM_LIMIT)


def _full(shape):
    return pl.BlockSpec(shape, lambda *_: (0,) * len(shape))


def _lane_lo(shape):
    return lax.broadcasted_iota(jnp.int32, shape, len(shape) - 1) < HEAD_DIM


def _unit_lane(half):
    return HEAD_DIM if half == 0 else 0


def _unit_lane_variants(x):
    lane = jnp.arange(x.shape[-1]) % LANES
    one = jnp.ones((), x.dtype)
    zero = jnp.zeros((), x.dtype)
    lo = jnp.where(lane < HEAD_DIM, x, jnp.where(lane == HEAD_DIM, one, zero))
    hi = jnp.where(lane >= HEAD_DIM, x, jnp.where(lane == 0, one, zero))
    return jnp.stack([lo, hi])


def _running_max(top, s):
    cm = s[:, 0:LANES]
    for j in range(1, s.shape[1] // LANES):
        cm = jnp.maximum(cm, s[:, j * LANES:(j + 1) * LANES])
    return jnp.maximum(top, jnp.max(cm.reshape(s.shape[0] // SUBLANES, SUBLANES, LANES), axis=0))


def _running_min_of_lane(low, x, unit):
    lane = lax.broadcasted_iota(jnp.int32, x.shape, 1)
    picked = jnp.where(lane == unit, x, BIG)
    return jnp.minimum(low, jnp.min(picked.reshape(x.shape[0] // SUBLANES, SUBLANES, LANES), axis=0))


def _outside_safe_range(flags):
    f = flags.reshape(-1, 2, SUBLANES, LANES)
    return jnp.logical_not((jnp.max(f[:, 0]) < SAFE_LOG2) & (jnp.min(f[:, 1]) > 2.0 ** -SAFE_LOG2))


def _mod_kernel(c_ref, w_ref, b_ref, o_ref):
    s = c_ref[...]
    s = s * jax.nn.sigmoid(s)
    o_ref[0] = jnp.dot(s, w_ref[0], preferred_element_type=F32, precision=lax.Precision.HIGHEST) + b_ref[0]


def _modulation(cvec, ada_w, ada_b):
    nb = 6
    return pl.pallas_call(
        _mod_kernel,
        grid=(DEPTH, nb),
        in_specs=[pl.BlockSpec((8, D_MODEL), lambda l, j: (0, 0)),
                  pl.BlockSpec((1, D_MODEL, D_MODEL), lambda l, j: (l, 0, j)),
                  pl.BlockSpec((1, 1, D_MODEL), lambda l, j: (l, 0, j))],
        out_specs=pl.BlockSpec((1, 8, D_MODEL), lambda l, j: (l, 0, j)),
        out_shape=jax.ShapeDtypeStruct((DEPTH, 8, 6 * D_MODEL), F32),
        compiler_params=_params(("arbitrary", "arbitrary")),
        name="modulation",
    )(cvec, ada_w, ada_b.reshape(DEPTH, 1, 6 * D_MODEL))


def _group_sumsq(z, gmat):
    x2 = z * z
    hi = x2.astype(BF16)
    lo = (x2 - hi.astype(F32)).astype(BF16)
    return (jnp.dot(hi, gmat, preferred_element_type=F32) + jnp.dot(lo, gmat, preferred_element_type=F32))


def _proj_kernel(groups, x_ref, mod_ref, w_ref, gain_ref, cos_ref, sin_ref, gmat_ref, *out_refs):
    shift = mod_ref[0:1, :]
    scale = mod_ref[1:2, :]
    h = (x_ref[...] * (1.0 + scale) + shift).astype(BF16)
    cos = cos_ref[...]
    sin = sin_ref[...]
    gmat = gmat_ref[...]
    first = lax.broadcasted_iota(jnp.int32, cos.shape, 1) % 32 < 16
    for (c0, width, g0, rope, qscale), o_ref in zip(groups, out_refs):
        z = jnp.dot(h, w_ref[:, c0:c0 + width], preferred_element_type=F32)
        for b in range(width // LANES):
            zb = z[:, b * LANES:(b + 1) * LANES]
            if g0 is not None:
                ms = _group_sumsq(zb, gmat) * (1.0 / HEAD_DIM)
                zb = zb * lax.rsqrt(ms + RMS_EPS) * gain_ref[:, g0 + b * LANES:g0 + (b + 1) * LANES]
            if rope:
                rot = jnp.where(first, pltpu.roll(zb, LANES - 16, 1), pltpu.roll(zb, 16, 1))
                zb = zb * cos + rot * sin
            if qscale != 1.0:
                zb = zb * qscale
            o_ref[:, b * LANES:(b + 1) * LANES] = zb.astype(BF16)


def _project(x, mod, w, gains, cos, sin, gmat, groups, tm):
    n = x.shape[0]
    win = w.shape[1]
    out_shape = [jax.ShapeDtypeStruct((n, g[1]), BF16) for g in groups]
    out_specs = [pl.BlockSpec((tm, g[1]), lambda i: (i, 0)) for g in groups]
    return pl.pallas_call(
        functools.partial(_proj_kernel, groups),
        grid=(n // tm,),
        in_specs=[pl.BlockSpec((tm, D_MODEL), lambda i: (i, 0)),
                  _full((6, D_MODEL)),
                  _full((D_MODEL, win)),
                  _full(gains.shape),
                  pl.BlockSpec((tm, LANES), lambda i: (i, 0)),
                  pl.BlockSpec((tm, LANES), lambda i: (i, 0)),
                  _full((LANES, LANES))],
        out_specs=out_specs,
        out_shape=out_shape,
        compiler_params=_params(("parallel",)),
        name="qkv_project",
    )(x, mod, w, gains, cos, sin, gmat)


GA_TQ = 512
GA_TK = 2048


GA_VROWS = 80


def _vt_variants(v):
    t = v.shape[0]
    tail = jnp.concatenate([jnp.ones((1, t), v.dtype), jnp.zeros((GA_VROWS - HEAD_DIM - 1, t), v.dtype)])
    return jnp.stack([jnp.concatenate([v[:, :HEAD_DIM].T, tail]), jnp.concatenate([v[:, HEAD_DIM:].T, tail])])


def _global_attn_kernel(q_ref, kc_ref, vct_ref, k_ref, vt_ref, o_ref, flag_ref, qx_ref, acc_ref):
    kk = pl.program_id(1)
    tq = q_ref.shape[0]
    nblk = q_ref.shape[1] // LANES

    def update(kb_ref, vtb_ref):
        top = flag_ref[0:SUBLANES, :]
        kb = kb_ref[...]
        for j in range(nblk):
            st = [lax.dot_general(kb, qx_ref[2 * j + half], _NT, preferred_element_type=F32) for half in range(2)]
            for half in range(2):
                top = _running_max(top, st[half])
                acc_ref[2 * j + half] += jnp.dot(vtb_ref[half], jnp.exp2(st[half]).astype(BF16),
                                                 preferred_element_type=F32)
        flag_ref[0:SUBLANES, :] = top

    @pl.when(kk == 0)
    def _():
        lo = _lane_lo((tq, LANES))
        for j in range(nblk):
            qj = q_ref[:, j * LANES:(j + 1) * LANES]
            zero = jnp.zeros_like(qj)
            qx_ref[2 * j] = jnp.where(lo, qj, zero)
            qx_ref[2 * j + 1] = jnp.where(lo, zero, qj)
        flag_ref[0:SUBLANES, :] = jnp.full((SUBLANES, LANES), NEG, F32)
        acc_ref[...] = jnp.zeros(acc_ref.shape, F32)
        update(kc_ref, vct_ref)

    update(k_ref, vt_ref)

    @pl.when(kk == pl.num_programs(1) - 1)
    def _():
        low = jnp.full((1, LANES), BIG, F32)
        for j in range(nblk):
            halves = []
            for half in range(2):
                a = acc_ref[2 * j + half]
                den = a[HEAD_DIM:HEAD_DIM + 1, :]
                for t in range(tq // LANES):
                    low = jnp.minimum(low, den[:, t * LANES:(t + 1) * LANES])
                halves.append(a[0:HEAD_DIM, :] / den)
            o_ref[:, j * LANES:(j + 1) * LANES] = jnp.concatenate(halves, axis=0).T.astype(o_ref.dtype)
        flag_ref[SUBLANES:2 * SUBLANES, :] = jnp.broadcast_to(low, (SUBLANES, LANES))


def _global_attn_exact_kernel(q_ref, kc_ref, vc_ref, k_ref, v_ref, o_ref, qx_ref, c_ref, acc_ref):
    kk = pl.program_id(1)
    tq = q_ref.shape[0]
    nblk = q_ref.shape[1] // LANES
    nh = 2 * nblk
    lane = lax.broadcasted_iota(jnp.int32, (tq, LANES), 1)

    def moving_reference_update(kx_ref, vx_ref, first):
        for h in range(nh):
            half = h % 2
            unit = _unit_lane(half)
            sp = lax.dot_general(qx_ref[h], kx_ref[half], _NT, preferred_element_type=F32)
            mb = jnp.max(sp, axis=-1, keepdims=True)
            c_old = c_ref[h]
            target = mb if first else jnp.maximum(c_old, c_old + mb)
            qx_ref[h] = jnp.where(lane == unit, -target, qx_ref[h].astype(F32)).astype(BF16)
            c_new = -qx_ref[h][:, unit:unit + 1].astype(F32)
            d = c_new - c_old
            pv = jnp.dot(jnp.exp2(sp - d).astype(BF16), vx_ref[half], preferred_element_type=F32)
            acc_ref[h] = pv if first else acc_ref[h] * jnp.exp2(-d) + pv
            c_ref[h] = c_new

    @pl.when(kk == 0)
    def _():
        lo = _lane_lo((tq, LANES))
        for j in range(nblk):
            qj = q_ref[:, j * LANES:(j + 1) * LANES]
            zero = jnp.zeros_like(qj)
            qx_ref[2 * j] = jnp.where(lo, qj, zero)
            qx_ref[2 * j + 1] = jnp.where(lo, zero, qj)
        c_ref[...] = jnp.zeros(c_ref.shape, F32)
        moving_reference_update(kc_ref, vc_ref, True)

    moving_reference_update(k_ref, v_ref, False)

    @pl.when(kk == pl.num_programs(1) - 1)
    def _():
        lo = _lane_lo((tq, LANES))
        for j in range(nblk):
            a_lo = acc_ref[2 * j]
            a_hi = acc_ref[2 * j + 1]
            o_lo = a_lo / a_lo[:, HEAD_DIM:HEAD_DIM + 1]
            o_hi = a_hi / a_hi[:, 0:1]
            o_ref[:, j * LANES:(j + 1) * LANES] = jnp.where(lo, o_lo, o_hi).astype(o_ref.dtype)


def _global_attention(q, k, v, kc, vc):
    n, qw = q.shape
    nc = kc.shape[0]
    nh = 2 * (qw // LANES)
    nq = n // GA_TQ
    grid = (nq, n // GA_TK)
    q_spec = pl.BlockSpec((GA_TQ, qw), lambda i, j: (i, 0))

    y, flags = pl.pallas_call(
        _global_attn_kernel,
        grid=grid,
        in_specs=[q_spec,
                  _full((nc, LANES)), _full((2, GA_VROWS, nc)),
                  pl.BlockSpec((GA_TK, LANES), lambda i, j: (j, 0)),
                  pl.BlockSpec((2, GA_VROWS, GA_TK), lambda i, j: (0, 0, j))],
        out_specs=[q_spec, pl.BlockSpec((2 * SUBLANES, LANES), lambda i, j: (i, 0))],
        out_shape=[jax.ShapeDtypeStruct((n, qw), BF16),
                   jax.ShapeDtypeStruct((nq * 2 * SUBLANES, LANES), F32)],
        scratch_shapes=[pltpu.VMEM((nh, GA_TQ, LANES), BF16),
                        pltpu.VMEM((nh, GA_VROWS, GA_TQ), F32)],
        compiler_params=_params(("parallel", "arbitrary")),
        name="global_attention",
    )(q, kc, _vt_variants(vc), k, _vt_variants(v))

    def exact():
        return pl.pallas_call(
            _global_attn_exact_kernel,
            grid=grid,
            in_specs=[q_spec,
                      _full((2, nc, LANES)), _full((2, nc, LANES)),
                      pl.BlockSpec((2, GA_TK, LANES), lambda i, j: (0, j, 0)),
                      pl.BlockSpec((2, GA_TK, LANES), lambda i, j: (0, j, 0))],
            out_specs=q_spec,
            out_shape=jax.ShapeDtypeStruct((n, qw), BF16),
            scratch_shapes=[pltpu.VMEM((nh, GA_TQ, LANES), BF16),
                            pltpu.VMEM((nh, GA_TQ, 1), F32),
                            pltpu.VMEM((nh, GA_TQ, LANES), F32)],
            compiler_params=_params(("parallel", "arbitrary")),
            name="global_attention_exact",
        )(q, _unit_lane_variants(kc), _unit_lane_variants(vc), _unit_lane_variants(k), _unit_lane_variants(v))

    return lax.cond(_outside_safe_range(flags), exact, lambda: y)


NB_QROWS = 8
NB_KROWS = NB_QROWS + NB_KH
NB_MASKED = 2 * NB_KH - 1
NB_TILES = 3 * (NB_MASKED + 1)


def _nbr_bias_tiles(rpb):
    nh = rpb.shape[0]
    col = np.arange(GRID_W)
    cs = np.clip(col - NB_KW // 2, 0, GRID_W - NB_KW)
    col_ok = (col[None, :] >= cs[:, None]) & (col[None, :] < cs[:, None] + NB_KW)
    dcol = col[None, :] - col[:, None] + NB_KW - 1
    sel_col = (dcol[:, :, None] == np.arange(2 * NB_KW - 1)) & col_ok[:, :, None]
    base = jnp.einsum('qkj,hdj->hdqk', jnp.asarray(sel_col, F32), rpb, precision=lax.Precision.HIGHEST)
    base = jnp.where(jnp.asarray(col_ok), base, NEG)
    masked = jnp.full((nh, 1, GRID_W, GRID_W), NEG, F32)
    base = jnp.concatenate([base, masked], axis=1)
    nxt = jnp.concatenate([base[:, 1:], masked], axis=1)
    allmasked = jnp.full_like(base, NEG)
    return jnp.concatenate([jnp.concatenate([base, nxt], -1),
                            jnp.concatenate([base, allmasked], -1),
                            jnp.concatenate([allmasked, base], -1)], axis=1)


def _nbr_tile_index(rows):
    nq = rows // NB_QROWS
    idx = np.zeros((3, NB_QROWS, NB_KROWS // 2), np.int64)
    for v, blk in enumerate((0, 1, nq - 1)):
        r0 = blk * NB_QROWS
        k0 = int(np.clip(r0 - NB_KH // 2, 0, rows - NB_KROWS))
        for qr in range(NB_QROWS):
            qa = r0 + qr
            rs = int(np.clip(qa - NB_KH // 2, 0, rows - NB_KH))

            def code(kr):
                ka = k0 + kr
                return ka - qa + NB_KH - 1 if rs <= ka < rs + NB_KH else NB_MASKED

            for kp in range(NB_KROWS // 2):
                left, right = code(2 * kp), code(2 * kp + 1)
                if left < NB_MASKED and right < NB_MASKED:
                    assert right == left + 1
                    idx[v, qr, kp] = left
                elif left < NB_MASKED:
                    idx[v, qr, kp] = (NB_MASKED + 1) + left
                else:
                    idx[v, qr, kp] = 2 * (NB_MASKED + 1) + right
    return tuple(tuple(tuple(int(t) for t in row) for row in var) for var in idx)


def _nbr_attn_kernel(exact, tile_index, q_ref, k_ref, v_ref, kc_ref, vc_ref, tiles_ref, o_ref, flag_ref, bias_ref):
    i = pl.program_id(1)
    nq = pl.num_programs(1)
    rows = k_ref.shape[0] // GRID_W
    span = NB_KROWS * GRID_W

    def fill_bias(variant):
        for half in range(2):
            for qr in range(NB_QROWS):
                for kp in range(NB_KROWS // 2):
                    bias_ref[half, qr * GRID_W:(qr + 1) * GRID_W, kp * LANES:(kp + 1) * LANES] = (
                        tiles_ref[half, tile_index[variant][qr][kp]])

    for variant, at in enumerate((0, 1, nq - 1)):
        pl.when(i == at)(functools.partial(fill_bias, variant))

    r0 = i * NB_QROWS
    ks = pl.multiple_of(jnp.clip(r0 - NB_KH // 2, 0, rows - NB_KROWS) * GRID_W, GRID_W)
    kw = k_ref[pl.ds(ks, span), :]
    kc = kc_ref[...]
    q = q_ref[...]
    lo = _lane_lo(q.shape)
    zero = jnp.zeros_like(q)
    top = jnp.full((SUBLANES, LANES), NEG, F32)
    low = jnp.full((SUBLANES, LANES), BIG, F32)
    outs = []
    logits = []
    for half in range(2):
        qm = jnp.where(lo, q, zero) if half == 0 else jnp.where(lo, zero, q)
        logits.append((lax.dot_general(qm, kw, _NT, preferred_element_type=F32),
                       lax.dot_general(qm, kc, _NT, preferred_element_type=F32)))
    for half in range(2):
        s_nb = logits[half][0] + bias_ref[half]
        s_cx = logits[half][1]
        if exact:
            m = jnp.maximum(jnp.max(s_nb, axis=-1, keepdims=True), jnp.max(s_cx, axis=-1, keepdims=True))
            s_nb = s_nb - m
            s_cx = s_cx - m
        else:
            top = _running_max(_running_max(top, s_nb), s_cx)
        pv = (jnp.dot(jnp.exp2(s_nb).astype(BF16), v_ref[half, pl.ds(ks, span), :], preferred_element_type=F32)
              + jnp.dot(jnp.exp2(s_cx).astype(BF16), vc_ref[half], preferred_element_type=F32))
        unit = _unit_lane(half)
        low = _running_min_of_lane(low, pv, unit)
        outs.append(pv / pv[:, unit:unit + 1])
    o_ref[...] = jnp.where(lo, outs[0], outs[1]).astype(o_ref.dtype)
    flag_ref[0:SUBLANES, :] = top
    flag_ref[SUBLANES:2 * SUBLANES, :] = low


def _nbr_attention(q, k, v, kc, vc, tiles):
    n, qw = q.shape
    nc = kc.shape[0]
    tq = NB_QROWS * GRID_W
    nq = n // tq
    assert nq >= 3
    npair = qw // LANES
    tile_index = _nbr_tile_index(n // GRID_W)
    operands = (q, k, _unit_lane_variants(v), kc, _unit_lane_variants(vc), tiles)

    def run(exact):
        return pl.pallas_call(
            functools.partial(_nbr_attn_kernel, exact, tile_index),
            grid=(npair, nq),
            in_specs=[pl.BlockSpec((tq, LANES), lambda p, i: (i, p)),
                      pl.BlockSpec((n, LANES), lambda p, i: (0, p)),
                      pl.BlockSpec((2, n, LANES), lambda p, i: (0, 0, p)),
                      pl.BlockSpec((nc, LANES), lambda p, i: (0, p)),
                      pl.BlockSpec((2, nc, LANES), lambda p, i: (0, 0, p)),
                      pl.BlockSpec((2, NB_TILES, GRID_W, LANES), lambda p, i: (p, 0, 0, 0))],
            out_specs=[pl.BlockSpec((tq, LANES), lambda p, i: (i, p)),
                       pl.BlockSpec((2 * SUBLANES, LANES), lambda p, i: (p * nq + i, 0))],
            out_shape=[jax.ShapeDtypeStruct((n, qw), BF16),
                       jax.ShapeDtypeStruct((npair * nq * 2 * SUBLANES, LANES), F32)],
            scratch_shapes=[pltpu.VMEM((2, tq, NB_KROWS * GRID_W), F32)],
            compiler_params=_params(("parallel", "arbitrary")),
            name="neighbourhood_attention_exact" if exact else "neighbourhood_attention",
        )(*operands)

    y, flags = run(False)
    return lax.cond(_outside_safe_range(flags), lambda: run(True)[0], lambda: y)


WIN_TQ = 256
WIN_SPAN = WIN_TQ + 2 * C_WINDOW


def _window_attn_kernel(exact, sink_ref, q_ref, k_ref, v_ref, kc_ref, vc_ref, o_ref, flag_ref):
    i = pl.program_id(0)
    n = k_ref.shape[0]
    tq = q_ref.shape[0]
    nblk = q_ref.shape[1] // LANES
    q0 = i * tq
    ks = pl.multiple_of(jnp.clip(q0 - C_WINDOW, 0, n - WIN_SPAN), C_WINDOW)
    kw = k_ref[pl.ds(ks, WIN_SPAN), :]
    kc = kc_ref[...]
    qpos = q0 + lax.broadcasted_iota(jnp.int32, (tq, WIN_SPAN), 0)
    kpos = ks + lax.broadcasted_iota(jnp.int32, (tq, WIN_SPAN), 1)
    valid = jnp.abs(kpos - qpos) <= C_WINDOW
    lo = _lane_lo((tq, LANES))
    top = jnp.full((SUBLANES, LANES), NEG, F32)
    low = jnp.full((SUBLANES, LANES), BIG, F32)
    for j in range(nblk):
        q = q_ref[:, j * LANES:(j + 1) * LANES]
        zero = jnp.zeros_like(q)
        outs = []
        logits = []
        for half in range(2):
            qm = jnp.where(lo, q, zero) if half == 0 else jnp.where(lo, zero, q)
            logits.append((lax.dot_general(qm, kw, _NT, preferred_element_type=F32),
                           lax.dot_general(qm, kc, _NT, preferred_element_type=F32)))
        for half in range(2):
            sink = jnp.full((tq, 1), sink_ref[j + nblk * half], F32)
            s_w = jnp.where(valid, logits[half][0], NEG)
            s_c = logits[half][1]
            if exact:
                m = jnp.maximum(jnp.maximum(jnp.max(s_w, axis=-1, keepdims=True),
                                            jnp.max(s_c, axis=-1, keepdims=True)), sink)
                s_w = s_w - m
                s_c = s_c - m
                sink = sink - m
            else:
                top = jnp.maximum(_running_max(_running_max(top, s_w), s_c), sink_ref[j + nblk * half])
            pv = (jnp.dot(jnp.exp2(s_w).astype(BF16), v_ref[half, pl.ds(ks, WIN_SPAN), :],
                          preferred_element_type=F32)
                  + jnp.dot(jnp.exp2(s_c).astype(BF16), vc_ref[half], preferred_element_type=F32))
            unit = _unit_lane(half)
            den = pv[:, unit:unit + 1] + jnp.exp2(sink)
            low = _running_min_of_lane(low, jnp.broadcast_to(den, pv.shape), unit)
            outs.append(pv / den)
        o_ref[:, j * LANES:(j + 1) * LANES] = jnp.where(lo, outs[0], outs[1]).astype(o_ref.dtype)
    flag_ref[0:SUBLANES, :] = top
    flag_ref[SUBLANES:2 * SUBLANES, :] = low


def _window_attention(q, k, v, kc, vc, sink):
    n, qw = q.shape
    nc = kc.shape[0]
    nq = n // WIN_TQ
    operands = (sink, q, k, _unit_lane_variants(v), kc, _unit_lane_variants(vc))

    def run(exact):
        return pl.pallas_call(
            functools.partial(_window_attn_kernel, exact),
            grid=(nq,),
            in_specs=[pl.BlockSpec(memory_space=pltpu.SMEM),
                      pl.BlockSpec((WIN_TQ, qw), lambda i: (i, 0)),
                      _full((n, LANES)), _full((2, n, LANES)),
                      _full((nc, LANES)), _full((2, nc, LANES))],
            out_specs=[pl.BlockSpec((WIN_TQ, qw), lambda i: (i, 0)),
                       pl.BlockSpec((2 * SUBLANES, LANES), lambda i: (i, 0))],
            out_shape=[jax.ShapeDtypeStruct((n, qw), BF16),
                       jax.ShapeDtypeStruct((nq * 2 * SUBLANES, LANES), F32)],
            compiler_params=_params(("parallel",)),
            name="window_attention_exact" if exact else "window_attention",
        )(*operands)

    y, flags = run(False)
    return lax.cond(_outside_safe_range(flags), lambda: run(True)[0], lambda: y)


def _ctx_attn_kernel(sink_ref, q_ref, k_ref, v_ref, o_ref):
    j = pl.program_id(0)
    nblk = pl.num_programs(0)
    q = q_ref[...]
    k = k_ref[...]
    v = v_ref[...]
    lo = _lane_lo(q.shape)
    zero = jnp.zeros_like(q)
    outs = []
    for half in range(2):
        qm = jnp.where(lo, q, zero) if half == 0 else jnp.where(lo, zero, q)
        sink = sink_ref[j + nblk * half]
        s = lax.dot_general(qm, k, _NT, preferred_element_type=F32)
        m = jnp.maximum(jnp.max(s, axis=-1, keepdims=True), sink)
        p = jnp.exp2(s - m)
        den = jnp.sum(p, axis=-1, keepdims=True) + jnp.exp2(sink - m)
        outs.append(jnp.dot(p.astype(BF16), v, preferred_element_type=F32) / den)
    o_ref[...] = jnp.where(lo, outs[0], outs[1]).astype(o_ref.dtype)


def _ctx_attention(q, k, v, sink, kv_per_block):
    c, qw = q.shape
    nblk = qw // LANES
    return pl.pallas_call(
        _ctx_attn_kernel,
        grid=(nblk,),
        in_specs=[pl.BlockSpec(memory_space=pltpu.SMEM),
                  pl.BlockSpec((c, LANES), lambda j: (0, j)),
                  pl.BlockSpec((c, LANES), lambda j: (0, j * kv_per_block)),
                  pl.BlockSpec((c, LANES), lambda j: (0, j * kv_per_block))],
        out_specs=pl.BlockSpec((c, LANES), lambda j: (0, j)),
        out_shape=jax.ShapeDtypeStruct((c, qw), BF16),
        compiler_params=_params(("parallel",)),
        name="context_attention",
    )(sink, q, k, v)


def _residual_layer_norm(x, f, gate, g, b):
    z = DN_ALPHA * x + gate * f
    mu = jnp.mean(z, axis=-1, keepdims=True)
    zc = z - mu
    var = jnp.mean(zc * zc, axis=-1, keepdims=True)
    return zc * lax.rsqrt(var + LN_EPS) * g + b


def _outproj_kernel(x_ref, y1_ref, y2_ref, w1_ref, w2_ref, mod_ref, g_ref, b_ref, o_ref):
    f = (jnp.dot(y1_ref[...], w1_ref[...], preferred_element_type=F32)
         + jnp.dot(y2_ref[...], w2_ref[...], preferred_element_type=F32))
    o_ref[...] = _residual_layer_norm(x_ref[...], f, mod_ref[2:3, :], g_ref[...], b_ref[...])


def _out_project(x, y1, y2, c2, w, mod, g, b, tm):
    n = x.shape[0]
    half = D_MODEL // 2
    return pl.pallas_call(
        _outproj_kernel,
        grid=(n // tm,),
        in_specs=[pl.BlockSpec((tm, D_MODEL), lambda i: (i, 0)),
                  pl.BlockSpec((tm, half), lambda i: (i, 0)),
                  pl.BlockSpec((tm, half), lambda i: (i, c2)),
                  pl.BlockSpec((half, D_MODEL), lambda i: (0, 0)),
                  pl.BlockSpec((half, D_MODEL), lambda i: (1, 0)),
                  _full((6, D_MODEL)), _full((1, D_MODEL)), _full((1, D_MODEL))],
        out_specs=pl.BlockSpec((tm, D_MODEL), lambda i: (i, 0)),
        out_shape=jax.ShapeDtypeStruct((n, D_MODEL), F32),
        compiler_params=_params(("parallel",)),
        name="out_project_ln",
    )(x, y1, y2, w, w, mod, g, b)


FFN_CHUNKS = 2
FFN_FC = D_FF // FFN_CHUNKS


def _ffn_kernel(x_ref, xp_ref, xn_ref, mod_ref, wa_ref, wg_ref, cwa_ref, cwg_ref, cba_ref, cbg_ref, wd_ref,
                g_ref, b_ref, o_ref, h_ref, ua_ref, ug_ref, acc_ref):
    i = pl.program_id(0)
    c = pl.program_id(1)
    tm = x_ref.shape[0]

    @pl.when(c == 0)
    def _():
        shift = mod_ref[3:4, :]
        scale = 1.0 + mod_ref[4:5, :]
        hp = jnp.where(i > 0, xp_ref[...] * scale + shift, 0.0)
        hn = jnp.where(i < pl.num_programs(0) - 1, xn_ref[...] * scale + shift, 0.0)
        hm = x_ref[...] * scale + shift
        h_ref[...] = jnp.concatenate([hp, hm, hn], axis=0).astype(BF16)
        acc_ref[...] = jnp.zeros(acc_ref.shape, F32)

    h = h_ref[...]
    ua_ref[...] = jnp.dot(h, wa_ref[...], preferred_element_type=F32)
    ug_ref[...] = jnp.dot(h, wg_ref[...], preferred_element_type=F32)

    def conv(u_ref, cw_ref, cb_ref):
        return (cb_ref[...]
                + u_ref[HALO - 1:HALO - 1 + tm, :] * cw_ref[0:1, :]
                + u_ref[HALO:HALO + tm, :] * cw_ref[1:2, :]
                + u_ref[HALO + 1:HALO + 1 + tm, :] * cw_ref[2:3, :])

    a = conv(ua_ref, cwa_ref, cba_ref)
    gt = conv(ug_ref, cwg_ref, cbg_ref)
    act = (gt * jax.nn.sigmoid(gt) * a).astype(BF16)
    acc_ref[...] += jnp.dot(act, wd_ref[...], preferred_element_type=F32)

    @pl.when(c == pl.num_programs(1) - 1)
    def _():
        o_ref[...] = _residual_layer_norm(x_ref[...], acc_ref[...], mod_ref[5:6, :], g_ref[...], b_ref[...])


def _conv_ffn(x, mod, w_up, conv_w, conv_b, w_down, g, b, tm):
    n = x.shape[0]
    nrow = n // tm
    nh = n // HALO
    per = tm // HALO
    nc = FFN_CHUNKS
    return pl.pallas_call(
        _ffn_kernel,
        grid=(nrow, nc),
        in_specs=[pl.BlockSpec((tm, D_MODEL), lambda i, c: (i, 0)),
                  pl.BlockSpec((HALO, D_MODEL), lambda i, c: (jnp.maximum(i * per - 1, 0), 0)),
                  pl.BlockSpec((HALO, D_MODEL), lambda i, c: (jnp.minimum((i + 1) * per, nh - 1), 0)),
                  _full((6, D_MODEL)),
                  pl.BlockSpec((D_MODEL, FFN_FC), lambda i, c: (0, c)),
                  pl.BlockSpec((D_MODEL, FFN_FC), lambda i, c: (0, nc + c)),
                  pl.BlockSpec((3, FFN_FC), lambda i, c: (0, c)),
                  pl.BlockSpec((3, FFN_FC), lambda i, c: (0, nc + c)),
                  pl.BlockSpec((1, FFN_FC), lambda i, c: (0, c)),
                  pl.BlockSpec((1, FFN_FC), lambda i, c: (0, nc + c)),
                  pl.BlockSpec((FFN_FC, D_MODEL), lambda i, c: (c, 0)),
                  _full((1, D_MODEL)), _full((1, D_MODEL))],
        out_specs=pl.BlockSpec((tm, D_MODEL), lambda i, c: (i, 0)),
        out_shape=jax.ShapeDtypeStruct((n, D_MODEL), F32),
        scratch_shapes=[pltpu.VMEM((tm + 2 * HALO, D_MODEL), BF16),
                        pltpu.VMEM((tm + 2 * HALO, FFN_FC), F32),
                        pltpu.VMEM((tm + 2 * HALO, FFN_FC), F32),
                        pltpu.VMEM((tm, D_MODEL), F32)],
        compiler_params=_params(("parallel", "arbitrary")),
        name="conv_ffn_ln",
    )(x, x, x, mod, w_up, w_up, conv_w, conv_w, conv_b, conv_b, w_down, g, b)


def _pair_perm(n_heads, kv_heads):
    per = n_heads // kv_heads
    order = []
    for j in range(per):
        for g in range(kv_heads):
            order.append(g * per + j)
    cols = np.concatenate([np.arange(h * HEAD_DIM, (h + 1) * HEAD_DIM) for h in order])
    return cols


_PERM_A = _pair_perm(A_HEADS, A_KV_HEADS)
_PERM_C = _pair_perm(C_HEADS, C_KV_HEADS)


def _rope_tables(n):
    t = np.arange(n)
    row = (t // GRID_W).astype(np.float32)
    col = (t % GRID_W).astype(np.float32)
    half = HEAD_DIM // 2
    inv = jnp.asarray(ROPE_THETA, F32) ** (-jnp.arange(0, half, 2, dtype=F32) / half)
    ang_r = jnp.asarray(row)[:, None] * inv
    ang_c = jnp.asarray(col)[:, None] * inv
    ang = jnp.concatenate([ang_r, ang_r, ang_c, ang_c], -1)
    ang = jnp.concatenate([ang, ang], -1)
    sign = np.where(np.arange(LANES) % 32 < 16, -1.0, 1.0).astype(np.float32)
    return jnp.cos(ang), jnp.sin(ang) * sign


_EVEN_GROUPS = (
    (0, A_Q, 0, True, QK_SCALE),
    (A_Q, A_KV, A_Q, True, 1.0),
    (A_Q + A_KV, A_KV, None, False, 1.0),
    (A_Q + 2 * A_KV, B_W, None, False, QK_SCALE),
    (A_Q + 2 * A_KV + B_W, B_W, None, False, 1.0),
    (A_Q + 2 * A_KV + 2 * B_W, B_W, None, False, 1.0),
)
_EVEN_GROUPS_CTX = tuple((c0, w, g0, False, s) for (c0, w, g0, _, s) in _EVEN_GROUPS)
_ODD_GROUPS = (
    (0, C_Q, None, True, QK_SCALE),
    (C_Q, C_KV, None, True, 1.0),
    (C_Q + C_KV, C_KV, None, False, 1.0),
)
_ODD_GROUPS_CTX = tuple((c0, w, g0, False, s) for (c0, w, g0, _, s) in _ODD_GROUPS)

LAT_TM = 512


def kernel(x, c, ctx, c_ctx, ada_w, ada_b, ln_g, ln_b, ev_w_in, ev_w_out, ev_q_gain, ev_k_gain, ev_rpb,
           od_w_in, od_w_out, od_sink, ffn_w_up, ffn_conv_w, ffn_conv_b, ffn_w_down):
    n = x.shape[1]
    nc = ctx.shape[1]
    x_lat = x[0]
    x_ctx = ctx[0]

    cvec = jnp.zeros((8, D_MODEL), F32).at[0].set(c[0]).at[1].set(c_ctx)
    mods = _modulation(cvec, ada_w, ada_b)

    cos, sin = _rope_tables(n)
    cos_c, sin_c = cos[:nc], sin[:nc]
    lane = np.arange(LANES)
    gmat = jnp.asarray((lane[:, None] // HEAD_DIM) == (lane[None, :] // HEAD_DIM), BF16)
    no_sink = jnp.full((C_HEADS,), NEG, F32)

    for l in range(DEPTH):
        i = l // 2
        ctx_out = l < DEPTH - 1
        m_lat = mods[l, 0].reshape(6, D_MODEL)
        m_ctx = mods[l, 1].reshape(6, D_MODEL)
        g0, b0 = ln_g[l, 0][None], ln_b[l, 0][None]
        g1, b1 = ln_g[l, 1][None], ln_b[l, 1][None]
        if l % 2 == 0:
            w_in = ev_w_in[i]
            w_in = jnp.concatenate([w_in[:, :A_Q][:, _PERM_A], w_in[:, A_Q:]], axis=1).astype(BF16)
            w_out = ev_w_out[i]
            w_out = jnp.concatenate([w_out[:A_Q][_PERM_A], w_out[A_Q:]], axis=0).astype(BF16)
            gains = jnp.concatenate([jnp.tile(ev_q_gain[i], A_HEADS), jnp.tile(ev_k_gain[i], A_KV_HEADS)])[None]
            qa, ka, va, qb, kb, vb = _project(x_lat, m_lat, w_in, gains, cos, sin, gmat, _EVEN_GROUPS, LAT_TM)
            qa_c, ka_c, va_c, qb_c, kb_c, vb_c = _project(x_ctx, m_ctx, w_in, gains, cos_c, sin_c, gmat,
                                                           _EVEN_GROUPS_CTX, nc)
            ya = _global_attention(qa, ka, va, ka_c, va_c)
            yb = _nbr_attention(qb, kb, vb, kb_c, vb_c, _nbr_bias_tiles(ev_rpb[i] * LOG2E))
            x_lat = _out_project(x_lat, ya, yb, 0, w_out, m_lat, g0, b0, LAT_TM)
            if ctx_out:
                ya_c = _ctx_attention(qa_c, ka_c, va_c, no_sink, 0)
                yb_c = _ctx_attention(qb_c, kb_c, vb_c, no_sink, 1)
                x_ctx = _out_project(x_ctx, ya_c, yb_c, 0, w_out, m_ctx, g0, b0, nc)
        else:
            w_in = od_w_in[i]
            w_in = jnp.concatenate([w_in[:, :C_Q][:, _PERM_C], w_in[:, C_Q:]], axis=1).astype(BF16)
            w_out = od_w_out[i][_PERM_C].astype(BF16)
            sink = od_sink[i] * LOG2E
            gains = jnp.ones((1, A_Q + A_KV), F32)
            q, k, v = _project(x_lat, m_lat, w_in, gains, cos, sin, gmat, _ODD_GROUPS, LAT_TM)
            q_c, k_c, v_c = _project(x_ctx, m_ctx, w_in, gains, cos_c, sin_c, gmat, _ODD_GROUPS_CTX, nc)
            y = _window_attention(q, k, v, k_c, v_c, sink)
            x_lat = _out_project(x_lat, y, y, 1, w_out, m_lat, g0, b0, LAT_TM)
            if ctx_out:
                y_c = _ctx_attention(q_c, k_c, v_c, sink, 0)
                x_ctx = _out_project(x_ctx, y_c, y_c, 1, w_out, m_ctx, g0, b0, nc)
        w_up = ffn_w_up[l].astype(BF16)
        w_down = ffn_w_down[l].astype(BF16)
        cb = ffn_conv_b[l][None]
        x_lat = _conv_ffn(x_lat, m_lat, w_up, ffn_conv_w[l], cb, w_down, g1, b1, LAT_TM)
        if ctx_out:
            x_ctx = _conv_ffn(x_ctx, m_ctx, w_up, ffn_conv_w[l], cb, w_down, g1, b1, nc)
    return x_lat[None]
```

```python
import functools

import numpy as np
import jax
import jax.numpy as jnp
from jax import lax
from jax.experimental import pallas as pl
from jax.experimental.pallas import tpu as pltpu

D_MODEL = 1024
DEPTH = 4
GRID_W = 64
HEAD_DIM = 64
A_HEADS = 8
A_KV_HEADS = 2
B_HEADS = 8
NB_KH = 8
NB_KW = 16
C_HEADS = 16
C_KV_HEADS = 2
C_WINDOW = 128
D_FF = 2816
ROPE_THETA = 10000.0
LN_EPS = 1e-5
RMS_EPS = 1e-6
NEG = -1e30
BIG = 1e30
DN_ALPHA = (2 * DEPTH) ** 0.25
A_Q = A_HEADS * HEAD_DIM
A_KV = A_KV_HEADS * HEAD_DIM
B_W = B_HEADS * HEAD_DIM
C_Q = C_HEADS * HEAD_DIM
C_KV = C_KV_HEADS * HEAD_DIM
LOG2E = 1.4426950408889634
QK_SCALE = HEAD_DIM ** -0.5 * LOG2E
SAFE_LOG2 = 64.0

LANES = 128
SUBLANES = 8
HALO = SUBLANES
VMEM_LIMIT = 52 * 1024 * 1024

BF16 = jnp.bfloat16
F32 = jnp.float32

_NT = (((1,), (1,)), ((), ()))


def _params(sem):
    return pltpu.CompilerParams(dimension_semantics=sem, vmem_limit_bytes=VMEM_LIMIT)


def _full(shape):
    return pl.BlockSpec(shape, lambda *_: (0,) * len(shape))


def _lane_lo(shape):
    return lax.broadcasted_iota(jnp.int32, shape, len(shape) - 1) < HEAD_DIM


def _unit_lane(half):
    return HEAD_DIM if half == 0 else 0


def _unit_lane_variants(x):
    lane = jnp.arange(x.shape[-1]) % LANES
    one = jnp.ones((), x.dtype)
    zero = jnp.zeros((), x.dtype)
    lo = jnp.where(lane < HEAD_DIM, x, jnp.where(lane == HEAD_DIM, one, zero))
    hi = jnp.where(lane >= HEAD_DIM, x, jnp.where(lane == 0, one, zero))
    return jnp.stack([lo, hi])


def _running_max(top, s):
    cm = s[:, 0:LANES]
    for j in range(1, s.shape[1] // LANES):
        cm = jnp.maximum(cm, s[:, j * LANES:(j + 1) * LANES])
    return jnp.maximum(top, jnp.max(cm.reshape(s.shape[0] // SUBLANES, SUBLANES, LANES), axis=0))


def _running_min_of_lane(low, x, unit):
    lane = lax.broadcasted_iota(jnp.int32, x.shape, 1)
    picked = jnp.where(lane == unit, x, BIG)
    return jnp.minimum(low, jnp.min(picked.reshape(x.shape[0] // SUBLANES, SUBLANES, LANES), axis=0))


def _outside_safe_range(flags):
    f = flags.reshape(-1, 2, SUBLANES, LANES)
    return jnp.logical_not((jnp.max(f[:, 0]) < SAFE_LOG2) & (jnp.min(f[:, 1]) > 2.0 ** -SAFE_LOG2))


def _mod_kernel(c_ref, w_ref, b_ref, o_ref):
    s = c_ref[...]
    s = s * jax.nn.sigmoid(s)
    o_ref[0] = jnp.dot(s, w_ref[0], preferred_element_type=F32, precision=lax.Precision.HIGHEST) + b_ref[0]


def _modulation(cvec, ada_w, ada_b):
    nb = 6
    return pl.pallas_call(
        _mod_kernel,
        grid=(DEPTH, nb),
        in_specs=[pl.BlockSpec((8, D_MODEL), lambda l, j: (0, 0)),
                  pl.BlockSpec((1, D_MODEL, D_MODEL), lambda l, j: (l, 0, j)),
                  pl.BlockSpec((1, 1, D_MODEL), lambda l, j: (l, 0, j))],
        out_specs=pl.BlockSpec((1, 8, D_MODEL), lambda l, j: (l, 0, j)),
        out_shape=jax.ShapeDtypeStruct((DEPTH, 8, 6 * D_MODEL), F32),
        compiler_params=_params(("arbitrary", "arbitrary")),
        name="modulation",
    )(cvec, ada_w, ada_b.reshape(DEPTH, 1, 6 * D_MODEL))


def _group_sumsq(z, gmat):
    x2 = z * z
    hi = x2.astype(BF16)
    lo = (x2 - hi.astype(F32)).astype(BF16)
    return (jnp.dot(hi, gmat, preferred_element_type=F32) + jnp.dot(lo, gmat, preferred_element_type=F32))


def _proj_kernel(groups, x_ref, mod_ref, w_ref, gain_ref, cos_ref, sin_ref, gmat_ref, *out_refs):
    shift = mod_ref[0:1, :]
    scale = mod_ref[1:2, :]
    h = (x_ref[...] * (1.0 + scale) + shift).astype(BF16)
    cos = cos_ref[...]
    sin = sin_ref[...]
    gmat = gmat_ref[...]
    first = lax.broadcasted_iota(jnp.int32, cos.shape, 1) % 32 < 16
    for (c0, width, g0, rope, qscale), o_ref in zip(groups, out_refs):
        z = jnp.dot(h, w_ref[:, c0:c0 + width], preferred_element_type=F32)
        for b in range(width // LANES):
            zb = z[:, b * LANES:(b + 1) * LANES]
            if g0 is not None:
                ms = _group_sumsq(zb, gmat) * (1.0 / HEAD_DIM)
                zb = zb * lax.rsqrt(ms + RMS_EPS) * gain_ref[:, g0 + b * LANES:g0 + (b + 1) * LANES]
            if rope:
                rot = jnp.where(first, pltpu.roll(zb, LANES - 16, 1), pltpu.roll(zb, 16, 1))
                zb = zb * cos + rot * sin
            if qscale != 1.0:
                zb = zb * qscale
            o_ref[:, b * LANES:(b + 1) * LANES] = zb.astype(BF16)


def _project(x, mod, w, gains, cos, sin, gmat, groups, tm):
    n = x.shape[0]
    win = w.shape[1]
    out_shape = [jax.ShapeDtypeStruct((n, g[1]), BF16) for g in groups]
    out_specs = [pl.BlockSpec((tm, g[1]), lambda i: (i, 0)) for g in groups]
    return pl.pallas_call(
        functools.partial(_proj_kernel, groups),
        grid=(n // tm,),
        in_specs=[pl.BlockSpec((tm, D_MODEL), lambda i: (i, 0)),
                  _full((6, D_MODEL)),
                  _full((D_MODEL, win)),
                  _full(gains.shape),
                  pl.BlockSpec((tm, LANES), lambda i: (i, 0)),
                  pl.BlockSpec((tm, LANES), lambda i: (i, 0)),
                  _full((LANES, LANES))],
        out_specs=out_specs,
        out_shape=out_shape,
        compiler_params=_params(("parallel",)),
        name="qkv_project",
    )(x, mod, w, gains, cos, sin, gmat)


GA_TQ = 512
GA_TK = 2048


GA_VROWS = 80
GA_KEY_CHUNK = 1024
GA_QUERY_CHUNK = 512


def _vt_variants(v):
    t = v.shape[0]
    tail = jnp.concatenate([jnp.ones((1, t), v.dtype), jnp.zeros((GA_VROWS - HEAD_DIM - 1, t), v.dtype)])
    return jnp.stack([jnp.concatenate([v[:, :HEAD_DIM].T, tail]), jnp.concatenate([v[:, HEAD_DIM:].T, tail])])


def _global_attn_kernel(q_ref, kc_ref, vct_ref, k_ref, vt_ref, o_ref, flag_ref, qx_ref, acc_ref):
    kk = pl.program_id(1)
    tq = q_ref.shape[0]
    nblk = q_ref.shape[1] // LANES

    def update(kb_ref, vtb_ref):
        top = flag_ref[0:SUBLANES, :]
        nkeys = kb_ref.shape[0]
        kr = min(GA_KEY_CHUNK, nkeys)
        for j in range(nblk):
            for c0 in range(0, tq, GA_QUERY_CHUNK):
                cs = slice(c0, c0 + GA_QUERY_CHUNK)
                part = [None, None]
                for r0 in range(0, nkeys, kr):
                    st = [lax.dot_general(kb_ref[r0:r0 + kr, :], qx_ref[2 * j + half, cs, :], _NT,
                                          preferred_element_type=F32) for half in range(2)]
                    for half in range(2):
                        top = _running_max(top, st[half])
                        pv = jnp.dot(vtb_ref[half, :, r0:r0 + kr], jnp.exp2(st[half]).astype(BF16),
                                     preferred_element_type=F32)
                        part[half] = pv if part[half] is None else part[half] + pv
                for half in range(2):
                    acc_ref[2 * j + half, :, cs] += part[half]
        flag_ref[0:SUBLANES, :] = top

    @pl.when(kk == 0)
    def _():
        lo = _lane_lo((tq, LANES))
        for j in range(nblk):
            qj = q_ref[:, j * LANES:(j + 1) * LANES]
            zero = jnp.zeros_like(qj)
            qx_ref[2 * j] = jnp.where(lo, qj, zero)
            qx_ref[2 * j + 1] = jnp.where(lo, zero, qj)
        flag_ref[0:SUBLANES, :] = jnp.full((SUBLANES, LANES), NEG, F32)
        acc_ref[...] = jnp.zeros(acc_ref.shape, F32)
        update(kc_ref, vct_ref)

    update(k_ref, vt_ref)

    @pl.when(kk == pl.num_programs(1) - 1)
    def _():
        low = jnp.full((1, LANES), BIG, F32)
        for j in range(nblk):
            halves = []
            for half in range(2):
                a = acc_ref[2 * j + half]
                den = a[HEAD_DIM:HEAD_DIM + 1, :]
                for t in range(tq // LANES):
                    low = jnp.minimum(low, den[:, t * LANES:(t + 1) * LANES])
                halves.append(a[0:HEAD_DIM, :] / den)
            o_ref[:, j * LANES:(j + 1) * LANES] = jnp.concatenate(halves, axis=0).T.astype(o_ref.dtype)
        flag_ref[SUBLANES:2 * SUBLANES, :] = jnp.broadcast_to(low, (SUBLANES, LANES))


def _global_attn_exact_kernel(q_ref, kc_ref, vc_ref, k_ref, v_ref, o_ref, qx_ref, c_ref, acc_ref):
    kk = pl.program_id(1)
    tq = q_ref.shape[0]
    nblk = q_ref.shape[1] // LANES
    nh = 2 * nblk
    lane = lax.broadcasted_iota(jnp.int32, (tq, LANES), 1)

    def moving_reference_update(kx_ref, vx_ref, first):
        for h in range(nh):
            half = h % 2
            unit = _unit_lane(half)
            sp = lax.dot_general(qx_ref[h], kx_ref[half], _NT, preferred_element_type=F32)
            mb = jnp.max(sp, axis=-1, keepdims=True)
            c_old = c_ref[h]
            target = mb if first else jnp.maximum(c_old, c_old + mb)
            qx_ref[h] = jnp.where(lane == unit, -target, qx_ref[h].astype(F32)).astype(BF16)
            c_new = -qx_ref[h][:, unit:unit + 1].astype(F32)
            d = c_new - c_old
            pv = jnp.dot(jnp.exp2(sp - d).astype(BF16), vx_ref[half], preferred_element_type=F32)
            acc_ref[h] = pv if first else acc_ref[h] * jnp.exp2(-d) + pv
            c_ref[h] = c_new

    @pl.when(kk == 0)
    def _():
        lo = _lane_lo((tq, LANES))
        for j in range(nblk):
            qj = q_ref[:, j * LANES:(j + 1) * LANES]
            zero = jnp.zeros_like(qj)
            qx_ref[2 * j] = jnp.where(lo, qj, zero)
            qx_ref[2 * j + 1] = jnp.where(lo, zero, qj)
        c_ref[...] = jnp.zeros(c_ref.shape, F32)
        moving_reference_update(kc_ref, vc_ref, True)

    moving_reference_update(k_ref, v_ref, False)

    @pl.when(kk == pl.num_programs(1) - 1)
    def _():
        lo = _lane_lo((tq, LANES))
        for j in range(nblk):
            a_lo = acc_ref[2 * j]
            a_hi = acc_ref[2 * j + 1]
            o_lo = a_lo / a_lo[:, HEAD_DIM:HEAD_DIM + 1]
            o_hi = a_hi / a_hi[:, 0:1]
            o_ref[:, j * LANES:(j + 1) * LANES] = jnp.where(lo, o_lo, o_hi).astype(o_ref.dtype)


def _global_attention(q, k, v, kc, vc):
    n, qw = q.shape
    nc = kc.shape[0]
    nh = 2 * (qw // LANES)
    nq = n // GA_TQ
    grid = (nq, n // GA_TK)
    q_spec = pl.BlockSpec((GA_TQ, qw), lambda i, j: (i, 0))

    y, flags = pl.pallas_call(
        _global_attn_kernel,
        grid=grid,
        in_specs=[q_spec,
                  _full((nc, LANES)), _full((2, GA_VROWS, nc)),
                  pl.BlockSpec((GA_TK, LANES), lambda i, j: (j, 0)),
                  pl.BlockSpec((2, GA_VROWS, GA_TK), lambda i, j: (0, 0, j))],
        out_specs=[q_spec, pl.BlockSpec((2 * SUBLANES, LANES), lambda i, j: (i, 0))],
        out_shape=[jax.ShapeDtypeStruct((n, qw), BF16),
                   jax.ShapeDtypeStruct((nq * 2 * SUBLANES, LANES), F32)],
        scratch_shapes=[pltpu.VMEM((nh, GA_TQ, LANES), BF16),
                        pltpu.VMEM((nh, GA_VROWS, GA_TQ), F32)],
        compiler_params=_params(("parallel", "arbitrary")),
        name="global_attention",
    )(q, kc, _vt_variants(vc), k, _vt_variants(v))

    def exact():
        return pl.pallas_call(
            _global_attn_exact_kernel,
            grid=grid,
            in_specs=[q_spec,
                      _full((2, nc, LANES)), _full((2, nc, LANES)),
                      pl.BlockSpec((2, GA_TK, LANES), lambda i, j: (0, j, 0)),
                      pl.BlockSpec((2, GA_TK, LANES), lambda i, j: (0, j, 0))],
            out_specs=q_spec,
            out_shape=jax.ShapeDtypeStruct((n, qw), BF16),
            scratch_shapes=[pltpu.VMEM((nh, GA_TQ, LANES), BF16),
                            pltpu.VMEM((nh, GA_TQ, 1), F32),
                            pltpu.VMEM((nh, GA_TQ, LANES), F32)],
            compiler_params=_params(("parallel", "arbitrary")),
            name="global_attention_exact",
        )(q, _unit_lane_variants(kc), _unit_lane_variants(vc), _unit_lane_variants(k), _unit_lane_variants(v))

    return lax.cond(_outside_safe_range(flags), exact, lambda: y)


NB_QROWS = 8
NB_KROWS = NB_QROWS + NB_KH
NB_MASKED = 2 * NB_KH - 1
NB_TILES = 3 * (NB_MASKED + 1)


def _nbr_bias_tiles(rpb):
    nh = rpb.shape[0]
    col = np.arange(GRID_W)
    cs = np.clip(col - NB_KW // 2, 0, GRID_W - NB_KW)
    col_ok = (col[:, None] >= cs[None, :]) & (col[:, None] < cs[None, :] + NB_KW)
    dcol = col[:, None] - col[None, :] + NB_KW - 1
    sel_col = (dcol[:, :, None] == np.arange(2 * NB_KW - 1)) & col_ok[:, :, None]
    base = jnp.einsum('kqj,hdj->hdkq', jnp.asarray(sel_col, F32), rpb, precision=lax.Precision.HIGHEST)
    base = jnp.where(jnp.asarray(col_ok), base, NEG)
    masked = jnp.full((nh, 1, GRID_W, GRID_W), NEG, F32)
    base = jnp.concatenate([base, masked], axis=1)
    prev = jnp.concatenate([masked, base[:, :-1]], axis=1)
    allmasked = jnp.full_like(base, NEG)
    return jnp.concatenate([jnp.concatenate([base, prev], -1),
                            jnp.concatenate([base, allmasked], -1),
                            jnp.concatenate([allmasked, base], -1)], axis=1)


def _nbr_tile_index(rows):
    nq = rows // NB_QROWS
    idx = np.zeros((3, NB_KROWS, NB_QROWS // 2), np.int64)
    for v, blk in enumerate((0, 1, nq - 1)):
        r0 = blk * NB_QROWS
        k0 = int(np.clip(r0 - NB_KH // 2, 0, rows - NB_KROWS))
        for kr in range(NB_KROWS):
            ka = k0 + kr

            def code(qr):
                qa = r0 + qr
                rs = int(np.clip(qa - NB_KH // 2, 0, rows - NB_KH))
                return ka - qa + NB_KH - 1 if rs <= ka < rs + NB_KH else NB_MASKED

            for qp in range(NB_QROWS // 2):
                left, right = code(2 * qp), code(2 * qp + 1)
                if left < NB_MASKED and right < NB_MASKED:
                    assert right == left - 1
                    idx[v, kr, qp] = left
                elif left < NB_MASKED:
                    idx[v, kr, qp] = (NB_MASKED + 1) + left
                else:
                    idx[v, kr, qp] = 2 * (NB_MASKED + 1) + right
    return tuple(tuple(tuple(int(t) for t in row) for row in var) for var in idx)


def _nbr_attn_kernel(exact, tile_index, q_ref, k_ref, vt_ref, kc_ref, vct_ref, tiles_ref, o_ref, flag_ref,
                     bias_ref):
    i = pl.program_id(1)
    nq = pl.num_programs(1)
    rows = k_ref.shape[0] // GRID_W
    span = NB_KROWS * GRID_W

    def fill_bias(variant):
        for half in range(2):
            for kr in range(NB_KROWS):
                for qp in range(NB_QROWS // 2):
                    bias_ref[half, kr * GRID_W:(kr + 1) * GRID_W, qp * LANES:(qp + 1) * LANES] = (
                        tiles_ref[half, tile_index[variant][kr][qp]])

    for variant, at in enumerate((0, 1, nq - 1)):
        pl.when(i == at)(functools.partial(fill_bias, variant))

    r0 = i * NB_QROWS
    ks = pl.multiple_of(jnp.clip(r0 - NB_KH // 2, 0, rows - NB_KROWS) * GRID_W, 2 * LANES)
    kw = k_ref[pl.ds(ks, span), :]
    kc = kc_ref[...]
    q = q_ref[...]
    tq = q.shape[0]
    lo = _lane_lo(q.shape)
    zero = jnp.zeros_like(q)
    top = jnp.full((SUBLANES, LANES), NEG, F32)
    low = jnp.full((1, LANES), BIG, F32)
    halves = []
    logits = []
    for half in range(2):
        qm = jnp.where(lo, q, zero) if half == 0 else jnp.where(lo, zero, q)
        logits.append((lax.dot_general(kw, qm, _NT, preferred_element_type=F32),
                       lax.dot_general(kc, qm, _NT, preferred_element_type=F32)))
    for half in range(2):
        s_nb = logits[half][0] + bias_ref[half]
        s_cx = logits[half][1]
        if exact:
            m = jnp.maximum(jnp.max(s_nb, axis=0, keepdims=True), jnp.max(s_cx, axis=0, keepdims=True))
            s_nb = s_nb - m
            s_cx = s_cx - m
        else:
            top = _running_max(_running_max(top, s_nb), s_cx)
        vtw = jnp.concatenate([vt_ref[0, half, ks // LANES + t] for t in range(span // LANES)], axis=1)
        ot = (jnp.dot(vtw, jnp.exp2(s_nb).astype(BF16), preferred_element_type=F32)
              + jnp.dot(vct_ref[0, half], jnp.exp2(s_cx).astype(BF16), preferred_element_type=F32))
        den = ot[HEAD_DIM:HEAD_DIM + 1, :]
        for t in range(tq // LANES):
            low = jnp.minimum(low, den[:, t * LANES:(t + 1) * LANES])
        halves.append(ot[0:HEAD_DIM, :] / den)
    o_ref[...] = jnp.concatenate(halves, axis=0).T.astype(o_ref.dtype)
    flag_ref[0:SUBLANES, :] = top
    flag_ref[SUBLANES:2 * SUBLANES, :] = jnp.broadcast_to(low, (SUBLANES, LANES))


def _nbr_attention(q, k, v, kc, vc, tiles):
    n, qw = q.shape
    nc = kc.shape[0]
    tq = NB_QROWS * GRID_W
    nq = n // tq
    assert nq >= 3
    npair = qw // LANES
    tile_index = _nbr_tile_index(n // GRID_W)
    vt = jnp.stack([_vt_chunks(v[:, p * LANES:(p + 1) * LANES]) for p in range(npair)])
    vct = jnp.stack([_vt_variants(vc[:, p * LANES:(p + 1) * LANES]) for p in range(npair)])
    operands = (q, k, vt, kc, vct, tiles)

    def run(exact):
        return pl.pallas_call(
            functools.partial(_nbr_attn_kernel, exact, tile_index),
            grid=(npair, nq),
            in_specs=[pl.BlockSpec((tq, LANES), lambda p, i: (i, p)),
                      pl.BlockSpec((n, LANES), lambda p, i: (0, p)),
                      pl.BlockSpec((1, 2, n // LANES, GA_VROWS, LANES), lambda p, i: (p, 0, 0, 0, 0)),
                      pl.BlockSpec((nc, LANES), lambda p, i: (0, p)),
                      pl.BlockSpec((1, 2, GA_VROWS, nc), lambda p, i: (p, 0, 0, 0)),
                      pl.BlockSpec((2, NB_TILES, GRID_W, LANES), lambda p, i: (p, 0, 0, 0))],
            out_specs=[pl.BlockSpec((tq, LANES), lambda p, i: (i, p)),
                       pl.BlockSpec((2 * SUBLANES, LANES), lambda p, i: (p * nq + i, 0))],
            out_shape=[jax.ShapeDtypeStruct((n, qw), BF16),
                       jax.ShapeDtypeStruct((npair * nq * 2 * SUBLANES, LANES), F32)],
            scratch_shapes=[pltpu.VMEM((2, NB_KROWS * GRID_W, tq), F32)],
            compiler_params=_params(("parallel", "arbitrary")),
            name="neighbourhood_attention_exact" if exact else "neighbourhood_attention",
        )(*operands)

    y, flags = run(False)
    return lax.cond(_outside_safe_range(flags), lambda: run(True)[0], lambda: y)


WIN_TQ = 256
WIN_SPAN = WIN_TQ + 2 * C_WINDOW


def _window_attn_kernel(exact, sink_ref, q_ref, k_ref, vt_ref, kc_ref, vct_ref, o_ref, flag_ref):
    i = pl.program_id(0)
    n = k_ref.shape[0]
    tq = q_ref.shape[0]
    nblk = q_ref.shape[1] // LANES
    nchunk = WIN_SPAN // LANES
    q0 = i * tq
    ks = pl.multiple_of(jnp.clip(q0 - C_WINDOW, 0, n - WIN_SPAN), C_WINDOW)
    kw = k_ref[pl.ds(ks, WIN_SPAN), :]
    kc = kc_ref[...]
    kpos = ks + lax.broadcasted_iota(jnp.int32, (WIN_SPAN, tq), 0)
    qpos = q0 + lax.broadcasted_iota(jnp.int32, (WIN_SPAN, tq), 1)
    valid = jnp.abs(kpos - qpos) <= C_WINDOW
    lo = _lane_lo((tq, LANES))
    top = jnp.full((SUBLANES, LANES), NEG, F32)
    low = jnp.full((1, LANES), BIG, F32)
    for j in range(nblk):
        q = q_ref[:, j * LANES:(j + 1) * LANES]
        zero = jnp.zeros_like(q)
        halves = []
        logits = []
        for half in range(2):
            qm = jnp.where(lo, q, zero) if half == 0 else jnp.where(lo, zero, q)
            logits.append((lax.dot_general(kw, qm, _NT, preferred_element_type=F32),
                           lax.dot_general(kc, qm, _NT, preferred_element_type=F32)))
        for half in range(2):
            sink = jnp.full((1, tq), sink_ref[j + nblk * half], F32)
            s_w = jnp.where(valid, logits[half][0], NEG)
            s_c = logits[half][1]
            if exact:
                m = jnp.maximum(jnp.maximum(jnp.max(s_w, axis=0, keepdims=True),
                                            jnp.max(s_c, axis=0, keepdims=True)), sink)
                s_w = s_w - m
                s_c = s_c - m
                sink = sink - m
            else:
                top = jnp.maximum(_running_max(_running_max(top, s_w), s_c), sink_ref[j + nblk * half])
            vtw = jnp.concatenate([vt_ref[half, ks // LANES + t] for t in range(nchunk)], axis=1)
            ot = (jnp.dot(vtw, jnp.exp2(s_w).astype(BF16), preferred_element_type=F32)
                  + jnp.dot(vct_ref[half], jnp.exp2(s_c).astype(BF16), preferred_element_type=F32))
            den = ot[HEAD_DIM:HEAD_DIM + 1, :] + jnp.exp2(sink)
            for t in range(tq // LANES):
                low = jnp.minimum(low, den[:, t * LANES:(t + 1) * LANES])
            halves.append(ot[0:HEAD_DIM, :] / den)
        o_ref[:, j * LANES:(j + 1) * LANES] = jnp.concatenate(halves, axis=0).T.astype(o_ref.dtype)
    flag_ref[0:SUBLANES, :] = top
    flag_ref[SUBLANES:2 * SUBLANES, :] = jnp.broadcast_to(low, (SUBLANES, LANES))


def _vt_chunks(v):
    t = v.shape[0]
    return _vt_variants(v).reshape(2, GA_VROWS, t // LANES, LANES).transpose(0, 2, 1, 3)


def _window_attention(q, k, v, kc, vc, sink):
    n, qw = q.shape
    nc = kc.shape[0]
    nq = n // WIN_TQ
    operands = (sink, q, k, _vt_chunks(v), kc, _vt_variants(vc))

    def run(exact):
        return pl.pallas_call(
            functools.partial(_window_attn_kernel, exact),
            grid=(nq,),
            in_specs=[pl.BlockSpec(memory_space=pltpu.SMEM),
                      pl.BlockSpec((WIN_TQ, qw), lambda i: (i, 0)),
                      _full((n, LANES)), _full((2, n // LANES, GA_VROWS, LANES)),
                      _full((nc, LANES)), _full((2, GA_VROWS, nc))],
            out_specs=[pl.BlockSpec((WIN_TQ, qw), lambda i: (i, 0)),
                       pl.BlockSpec((2 * SUBLANES, LANES), lambda i: (i, 0))],
            out_shape=[jax.ShapeDtypeStruct((n, qw), BF16),
                       jax.ShapeDtypeStruct((nq * 2 * SUBLANES, LANES), F32)],
            compiler_params=_params(("parallel",)),
            name="window_attention_exact" if exact else "window_attention",
        )(*operands)

    y, flags = run(False)
    return lax.cond(_outside_safe_range(flags), lambda: run(True)[0], lambda: y)


def _ctx_attn_kernel(sink_ref, q_ref, k_ref, v_ref, o_ref):
    j = pl.program_id(0)
    nblk = pl.num_programs(0)
    q = q_ref[...]
    k = k_ref[...]
    v = v_ref[...]
    lo = _lane_lo(q.shape)
    zero = jnp.zeros_like(q)
    outs = []
    for half in range(2):
        qm = jnp.where(lo, q, zero) if half == 0 else jnp.where(lo, zero, q)
        sink = sink_ref[j + nblk * half]
        s = lax.dot_general(qm, k, _NT, preferred_element_type=F32)
        m = jnp.maximum(jnp.max(s, axis=-1, keepdims=True), sink)
        p = jnp.exp2(s - m)
        den = jnp.sum(p, axis=-1, keepdims=True) + jnp.exp2(sink - m)
        outs.append(jnp.dot(p.astype(BF16), v, preferred_element_type=F32) / den)
    o_ref[...] = jnp.where(lo, outs[0], outs[1]).astype(o_ref.dtype)


def _ctx_attention(q, k, v, sink, kv_per_block):
    c, qw = q.shape
    nblk = qw // LANES
    return pl.pallas_call(
        _ctx_attn_kernel,
        grid=(nblk,),
        in_specs=[pl.BlockSpec(memory_space=pltpu.SMEM),
                  pl.BlockSpec((c, LANES), lambda j: (0, j)),
                  pl.BlockSpec((c, LANES), lambda j: (0, j * kv_per_block)),
                  pl.BlockSpec((c, LANES), lambda j: (0, j * kv_per_block))],
        out_specs=pl.BlockSpec((c, LANES), lambda j: (0, j)),
        out_shape=jax.ShapeDtypeStruct((c, qw), BF16),
        compiler_params=_params(("parallel",)),
        name="context_attention",
    )(sink, q, k, v)


def _residual_layer_norm(x, f, gate, g, b):
    z = x + (gate * (1.0 / DN_ALPHA)) * f
    mu = jnp.mean(z, axis=-1, keepdims=True)
    zc = z - mu
    var = jnp.mean(zc * zc, axis=-1, keepdims=True)
    return zc * lax.rsqrt(var + LN_EPS / (DN_ALPHA * DN_ALPHA)) * g + b


def _outproj_kernel(x_ref, y1_ref, y2_ref, w1_ref, w2_ref, mod_ref, g_ref, b_ref, o_ref):
    f = (jnp.dot(y1_ref[...], w1_ref[...], preferred_element_type=F32)
         + jnp.dot(y2_ref[...], w2_ref[...], preferred_element_type=F32))
    o_ref[...] = _residual_layer_norm(x_ref[...], f, mod_ref[2:3, :], g_ref[...], b_ref[...])


def _out_project(x, y1, y2, c2, w, mod, g, b, tm):
    n = x.shape[0]
    half = D_MODEL // 2
    return pl.pallas_call(
        _outproj_kernel,
        grid=(n // tm,),
        in_specs=[pl.BlockSpec((tm, D_MODEL), lambda i: (i, 0)),
                  pl.BlockSpec((tm, half), lambda i: (i, 0)),
                  pl.BlockSpec((tm, half), lambda i: (i, c2)),
                  pl.BlockSpec((half, D_MODEL), lambda i: (0, 0)),
                  pl.BlockSpec((half, D_MODEL), lambda i: (1, 0)),
                  _full((6, D_MODEL)), _full((1, D_MODEL)), _full((1, D_MODEL))],
        out_specs=pl.BlockSpec((tm, D_MODEL), lambda i: (i, 0)),
        out_shape=jax.ShapeDtypeStruct((n, D_MODEL), F32),
        compiler_params=_params(("parallel",)),
        name="out_project_ln",
    )(x, y1, y2, w, w, mod, g, b)


FFN_CHUNKS = 2
FFN_FC = D_FF // FFN_CHUNKS


def _ffn_kernel(x_ref, xp_ref, xn_ref, mod_ref, wa_ref, wg_ref, cwa_ref, cwg_ref, cba_ref, cbg_ref, wd_ref,
                g_ref, b_ref, o_ref, h_ref, ua_ref, ug_ref, acc_ref):
    i = pl.program_id(0)
    c = pl.program_id(1)
    tm = x_ref.shape[0]

    @pl.when(c == 0)
    def _():
        shift = mod_ref[3:4, :]
        scale = 1.0 + mod_ref[4:5, :]
        hp = jnp.where(i > 0, xp_ref[...] * scale + shift, 0.0)
        hn = jnp.where(i < pl.num_programs(0) - 1, xn_ref[...] * scale + shift, 0.0)
        hm = x_ref[...] * scale + shift
        h_ref[...] = jnp.concatenate([hp, hm, hn], axis=0).astype(BF16)
        acc_ref[...] = jnp.zeros(acc_ref.shape, F32)

    h = h_ref[...]
    ua_ref[...] = jnp.dot(h, wa_ref[...], preferred_element_type=F32)
    ug_ref[...] = jnp.dot(h, wg_ref[...], preferred_element_type=F32)

    def conv(u_ref, cw_ref, cb_ref):
        return (cb_ref[...]
                + u_ref[HALO - 1:HALO - 1 + tm, :] * cw_ref[0:1, :]
                + u_ref[HALO:HALO + tm, :] * cw_ref[1:2, :]
                + u_ref[HALO + 1:HALO + 1 + tm, :] * cw_ref[2:3, :])

    a = conv(ua_ref, cwa_ref, cba_ref)
    gt = conv(ug_ref, cwg_ref, cbg_ref)
    act = (gt * jax.nn.sigmoid(gt) * a).astype(BF16)
    acc_ref[...] += jnp.dot(act, wd_ref[...], preferred_element_type=F32)

    @pl.when(c == pl.num_programs(1) - 1)
    def _():
        o_ref[...] = _residual_layer_norm(x_ref[...], acc_ref[...], mod_ref[5:6, :], g_ref[...], b_ref[...])


def _conv_ffn(x, mod, w_up, conv_w, conv_b, w_down, g, b, tm):
    n = x.shape[0]
    nrow = n // tm
    nh = n // HALO
    per = tm // HALO
    nc = FFN_CHUNKS
    return pl.pallas_call(
        _ffn_kernel,
        grid=(nrow, nc),
        in_specs=[pl.BlockSpec((tm, D_MODEL), lambda i, c: (i, 0)),
                  pl.BlockSpec((HALO, D_MODEL), lambda i, c: (jnp.maximum(i * per - 1, 0), 0)),
                  pl.BlockSpec((HALO, D_MODEL), lambda i, c: (jnp.minimum((i + 1) * per, nh - 1), 0)),
                  _full((6, D_MODEL)),
                  pl.BlockSpec((D_MODEL, FFN_FC), lambda i, c: (0, c)),
                  pl.BlockSpec((D_MODEL, FFN_FC), lambda i, c: (0, nc + c)),
                  pl.BlockSpec((3, FFN_FC), lambda i, c: (0, c)),
                  pl.BlockSpec((3, FFN_FC), lambda i, c: (0, nc + c)),
                  pl.BlockSpec((1, FFN_FC), lambda i, c: (0, c)),
                  pl.BlockSpec((1, FFN_FC), lambda i, c: (0, nc + c)),
                  pl.BlockSpec((FFN_FC, D_MODEL), lambda i, c: (c, 0)),
                  _full((1, D_MODEL)), _full((1, D_MODEL))],
        out_specs=pl.BlockSpec((tm, D_MODEL), lambda i, c: (i, 0)),
        out_shape=jax.ShapeDtypeStruct((n, D_MODEL), F32),
        scratch_shapes=[pltpu.VMEM((tm + 2 * HALO, D_MODEL), BF16),
                        pltpu.VMEM((tm + 2 * HALO, FFN_FC), F32),
                        pltpu.VMEM((tm + 2 * HALO, FFN_FC), F32),
                        pltpu.VMEM((tm, D_MODEL), F32)],
        compiler_params=_params(("parallel", "arbitrary")),
        name="conv_ffn_ln",
    )(x, x, x, mod, w_up, w_up, conv_w, conv_w, conv_b, conv_b, w_down, g, b)


def _pair_perm(n_heads, kv_heads):
    per = n_heads // kv_heads
    order = []
    for j in range(per):
        for g in range(kv_heads):
            order.append(g * per + j)
    cols = np.concatenate([np.arange(h * HEAD_DIM, (h + 1) * HEAD_DIM) for h in order])
    return cols


_PERM_A = _pair_perm(A_HEADS, A_KV_HEADS)
_PERM_C = _pair_perm(C_HEADS, C_KV_HEADS)


def _rope_tables(n):
    t = np.arange(n)
    row = (t // GRID_W).astype(np.float64)
    col = (t % GRID_W).astype(np.float64)
    half = HEAD_DIM // 2
    inv = ROPE_THETA ** (-np.arange(0, half, 2, dtype=np.float64) / half)
    ang_r = row[:, None] * inv
    ang_c = col[:, None] * inv
    ang = np.concatenate([ang_r, ang_r, ang_c, ang_c], -1)
    ang = np.concatenate([ang, ang], -1)
    sign = np.where(np.arange(LANES) % 32 < 16, -1.0, 1.0)
    return jnp.asarray(np.cos(ang), F32), jnp.asarray(np.sin(ang) * sign, F32)


_EVEN_GROUPS = (
    (0, A_Q, 0, True, QK_SCALE),
    (A_Q, A_KV, A_Q, True, 1.0),
    (A_Q + A_KV, A_KV, None, False, 1.0),
    (A_Q + 2 * A_KV, B_W, None, False, QK_SCALE),
    (A_Q + 2 * A_KV + B_W, B_W, None, False, 1.0),
    (A_Q + 2 * A_KV + 2 * B_W, B_W, None, False, 1.0),
)
_EVEN_GROUPS_CTX = tuple((c0, w, g0, False, s) for (c0, w, g0, _, s) in _EVEN_GROUPS)
_ODD_GROUPS = (
    (0, C_Q, None, True, QK_SCALE),
    (C_Q, C_KV, None, True, 1.0),
    (C_Q + C_KV, C_KV, None, False, 1.0),
)
_ODD_GROUPS_CTX = tuple((c0, w, g0, False, s) for (c0, w, g0, _, s) in _ODD_GROUPS)

LAT_TM = 512


def kernel(x, c, ctx, c_ctx, ada_w, ada_b, ln_g, ln_b, ev_w_in, ev_w_out, ev_q_gain, ev_k_gain, ev_rpb,
           od_w_in, od_w_out, od_sink, ffn_w_up, ffn_conv_w, ffn_conv_b, ffn_w_down):
    n = x.shape[1]
    nc = ctx.shape[1]
    x_lat = x[0]
    x_ctx = ctx[0]

    cvec = jnp.zeros((8, D_MODEL), F32).at[0].set(c[0]).at[1].set(c_ctx)
    mods = _modulation(cvec, ada_w, ada_b)

    cos, sin = _rope_tables(n)
    cos_c, sin_c = cos[:nc], sin[:nc]
    lane = np.arange(LANES)
    gmat = jnp.asarray((lane[:, None] // HEAD_DIM) == (lane[None, :] // HEAD_DIM), BF16)
    no_sink = jnp.full((C_HEADS,), NEG, F32)

    for l in range(DEPTH):
        i = l // 2
        ctx_out = l < DEPTH - 1
        m_lat = mods[l, 0].reshape(6, D_MODEL)
        m_ctx = mods[l, 1].reshape(6, D_MODEL)
        g0, b0 = ln_g[l, 0][None], ln_b[l, 0][None]
        g1, b1 = ln_g[l, 1][None], ln_b[l, 1][None]
        if l % 2 == 0:
            w_in = ev_w_in[i]
            w_in = jnp.concatenate([w_in[:, :A_Q][:, _PERM_A], w_in[:, A_Q:]], axis=1).astype(BF16)
            w_out = ev_w_out[i]
            w_out = jnp.concatenate([w_out[:A_Q][_PERM_A], w_out[A_Q:]], axis=0).astype(BF16)
            gains = jnp.concatenate([jnp.tile(ev_q_gain[i], A_HEADS), jnp.tile(ev_k_gain[i], A_KV_HEADS)])[None]
            qa, ka, va, qb, kb, vb = _project(x_lat, m_lat, w_in, gains, cos, sin, gmat, _EVEN_GROUPS, LAT_TM)
            qa_c, ka_c, va_c, qb_c, kb_c, vb_c = _project(x_ctx, m_ctx, w_in, gains, cos_c, sin_c, gmat,
                                                           _EVEN_GROUPS_CTX, nc)
            ya = _global_attention(qa, ka, va, ka_c, va_c)
            yb = _nbr_attention(qb, kb, vb, kb_c, vb_c, _nbr_bias_tiles(ev_rpb[i] * LOG2E))
            x_lat = _out_project(x_lat, ya, yb, 0, w_out, m_lat, g0, b0, LAT_TM)
            if ctx_out:
                ya_c = _ctx_attention(qa_c, ka_c, va_c, no_sink, 0)
                yb_c = _ctx_attention(qb_c, kb_c, vb_c, no_sink, 1)
                x_ctx = _out_project(x_ctx, ya_c, yb_c, 0, w_out, m_ctx, g0, b0, nc)
        else:
            w_in = od_w_in[i]
            w_in = jnp.concatenate([w_in[:, :C_Q][:, _PERM_C], w_in[:, C_Q:]], axis=1).astype(BF16)
            w_out = od_w_out[i][_PERM_C].astype(BF16)
            sink = od_sink[i] * LOG2E
            gains = jnp.ones((1, A_Q + A_KV), F32)
            q, k, v = _project(x_lat, m_lat, w_in, gains, cos, sin, gmat, _ODD_GROUPS, LAT_TM)
            q_c, k_c, v_c = _project(x_ctx, m_ctx, w_in, gains, cos_c, sin_c, gmat, _ODD_GROUPS_CTX, nc)
            y = _window_attention(q, k, v, k_c, v_c, sink)
            x_lat = _out_project(x_lat, y, y, 1, w_out, m_lat, g0, b0, LAT_TM)
            if ctx_out:
                y_c = _ctx_attention(q_c, k_c, v_c, sink, 0)
                x_ctx = _out_project(x_ctx, y_c, y_c, 1, w_out, m_ctx, g0, b0, nc)
        w_up = ffn_w_up[l].astype(BF16)
        w_down = ffn_w_down[l].astype(BF16)
        cb = ffn_conv_b[l][None]
        x_lat = _conv_ffn(x_lat, m_lat, w_up, ffn_conv_w[l], cb, w_down, g1, b1, LAT_TM)
        if ctx_out:
            x_ctx = _conv_ffn(x_ctx, m_ctx, w_up, ffn_conv_w[l], cb, w_down, g1, b1, nc)
    return x_lat[None]
```

```python
import functools

import numpy as np
import jax
import jax.numpy as jnp
from jax import lax
from jax.experimental import pallas as pl
from jax.experimental.pallas import tpu as pltpu

D_MODEL = 1024
DEPTH = 4
GRID_W = 64
HEAD_DIM = 64
A_HEADS = 8
A_KV_HEADS = 2
B_HEADS = 8
NB_KH = 8
NB_KW = 16
C_HEADS = 16
C_KV_HEADS = 2
C_WINDOW = 128
D_FF = 2816
ROPE_THETA = 10000.0
LN_EPS = 1e-5
RMS_EPS = 1e-6
NEG = -1e30
BIG = 1e30
DN_ALPHA = (2 * DEPTH) ** 0.25
A_Q = A_HEADS * HEAD_DIM
A_KV = A_KV_HEADS * HEAD_DIM
B_W = B_HEADS * HEAD_DIM
C_Q = C_HEADS * HEAD_DIM
C_KV = C_KV_HEADS * HEAD_DIM
LOG2E = 1.4426950408889634
QK_SCALE = HEAD_DIM ** -0.5 * LOG2E
SAFE_LOG2 = 64.0

LANES = 128
SUBLANES = 8
HALO = SUBLANES
VMEM_LIMIT = 52 * 1024 * 1024

BF16 = jnp.bfloat16
F32 = jnp.float32

_NT = (((1,), (1,)), ((), ()))


def _params(sem):
    return pltpu.CompilerParams(dimension_semantics=sem, vmem_limit_bytes=VMEM_LIMIT)


def _full(shape):
    return pl.BlockSpec(shape, lambda *_: (0,) * len(shape))


def _lane_lo(shape):
    return lax.broadcasted_iota(jnp.int32, shape, len(shape) - 1) < HEAD_DIM


def _unit_lane(half):
    return HEAD_DIM if half == 0 else 0


def _unit_lane_variants(x):
    lane = jnp.arange(x.shape[-1]) % LANES
    one = jnp.ones((), x.dtype)
    zero = jnp.zeros((), x.dtype)
    lo = jnp.where(lane < HEAD_DIM, x, jnp.where(lane == HEAD_DIM, one, zero))
    hi = jnp.where(lane >= HEAD_DIM, x, jnp.where(lane == 0, one, zero))
    return jnp.stack([lo, hi])


def _running_max(top, s):
    cm = s[:, 0:LANES]
    for j in range(1, s.shape[1] // LANES):
        cm = jnp.maximum(cm, s[:, j * LANES:(j + 1) * LANES])
    return jnp.maximum(top, jnp.max(cm.reshape(s.shape[0] // SUBLANES, SUBLANES, LANES), axis=0))


def _running_min_of_lane(low, x, unit):
    lane = lax.broadcasted_iota(jnp.int32, x.shape, 1)
    picked = jnp.where(lane == unit, x, BIG)
    return jnp.minimum(low, jnp.min(picked.reshape(x.shape[0] // SUBLANES, SUBLANES, LANES), axis=0))


def _outside_safe_range(flags):
    f = flags.reshape(-1, 2, SUBLANES, LANES)
    return jnp.logical_not((jnp.max(f[:, 0]) < SAFE_LOG2) & (jnp.min(f[:, 1]) > 2.0 ** -SAFE_LOG2))


def _mod_kernel(c_ref, w_ref, b_ref, o_ref):
    s = c_ref[...]
    s = s * jax.nn.sigmoid(s)
    o_ref[0] = jnp.dot(s, w_ref[0], preferred_element_type=F32, precision=lax.Precision.HIGHEST) + b_ref[0]


def _modulation(cvec, ada_w, ada_b):
    nb = 6
    return pl.pallas_call(
        _mod_kernel,
        grid=(DEPTH, nb),
        in_specs=[pl.BlockSpec((8, D_MODEL), lambda l, j: (0, 0)),
                  pl.BlockSpec((1, D_MODEL, D_MODEL), lambda l, j: (l, 0, j)),
                  pl.BlockSpec((1, 1, D_MODEL), lambda l, j: (l, 0, j))],
        out_specs=pl.BlockSpec((1, 8, D_MODEL), lambda l, j: (l, 0, j)),
        out_shape=jax.ShapeDtypeStruct((DEPTH, 8, 6 * D_MODEL), F32),
        compiler_params=_params(("arbitrary", "arbitrary")),
        name="modulation",
    )(cvec, ada_w, ada_b.reshape(DEPTH, 1, 6 * D_MODEL))


def _group_sumsq(z, gmat):
    x2 = z * z
    hi = x2.astype(BF16)
    lo = (x2 - hi.astype(F32)).astype(BF16)
    return (jnp.dot(hi, gmat, preferred_element_type=F32) + jnp.dot(lo, gmat, preferred_element_type=F32))


def _proj_kernel(groups, x_ref, mod_ref, w_ref, gain_ref, cos_ref, sin_ref, gmat_ref, *out_refs):
    shift = mod_ref[0:1, :]
    scale = mod_ref[1:2, :]
    h = (x_ref[...] * (1.0 + scale) + shift).astype(BF16)
    cos = cos_ref[...]
    sin = sin_ref[...]
    gmat = gmat_ref[...]
    first = lax.broadcasted_iota(jnp.int32, cos.shape, 1) % 32 < 16
    for (c0, width, g0, rope, qscale), o_ref in zip(groups, out_refs):
        z = jnp.dot(h, w_ref[:, c0:c0 + width], preferred_element_type=F32)
        for b in range(width // LANES):
            zb = z[:, b * LANES:(b + 1) * LANES]
            if g0 is not None:
                ms = _group_sumsq(zb, gmat) * (1.0 / HEAD_DIM)
                zb = zb * lax.rsqrt(ms + RMS_EPS) * gain_ref[:, g0 + b * LANES:g0 + (b + 1) * LANES]
            if rope:
                rot = jnp.where(first, pltpu.roll(zb, LANES - 16, 1), pltpu.roll(zb, 16, 1))
                zb = zb * cos + rot * sin
            if qscale != 1.0:
                zb = zb * qscale
            o_ref[:, b * LANES:(b + 1) * LANES] = zb.astype(BF16)


def _project(x, mod, w, gains, cos, sin, gmat, groups, tm):
    n = x.shape[0]
    win = w.shape[1]
    out_shape = [jax.ShapeDtypeStruct((n, g[1]), BF16) for g in groups]
    out_specs = [pl.BlockSpec((tm, g[1]), lambda i: (i, 0)) for g in groups]
    return pl.pallas_call(
        functools.partial(_proj_kernel, groups),
        grid=(n // tm,),
        in_specs=[pl.BlockSpec((tm, D_MODEL), lambda i: (i, 0)),
                  _full((6, D_MODEL)),
                  _full((D_MODEL, win)),
                  _full(gains.shape),
                  pl.BlockSpec((tm, LANES), lambda i: (i, 0)),
                  pl.BlockSpec((tm, LANES), lambda i: (i, 0)),
                  _full((LANES, LANES))],
        out_specs=out_specs,
        out_shape=out_shape,
        compiler_params=_params(("parallel",)),
        name="qkv_project",
    )(x, mod, w, gains, cos, sin, gmat)


GA_TQ = 512
GA_TK = 2048


GA_VROWS = 80
GA_KEY_CHUNK = 1024
GA_QUERY_CHUNK = 512


def _vt_variants(v):
    t = v.shape[0]
    tail = jnp.concatenate([jnp.ones((1, t), v.dtype), jnp.zeros((GA_VROWS - HEAD_DIM - 1, t), v.dtype)])
    return jnp.stack([jnp.concatenate([v[:, :HEAD_DIM].T, tail]), jnp.concatenate([v[:, HEAD_DIM:].T, tail])])


def _global_attn_kernel(q_ref, kc_ref, vct_ref, k_ref, vt_ref, o_ref, flag_ref, qx_ref, acc_ref):
    kk = pl.program_id(1)
    tq = q_ref.shape[0]
    nblk = q_ref.shape[1] // LANES

    def update(kb_ref, vtb_ref):
        top = flag_ref[0:SUBLANES, :]
        nkeys = kb_ref.shape[0]
        kr = min(GA_KEY_CHUNK, nkeys)
        for j in range(nblk):
            for c0 in range(0, tq, GA_QUERY_CHUNK):
                cs = slice(c0, c0 + GA_QUERY_CHUNK)
                part = [None, None]
                for r0 in range(0, nkeys, kr):
                    st = [lax.dot_general(kb_ref[r0:r0 + kr, :], qx_ref[2 * j + half, cs, :], _NT,
                                          preferred_element_type=F32) for half in range(2)]
                    for half in range(2):
                        top = _running_max(top, st[half])
                        pv = jnp.dot(vtb_ref[half, :, r0:r0 + kr], jnp.exp2(st[half]).astype(BF16),
                                     preferred_element_type=F32)
                        part[half] = pv if part[half] is None else part[half] + pv
                for half in range(2):
                    acc_ref[2 * j + half, :, cs] += part[half]
        flag_ref[0:SUBLANES, :] = top

    @pl.when(kk == 0)
    def _():
        lo = _lane_lo((tq, LANES))
        for j in range(nblk):
            qj = q_ref[:, j * LANES:(j + 1) * LANES]
            zero = jnp.zeros_like(qj)
            qx_ref[2 * j] = jnp.where(lo, qj, zero)
            qx_ref[2 * j + 1] = jnp.where(lo, zero, qj)
        flag_ref[0:SUBLANES, :] = jnp.full((SUBLANES, LANES), NEG, F32)
        acc_ref[...] = jnp.zeros(acc_ref.shape, F32)
        update(kc_ref, vct_ref)

    update(k_ref, vt_ref)

    @pl.when(kk == pl.num_programs(1) - 1)
    def _():
        low = jnp.full((1, LANES), BIG, F32)
        for j in range(nblk):
            halves = []
            for half in range(2):
                a = acc_ref[2 * j + half]
                den = a[HEAD_DIM:HEAD_DIM + 1, :]
                for t in range(tq // LANES):
                    low = jnp.minimum(low, den[:, t * LANES:(t + 1) * LANES])
                halves.append(a[0:HEAD_DIM, :] / den)
            o_ref[:, j * LANES:(j + 1) * LANES] = jnp.concatenate(halves, axis=0).T.astype(o_ref.dtype)
        flag_ref[SUBLANES:2 * SUBLANES, :] = jnp.broadcast_to(low, (SUBLANES, LANES))


def _global_attn_exact_kernel(q_ref, kc_ref, vc_ref, k_ref, v_ref, o_ref, qx_ref, c_ref, acc_ref):
    kk = pl.program_id(1)
    tq = q_ref.shape[0]
    nblk = q_ref.shape[1] // LANES
    nh = 2 * nblk
    lane = lax.broadcasted_iota(jnp.int32, (tq, LANES), 1)

    def moving_reference_update(kx_ref, vx_ref, first):
        for h in range(nh):
            half = h % 2
            unit = _unit_lane(half)
            sp = lax.dot_general(qx_ref[h], kx_ref[half], _NT, preferred_element_type=F32)
            mb = jnp.max(sp, axis=-1, keepdims=True)
            c_old = c_ref[h]
            target = mb if first else jnp.maximum(c_old, c_old + mb)
            qx_ref[h] = jnp.where(lane == unit, -target, qx_ref[h].astype(F32)).astype(BF16)
            c_new = -qx_ref[h][:, unit:unit + 1].astype(F32)
            d = c_new - c_old
            pv = jnp.dot(jnp.exp2(sp - d).astype(BF16), vx_ref[half], preferred_element_type=F32)
            acc_ref[h] = pv if first else acc_ref[h] * jnp.exp2(-d) + pv
            c_ref[h] = c_new

    @pl.when(kk == 0)
    def _():
        lo = _lane_lo((tq, LANES))
        for j in range(nblk):
            qj = q_ref[:, j * LANES:(j + 1) * LANES]
            zero = jnp.zeros_like(qj)
            qx_ref[2 * j] = jnp.where(lo, qj, zero)
            qx_ref[2 * j + 1] = jnp.where(lo, zero, qj)
        c_ref[...] = jnp.zeros(c_ref.shape, F32)
        moving_reference_update(kc_ref, vc_ref, True)

    moving_reference_update(k_ref, v_ref, False)

    @pl.when(kk == pl.num_programs(1) - 1)
    def _():
        lo = _lane_lo((tq, LANES))
        for j in range(nblk):
            a_lo = acc_ref[2 * j]
            a_hi = acc_ref[2 * j + 1]
            o_lo = a_lo / a_lo[:, HEAD_DIM:HEAD_DIM + 1]
            o_hi = a_hi / a_hi[:, 0:1]
            o_ref[:, j * LANES:(j + 1) * LANES] = jnp.where(lo, o_lo, o_hi).astype(o_ref.dtype)


def _global_attention(q, k, v, kc, vc):
    n, qw = q.shape
    nc = kc.shape[0]
    nh = 2 * (qw // LANES)
    nq = n // GA_TQ
    grid = (nq, n // GA_TK)
    q_spec = pl.BlockSpec((GA_TQ, qw), lambda i, j: (i, 0))

    y, flags = pl.pallas_call(
        _global_attn_kernel,
        grid=grid,
        in_specs=[q_spec,
                  _full((nc, LANES)), _full((2, GA_VROWS, nc)),
                  pl.BlockSpec((GA_TK, LANES), lambda i, j: (j, 0)),
                  pl.BlockSpec((2, GA_VROWS, GA_TK), lambda i, j: (0, 0, j))],
        out_specs=[q_spec, pl.BlockSpec((2 * SUBLANES, LANES), lambda i, j: (i, 0))],
        out_shape=[jax.ShapeDtypeStruct((n, qw), BF16),
                   jax.ShapeDtypeStruct((nq * 2 * SUBLANES, LANES), F32)],
        scratch_shapes=[pltpu.VMEM((nh, GA_TQ, LANES), BF16),
                        pltpu.VMEM((nh, GA_VROWS, GA_TQ), F32)],
        compiler_params=_params(("parallel", "arbitrary")),
        name="global_attention",
    )(q, kc, _vt_variants(vc), k, _vt_variants(v))

    def exact():
        return pl.pallas_call(
            _global_attn_exact_kernel,
            grid=grid,
            in_specs=[q_spec,
                      _full((2, nc, LANES)), _full((2, nc, LANES)),
                      pl.BlockSpec((2, GA_TK, LANES), lambda i, j: (0, j, 0)),
                      pl.BlockSpec((2, GA_TK, LANES), lambda i, j: (0, j, 0))],
            out_specs=q_spec,
            out_shape=jax.ShapeDtypeStruct((n, qw), BF16),
            scratch_shapes=[pltpu.VMEM((nh, GA_TQ, LANES), BF16),
                            pltpu.VMEM((nh, GA_TQ, 1), F32),
                            pltpu.VMEM((nh, GA_TQ, LANES), F32)],
            compiler_params=_params(("parallel", "arbitrary")),
            name="global_attention_exact",
        )(q, _unit_lane_variants(kc), _unit_lane_variants(vc), _unit_lane_variants(k), _unit_lane_variants(v))

    return lax.cond(_outside_safe_range(flags), exact, lambda: y)


NB_QROWS = 8
NB_KROWS = NB_QROWS + NB_KH
NB_MASKED = 2 * NB_KH - 1
NB_TILES = 3 * (NB_MASKED + 1)


def _nbr_bias_tiles(rpb):
    nh = rpb.shape[0]
    col = np.arange(GRID_W)
    cs = np.clip(col - NB_KW // 2, 0, GRID_W - NB_KW)
    col_ok = (col[:, None] >= cs[None, :]) & (col[:, None] < cs[None, :] + NB_KW)
    dcol = col[:, None] - col[None, :] + NB_KW - 1
    sel_col = (dcol[:, :, None] == np.arange(2 * NB_KW - 1)) & col_ok[:, :, None]
    base = jnp.einsum('kqj,hdj->hdkq', jnp.asarray(sel_col, F32), rpb, precision=lax.Precision.HIGHEST)
    base = jnp.where(jnp.asarray(col_ok), base, NEG)
    masked = jnp.full((nh, 1, GRID_W, GRID_W), NEG, F32)
    base = jnp.concatenate([base, masked], axis=1)
    prev = jnp.concatenate([masked, base[:, :-1]], axis=1)
    allmasked = jnp.full_like(base, NEG)
    return jnp.concatenate([jnp.concatenate([base, prev], -1),
                            jnp.concatenate([base, allmasked], -1),
                            jnp.concatenate([allmasked, base], -1)], axis=1)


def _nbr_tile_index(rows):
    nq = rows // NB_QROWS
    idx = np.zeros((3, NB_KROWS, NB_QROWS // 2), np.int64)
    for v, blk in enumerate((0, 1, nq - 1)):
        r0 = blk * NB_QROWS
        k0 = int(np.clip(r0 - NB_KH // 2, 0, rows - NB_KROWS))
        for kr in range(NB_KROWS):
            ka = k0 + kr

            def code(qr):
                qa = r0 + qr
                rs = int(np.clip(qa - NB_KH // 2, 0, rows - NB_KH))
                return ka - qa + NB_KH - 1 if rs <= ka < rs + NB_KH else NB_MASKED

            for qp in range(NB_QROWS // 2):
                left, right = code(2 * qp), code(2 * qp + 1)
                if left < NB_MASKED and right < NB_MASKED:
                    assert right == left - 1
                    idx[v, kr, qp] = left
                elif left < NB_MASKED:
                    idx[v, kr, qp] = (NB_MASKED + 1) + left
                else:
                    idx[v, kr, qp] = 2 * (NB_MASKED + 1) + right
    return tuple(tuple(tuple(int(t) for t in row) for row in var) for var in idx)


def _nbr_attn_kernel(exact, tile_index, q_ref, k_ref, vt_ref, kc_ref, vct_ref, tiles_ref, o_ref, flag_ref,
                     bias_ref):
    i = pl.program_id(1)
    nq = pl.num_programs(1)
    rows = k_ref.shape[0] // GRID_W
    span = NB_KROWS * GRID_W

    def fill_bias(variant):
        for half in range(2):
            for kr in range(NB_KROWS):
                for qp in range(NB_QROWS // 2):
                    bias_ref[half, kr * GRID_W:(kr + 1) * GRID_W, qp * LANES:(qp + 1) * LANES] = (
                        tiles_ref[half, tile_index[variant][kr][qp]])

    for variant, at in enumerate((0, 1, nq - 1)):
        pl.when(i == at)(functools.partial(fill_bias, variant))

    r0 = i * NB_QROWS
    ks = pl.multiple_of(jnp.clip(r0 - NB_KH // 2, 0, rows - NB_KROWS) * GRID_W, 2 * LANES)
    kw = k_ref[pl.ds(ks, span), :]
    kc = kc_ref[...]
    q = q_ref[...]
    tq = q.shape[0]
    lo = _lane_lo(q.shape)
    zero = jnp.zeros_like(q)
    top = jnp.full((SUBLANES, LANES), NEG, F32)
    low = jnp.full((1, LANES), BIG, F32)
    halves = []
    logits = []
    for half in range(2):
        qm = jnp.where(lo, q, zero) if half == 0 else jnp.where(lo, zero, q)
        logits.append((lax.dot_general(kw, qm, _NT, preferred_element_type=F32),
                       lax.dot_general(kc, qm, _NT, preferred_element_type=F32)))
    for half in range(2):
        s_nb = logits[half][0] + bias_ref[half]
        s_cx = logits[half][1]
        if exact:
            m = jnp.maximum(jnp.max(s_nb, axis=0, keepdims=True), jnp.max(s_cx, axis=0, keepdims=True))
            s_nb = s_nb - m
            s_cx = s_cx - m
        else:
            top = _running_max(_running_max(top, s_nb), s_cx)
        vtw = jnp.concatenate([vt_ref[0, half, ks // LANES + t] for t in range(span // LANES)], axis=1)
        ot = (jnp.dot(vtw, jnp.exp2(s_nb).astype(BF16), preferred_element_type=F32)
              + jnp.dot(vct_ref[0, half], jnp.exp2(s_cx).astype(BF16), preferred_element_type=F32))
        den = ot[HEAD_DIM:HEAD_DIM + 1, :]
        for t in range(tq // LANES):
            low = jnp.minimum(low, den[:, t * LANES:(t + 1) * LANES])
        halves.append(ot[0:HEAD_DIM, :] / den)
    o_ref[...] = jnp.concatenate(halves, axis=0).T.astype(o_ref.dtype)
    flag_ref[0:SUBLANES, :] = top
    flag_ref[SUBLANES:2 * SUBLANES, :] = jnp.broadcast_to(low, (SUBLANES, LANES))


def _nbr_attention(q, k, v, kc, vc, tiles):
    n, qw = q.shape
    nc = kc.shape[0]
    tq = NB_QROWS * GRID_W
    nq = n // tq
    assert nq >= 3
    npair = qw // LANES
    tile_index = _nbr_tile_index(n // GRID_W)
    vt = jnp.stack([_vt_chunks(v[:, p * LANES:(p + 1) * LANES]) for p in range(npair)])
    vct = jnp.stack([_vt_variants(vc[:, p * LANES:(p + 1) * LANES]) for p in range(npair)])
    operands = (q, k, vt, kc, vct, tiles)

    def run(exact):
        return pl.pallas_call(
            functools.partial(_nbr_attn_kernel, exact, tile_index),
            grid=(npair, nq),
            in_specs=[pl.BlockSpec((tq, LANES), lambda p, i: (i, p)),
                      pl.BlockSpec((n, LANES), lambda p, i: (0, p)),
                      pl.BlockSpec((1, 2, n // LANES, GA_VROWS, LANES), lambda p, i: (p, 0, 0, 0, 0)),
                      pl.BlockSpec((nc, LANES), lambda p, i: (0, p)),
                      pl.BlockSpec((1, 2, GA_VROWS, nc), lambda p, i: (p, 0, 0, 0)),
                      pl.BlockSpec((2, NB_TILES, GRID_W, LANES), lambda p, i: (p, 0, 0, 0))],
            out_specs=[pl.BlockSpec((tq, LANES), lambda p, i: (i, p)),
                       pl.BlockSpec((2 * SUBLANES, LANES), lambda p, i: (p * nq + i, 0))],
            out_shape=[jax.ShapeDtypeStruct((n, qw), BF16),
                       jax.ShapeDtypeStruct((npair * nq * 2 * SUBLANES, LANES), F32)],
            scratch_shapes=[pltpu.VMEM((2, NB_KROWS * GRID_W, tq), F32)],
            compiler_params=_params(("parallel", "arbitrary")),
            name="neighbourhood_attention_exact" if exact else "neighbourhood_attention",
        )(*operands)

    y, flags = run(False)
    return lax.cond(_outside_safe_range(flags), lambda: run(True)[0], lambda: y)


WIN_TQ = 256
WIN_SPAN = WIN_TQ + 2 * C_WINDOW


def _window_attn_kernel(exact, sink_ref, q_ref, k_ref, vt_ref, kc_ref, vct_ref, o_ref, flag_ref):
    i = pl.program_id(0)
    n = k_ref.shape[0]
    tq = q_ref.shape[0]
    nblk = q_ref.shape[1] // LANES
    nchunk = WIN_SPAN // LANES
    q0 = i * tq
    ks = pl.multiple_of(jnp.clip(q0 - C_WINDOW, 0, n - WIN_SPAN), C_WINDOW)
    kw = k_ref[pl.ds(ks, WIN_SPAN), :]
    kc = kc_ref[...]
    kpos = ks + lax.broadcasted_iota(jnp.int32, (WIN_SPAN, tq), 0)
    qpos = q0 + lax.broadcasted_iota(jnp.int32, (WIN_SPAN, tq), 1)
    valid = jnp.abs(kpos - qpos) <= C_WINDOW
    lo = _lane_lo((tq, LANES))
    top = jnp.full((SUBLANES, LANES), NEG, F32)
    low = jnp.full((1, LANES), BIG, F32)
    for j in range(nblk):
        q = q_ref[:, j * LANES:(j + 1) * LANES]
        zero = jnp.zeros_like(q)
        halves = []
        logits = []
        for half in range(2):
            qm = jnp.where(lo, q, zero) if half == 0 else jnp.where(lo, zero, q)
            logits.append((lax.dot_general(kw, qm, _NT, preferred_element_type=F32),
                           lax.dot_general(kc, qm, _NT, preferred_element_type=F32)))
        for half in range(2):
            sink = jnp.full((1, tq), sink_ref[j + nblk * half], F32)
            s_w = jnp.where(valid, logits[half][0], NEG)
            s_c = logits[half][1]
            if exact:
                m = jnp.maximum(jnp.maximum(jnp.max(s_w, axis=0, keepdims=True),
                                            jnp.max(s_c, axis=0, keepdims=True)), sink)
                s_w = s_w - m
                s_c = s_c - m
                sink = sink - m
            else:
                top = jnp.maximum(_running_max(_running_max(top, s_w), s_c), sink_ref[j + nblk * half])
            vtw = jnp.concatenate([vt_ref[half, ks // LANES + t] for t in range(nchunk)], axis=1)
            ot = (jnp.dot(vtw, jnp.exp2(s_w).astype(BF16), preferred_element_type=F32)
                  + jnp.dot(vct_ref[half], jnp.exp2(s_c).astype(BF16), preferred_element_type=F32))
            den = ot[HEAD_DIM:HEAD_DIM + 1, :] + jnp.exp2(sink)
            for t in range(tq // LANES):
                low = jnp.minimum(low, den[:, t * LANES:(t + 1) * LANES])
            halves.append(ot[0:HEAD_DIM, :] / den)
        o_ref[:, j * LANES:(j + 1) * LANES] = jnp.concatenate(halves, axis=0).T.astype(o_ref.dtype)
    flag_ref[0:SUBLANES, :] = top
    flag_ref[SUBLANES:2 * SUBLANES, :] = jnp.broadcast_to(low, (SUBLANES, LANES))


def _vt_chunks(v):
    t = v.shape[0]
    return _vt_variants(v).reshape(2, GA_VROWS, t // LANES, LANES).transpose(0, 2, 1, 3)


def _window_attention(q, k, v, kc, vc, sink):
    n, qw = q.shape
    nc = kc.shape[0]
    nq = n // WIN_TQ
    operands = (sink, q, k, _vt_chunks(v), kc, _vt_variants(vc))

    def run(exact):
        return pl.pallas_call(
            functools.partial(_window_attn_kernel, exact),
            grid=(nq,),
            in_specs=[pl.BlockSpec(memory_space=pltpu.SMEM),
                      pl.BlockSpec((WIN_TQ, qw), lambda i: (i, 0)),
                      _full((n, LANES)), _full((2, n // LANES, GA_VROWS, LANES)),
                      _full((nc, LANES)), _full((2, GA_VROWS, nc))],
            out_specs=[pl.BlockSpec((WIN_TQ, qw), lambda i: (i, 0)),
                       pl.BlockSpec((2 * SUBLANES, LANES), lambda i: (i, 0))],
            out_shape=[jax.ShapeDtypeStruct((n, qw), BF16),
                       jax.ShapeDtypeStruct((nq * 2 * SUBLANES, LANES), F32)],
            compiler_params=_params(("parallel",)),
            name="window_attention_exact" if exact else "window_attention",
        )(*operands)

    y, flags = run(False)
    return lax.cond(_outside_safe_range(flags), lambda: run(True)[0], lambda: y)


def _ctx_attn_kernel(sink_ref, q_ref, k_ref, v_ref, o_ref):
    j = pl.program_id(0)
    nblk = pl.num_programs(0)
    q = q_ref[...]
    k = k_ref[...]
    v = v_ref[...]
    lo = _lane_lo(q.shape)
    zero = jnp.zeros_like(q)
    outs = []
    for half in range(2):
        qm = jnp.where(lo, q, zero) if half == 0 else jnp.where(lo, zero, q)
        sink = sink_ref[j + nblk * half]
        s = lax.dot_general(qm, k, _NT, preferred_element_type=F32)
        m = jnp.maximum(jnp.max(s, axis=-1, keepdims=True), sink)
        p = jnp.exp2(s - m)
        den = jnp.sum(p, axis=-1, keepdims=True) + jnp.exp2(sink - m)
        outs.append(jnp.dot(p.astype(BF16), v, preferred_element_type=F32) / den)
    o_ref[...] = jnp.where(lo, outs[0], outs[1]).astype(o_ref.dtype)


def _ctx_attention(q, k, v, sink, kv_per_block):
    c, qw = q.shape
    nblk = qw // LANES
    return pl.pallas_call(
        _ctx_attn_kernel,
        grid=(nblk,),
        in_specs=[pl.BlockSpec(memory_space=pltpu.SMEM),
                  pl.BlockSpec((c, LANES), lambda j: (0, j)),
                  pl.BlockSpec((c, LANES), lambda j: (0, j * kv_per_block)),
                  pl.BlockSpec((c, LANES), lambda j: (0, j * kv_per_block))],
        out_specs=pl.BlockSpec((c, LANES), lambda j: (0, j)),
        out_shape=jax.ShapeDtypeStruct((c, qw), BF16),
        compiler_params=_params(("parallel",)),
        name="context_attention",
    )(sink, q, k, v)


def _residual_layer_norm(x, f, gate, g, b):
    z = x + (gate * (1.0 / DN_ALPHA)) * f
    mu = jnp.mean(z, axis=-1, keepdims=True)
    zc = z - mu
    var = jnp.mean(zc * zc, axis=-1, keepdims=True)
    return zc * lax.rsqrt(var + LN_EPS / (DN_ALPHA * DN_ALPHA)) * g + b


def _outproj_kernel(x_ref, y1_ref, y2_ref, w1_ref, w2_ref, mod_ref, g_ref, b_ref, o_ref):
    rows = x_ref.shape[0] // 2
    for s in range(2):
        rs = slice(s * rows, (s + 1) * rows)
        f = (jnp.dot(y1_ref[rs, :], w1_ref[...], preferred_element_type=F32)
             + jnp.dot(y2_ref[rs, :], w2_ref[...], preferred_element_type=F32))
        o_ref[rs, :] = _residual_layer_norm(x_ref[rs, :], f, mod_ref[2:3, :], g_ref[...], b_ref[...])


def _out_project(x, y1, y2, c2, w, mod, g, b, tm):
    n = x.shape[0]
    half = D_MODEL // 2
    return pl.pallas_call(
        _outproj_kernel,
        grid=(n // tm,),
        in_specs=[pl.BlockSpec((tm, D_MODEL), lambda i: (i, 0)),
                  pl.BlockSpec((tm, half), lambda i: (i, 0)),
                  pl.BlockSpec((tm, half), lambda i: (i, c2)),
                  pl.BlockSpec((half, D_MODEL), lambda i: (0, 0)),
                  pl.BlockSpec((half, D_MODEL), lambda i: (1, 0)),
                  _full((6, D_MODEL)), _full((1, D_MODEL)), _full((1, D_MODEL))],
        out_specs=pl.BlockSpec((tm, D_MODEL), lambda i: (i, 0)),
        out_shape=jax.ShapeDtypeStruct((n, D_MODEL), F32),
        compiler_params=_params(("parallel",)),
        name="out_project_ln",
    )(x, y1, y2, w, w, mod, g, b)


FFN_CHUNKS = 2
FFN_FC = D_FF // FFN_CHUNKS


FFN_CHAINS = 1


def _ffn_kernel(x_ref, xp_ref, xn_ref, mod_ref, wu_ref, cw_ref, cb_ref, wd_ref, g_ref, b_ref, o_ref,
                h_ref, ua_ref, ug_ref):
    i = pl.program_id(0)
    tm = x_ref.shape[0]
    rows = tm // FFN_CHAINS
    shift = mod_ref[3:4, :]
    scale = 1.0 + mod_ref[4:5, :]
    hp = jnp.where(i > 0, xp_ref[...] * scale + shift, 0.0)
    hn = jnp.where(i < pl.num_programs(0) - 1, xn_ref[...] * scale + shift, 0.0)
    hm = x_ref[...] * scale + shift
    h_ref[...] = jnp.concatenate([hp, hm, hn], axis=0).astype(BF16)

    def conv(u_ref, col0):
        cols = slice(col0, col0 + FFN_FC)
        return (cb_ref[:, cols]
                + u_ref[HALO - 1:HALO - 1 + rows, :] * cw_ref[0:1, cols]
                + u_ref[HALO:HALO + rows, :] * cw_ref[1:2, cols]
                + u_ref[HALO + 1:HALO + 1 + rows, :] * cw_ref[2:3, cols])

    for s in range(FFN_CHAINS):
        r0 = s * rows
        hs = h_ref[r0:r0 + rows + 2 * HALO, :]
        f = None
        for c in range(FFN_CHUNKS):
            a0 = c * FFN_FC
            g0 = D_FF + c * FFN_FC
            ua = ua_ref.at[s, c]
            ug = ug_ref.at[s, c]
            ua[...] = jnp.dot(hs, wu_ref[:, a0:a0 + FFN_FC], preferred_element_type=F32)
            ug[...] = jnp.dot(hs, wu_ref[:, g0:g0 + FFN_FC], preferred_element_type=F32)
            gt = conv(ug, g0)
            act = (gt * jax.nn.sigmoid(gt) * conv(ua, a0)).astype(BF16)
            part = jnp.dot(act, wd_ref[a0:a0 + FFN_FC, :], preferred_element_type=F32)
            f = part if f is None else f + part
        o_ref[r0:r0 + rows, :] = _residual_layer_norm(x_ref[r0:r0 + rows, :], f, mod_ref[5:6, :],
                                                      g_ref[...], b_ref[...])


def _conv_ffn(x, mod, w_up, conv_w, conv_b, w_down, g, b, tm):
    n = x.shape[0]
    nh = n // HALO
    per = tm // HALO
    rows = tm // FFN_CHAINS
    resident = pl.Buffered(1)
    return pl.pallas_call(
        _ffn_kernel,
        grid=(n // tm,),
        in_specs=[pl.BlockSpec((tm, D_MODEL), lambda i: (i, 0)),
                  pl.BlockSpec((HALO, D_MODEL), lambda i: (jnp.maximum(i * per - 1, 0), 0)),
                  pl.BlockSpec((HALO, D_MODEL), lambda i: (jnp.minimum((i + 1) * per, nh - 1), 0)),
                  _full((6, D_MODEL)),
                  pl.BlockSpec((D_MODEL, 2 * D_FF), lambda i: (0, 0), pipeline_mode=resident),
                  _full((3, 2 * D_FF)), _full((1, 2 * D_FF)),
                  pl.BlockSpec((D_FF, D_MODEL), lambda i: (0, 0), pipeline_mode=resident),
                  _full((1, D_MODEL)), _full((1, D_MODEL))],
        out_specs=pl.BlockSpec((tm, D_MODEL), lambda i: (i, 0)),
        out_shape=jax.ShapeDtypeStruct((n, D_MODEL), F32),
        scratch_shapes=[pltpu.VMEM((tm + 2 * HALO, D_MODEL), BF16),
                        pltpu.VMEM((FFN_CHAINS, FFN_CHUNKS, rows + 2 * HALO, FFN_FC), F32),
                        pltpu.VMEM((FFN_CHAINS, FFN_CHUNKS, rows + 2 * HALO, FFN_FC), F32)],
        compiler_params=_params(("parallel",)),
        name="conv_ffn_ln",
    )(x, x, x, mod, w_up, conv_w, conv_b, w_down, g, b)


def _pair_perm(n_heads, kv_heads):
    per = n_heads // kv_heads
    order = []
    for j in range(per):
        for g in range(kv_heads):
            order.append(g * per + j)
    cols = np.concatenate([np.arange(h * HEAD_DIM, (h + 1) * HEAD_DIM) for h in order])
    return cols


_PERM_A = _pair_perm(A_HEADS, A_KV_HEADS)
_PERM_C = _pair_perm(C_HEADS, C_KV_HEADS)


def _rope_tables(n):
    t = np.arange(n)
    row = (t // GRID_W).astype(np.float64)
    col = (t % GRID_W).astype(np.float64)
    half = HEAD_DIM // 2
    inv = ROPE_THETA ** (-np.arange(0, half, 2, dtype=np.float64) / half)
    ang_r = row[:, None] * inv
    ang_c = col[:, None] * inv
    ang = np.concatenate([ang_r, ang_r, ang_c, ang_c], -1)
    ang = np.concatenate([ang, ang], -1)
    sign = np.where(np.arange(LANES) % 32 < 16, -1.0, 1.0)
    return jnp.asarray(np.cos(ang), F32), jnp.asarray(np.sin(ang) * sign, F32)


_EVEN_GROUPS = (
    (0, A_Q, 0, True, QK_SCALE),
    (A_Q, A_KV, A_Q, True, 1.0),
    (A_Q + A_KV, A_KV, None, False, 1.0),
    (A_Q + 2 * A_KV, B_W, None, False, QK_SCALE),
    (A_Q + 2 * A_KV + B_W, B_W, None, False, 1.0),
    (A_Q + 2 * A_KV + 2 * B_W, B_W, None, False, 1.0),
)
_EVEN_GROUPS_CTX = tuple((c0, w, g0, False, s) for (c0, w, g0, _, s) in _EVEN_GROUPS)
_ODD_GROUPS = (
    (0, C_Q, None, True, QK_SCALE),
    (C_Q, C_KV, None, True, 1.0),
    (C_Q + C_KV, C_KV, None, False, 1.0),
)
_ODD_GROUPS_CTX = tuple((c0, w, g0, False, s) for (c0, w, g0, _, s) in _ODD_GROUPS)

LAT_TM = 512


def kernel(x, c, ctx, c_ctx, ada_w, ada_b, ln_g, ln_b, ev_w_in, ev_w_out, ev_q_gain, ev_k_gain, ev_rpb,
           od_w_in, od_w_out, od_sink, ffn_w_up, ffn_conv_w, ffn_conv_b, ffn_w_down):
    n = x.shape[1]
    nc = ctx.shape[1]
    x_lat = x[0]
    x_ctx = ctx[0]

    cvec = jnp.zeros((8, D_MODEL), F32).at[0].set(c[0]).at[1].set(c_ctx)
    mods = _modulation(cvec, ada_w, ada_b)

    cos, sin = _rope_tables(n)
    cos_c, sin_c = cos[:nc], sin[:nc]
    lane = np.arange(LANES)
    gmat = jnp.asarray((lane[:, None] // HEAD_DIM) == (lane[None, :] // HEAD_DIM), BF16)
    no_sink = jnp.full((C_HEADS,), NEG, F32)

    for l in range(DEPTH):
        i = l // 2
        ctx_out = l < DEPTH - 1
        m_lat = mods[l, 0].reshape(6, D_MODEL)
        m_ctx = mods[l, 1].reshape(6, D_MODEL)
        g0, b0 = ln_g[l, 0][None], ln_b[l, 0][None]
        g1, b1 = ln_g[l, 1][None], ln_b[l, 1][None]
        if l % 2 == 0:
            w_in = ev_w_in[i]
            w_in = jnp.concatenate([w_in[:, :A_Q][:, _PERM_A], w_in[:, A_Q:]], axis=1).astype(BF16)
            w_out = ev_w_out[i]
            w_out = jnp.concatenate([w_out[:A_Q][_PERM_A], w_out[A_Q:]], axis=0).astype(BF16)
            gains = jnp.concatenate([jnp.tile(ev_q_gain[i], A_HEADS), jnp.tile(ev_k_gain[i], A_KV_HEADS)])[None]
            qa, ka, va, qb, kb, vb = _project(x_lat, m_lat, w_in, gains, cos, sin, gmat, _EVEN_GROUPS, LAT_TM)
            qa_c, ka_c, va_c, qb_c, kb_c, vb_c = _project(x_ctx, m_ctx, w_in, gains, cos_c, sin_c, gmat,
                                                           _EVEN_GROUPS_CTX, nc)
            ya = _global_attention(qa, ka, va, ka_c, va_c)
            yb = _nbr_attention(qb, kb, vb, kb_c, vb_c, _nbr_bias_tiles(ev_rpb[i] * LOG2E))
            x_lat = _out_project(x_lat, ya, yb, 0, w_out, m_lat, g0, b0, LAT_TM)
            if ctx_out:
                ya_c = _ctx_attention(qa_c, ka_c, va_c, no_sink, 0)
                yb_c = _ctx_attention(qb_c, kb_c, vb_c, no_sink, 1)
                x_ctx = _out_project(x_ctx, ya_c, yb_c, 0, w_out, m_ctx, g0, b0, nc)
        else:
            w_in = od_w_in[i]
            w_in = jnp.concatenate([w_in[:, :C_Q][:, _PERM_C], w_in[:, C_Q:]], axis=1).astype(BF16)
            w_out = od_w_out[i][_PERM_C].astype(BF16)
            sink = od_sink[i] * LOG2E
            gains = jnp.ones((1, A_Q + A_KV), F32)
            q, k, v = _project(x_lat, m_lat, w_in, gains, cos, sin, gmat, _ODD_GROUPS, LAT_TM)
            q_c, k_c, v_c = _project(x_ctx, m_ctx, w_in, gains, cos_c, sin_c, gmat, _ODD_GROUPS_CTX, nc)
            y = _window_attention(q, k, v, k_c, v_c, sink)
            x_lat = _out_project(x_lat, y, y, 1, w_out, m_lat, g0, b0, LAT_TM)
            if ctx_out:
                y_c = _ctx_attention(q_c, k_c, v_c, sink, 0)
                x_ctx = _out_project(x_ctx, y_c, y_c, 1, w_out, m_ctx, g0, b0, nc)
        w_up = ffn_w_up[l].astype(BF16)
        w_down = ffn_w_down[l].astype(BF16)
        cb = ffn_conv_b[l][None]
        x_lat = _conv_ffn(x_lat, m_lat, w_up, ffn_conv_w[l], cb, w_down, g1, b1, LAT_TM)
        if ctx_out:
            x_ctx = _conv_ffn(x_ctx, m_ctx, w_up, ffn_conv_w[l], cb, w_down, g1, b1, nc)
    return x_lat[None]
```

```python
import functools

import numpy as np
import jax
import jax.numpy as jnp
from jax import lax
from jax.experimental import pallas as pl
from jax.experimental.pallas import tpu as pltpu

D_MODEL = 1024
DEPTH = 4
GRID_W = 64
HEAD_DIM = 64
A_HEADS = 8
A_KV_HEADS = 2
B_HEADS = 8
NB_KH = 8
NB_KW = 16
C_HEADS = 16
C_KV_HEADS = 2
C_WINDOW = 128
D_FF = 2816
ROPE_THETA = 10000.0
LN_EPS = 1e-5
RMS_EPS = 1e-6
NEG = -1e30
BIG = 1e30
DN_ALPHA = (2 * DEPTH) ** 0.25
A_Q = A_HEADS * HEAD_DIM
A_KV = A_KV_HEADS * HEAD_DIM
B_W = B_HEADS * HEAD_DIM
C_Q = C_HEADS * HEAD_DIM
C_KV = C_KV_HEADS * HEAD_DIM
LOG2E = 1.4426950408889634
QK_SCALE = HEAD_DIM ** -0.5 * LOG2E
DEN_HIGH_LOG2 = 100.0
DEN_LOW_LOG2 = 64.0

LANES = 128
SUBLANES = 8
HALO = SUBLANES
VMEM_LIMIT = 52 * 1024 * 1024

BF16 = jnp.bfloat16
F32 = jnp.float32

_NT = (((1,), (1,)), ((), ()))


def _params(sem):
    return pltpu.CompilerParams(dimension_semantics=sem, vmem_limit_bytes=VMEM_LIMIT)


def _full(shape):
    return pl.BlockSpec(shape, lambda *_: (0,) * len(shape))


def _lane_lo(shape):
    return lax.broadcasted_iota(jnp.int32, shape, len(shape) - 1) < HEAD_DIM


def _unit_lane(half):
    return HEAD_DIM if half == 0 else 0


def _unit_lane_variants(x):
    lane = jnp.arange(x.shape[-1]) % LANES
    one = jnp.ones((), x.dtype)
    zero = jnp.zeros((), x.dtype)
    lo = jnp.where(lane < HEAD_DIM, x, jnp.where(lane == HEAD_DIM, one, zero))
    hi = jnp.where(lane >= HEAD_DIM, x, jnp.where(lane == 0, one, zero))
    return jnp.stack([lo, hi])


def _fold_denominators(high, low, den):
    for t in range(den.shape[1] // LANES):
        piece = den[:, t * LANES:(t + 1) * LANES]
        high = jnp.maximum(high, piece)
        low = jnp.minimum(low, piece)
    return high, low


def _store_flags(flag_ref, high, low):
    flag_ref[0:SUBLANES, :] = jnp.broadcast_to(high, (SUBLANES, LANES))
    flag_ref[SUBLANES:2 * SUBLANES, :] = jnp.broadcast_to(low, (SUBLANES, LANES))


def _outside_safe_range(flags):
    f = flags.reshape(-1, 2, SUBLANES, LANES)
    return jnp.logical_not((jnp.max(f[:, 0]) < 2.0 ** DEN_HIGH_LOG2) & (jnp.min(f[:, 1]) > 2.0 ** -DEN_LOW_LOG2))


def _mod_kernel(c_ref, w_ref, b_ref, o_ref):
    s = c_ref[...]
    s = s * jax.nn.sigmoid(s)
    o_ref[0] = jnp.dot(s, w_ref[0], preferred_element_type=F32, precision=lax.Precision.HIGHEST) + b_ref[0]


def _modulation(cvec, ada_w, ada_b):
    nb = 6
    return pl.pallas_call(
        _mod_kernel,
        grid=(DEPTH, nb),
        in_specs=[pl.BlockSpec((8, D_MODEL), lambda l, j: (0, 0)),
                  pl.BlockSpec((1, D_MODEL, D_MODEL), lambda l, j: (l, 0, j)),
                  pl.BlockSpec((1, 1, D_MODEL), lambda l, j: (l, 0, j))],
        out_specs=pl.BlockSpec((1, 8, D_MODEL), lambda l, j: (l, 0, j)),
        out_shape=jax.ShapeDtypeStruct((DEPTH, 8, 6 * D_MODEL), F32),
        compiler_params=_params(("arbitrary", "arbitrary")),
        name="modulation",
    )(cvec, ada_w, ada_b.reshape(DEPTH, 1, 6 * D_MODEL))


def _group_sumsq(z, gmat):
    x2 = z * z
    hi = x2.astype(BF16)
    lo = (x2 - hi.astype(F32)).astype(BF16)
    return (jnp.dot(hi, gmat, preferred_element_type=F32) + jnp.dot(lo, gmat, preferred_element_type=F32))


def _proj_kernel(groups, x_ref, mod_ref, w_ref, gain_ref, cos_ref, sin_ref, gmat_ref, *out_refs):
    shift = mod_ref[0:1, :]
    scale = mod_ref[1:2, :]
    h = (x_ref[...] * (1.0 + scale) + shift).astype(BF16)
    cos = cos_ref[...]
    sin = sin_ref[...]
    gmat = gmat_ref[...]
    first = lax.broadcasted_iota(jnp.int32, cos.shape, 1) % 32 < 16
    for (c0, width, g0, rope, qscale), o_ref in zip(groups, out_refs):
        z = jnp.dot(h, w_ref[:, c0:c0 + width], preferred_element_type=F32)
        for b in range(width // LANES):
            zb = z[:, b * LANES:(b + 1) * LANES]
            if g0 is not None:
                ms = _group_sumsq(zb, gmat) * (1.0 / HEAD_DIM)
                zb = zb * lax.rsqrt(ms + RMS_EPS) * gain_ref[:, g0 + b * LANES:g0 + (b + 1) * LANES]
            if rope:
                rot = jnp.where(first, pltpu.roll(zb, LANES - 16, 1), pltpu.roll(zb, 16, 1))
                zb = zb * cos + rot * sin
            if qscale != 1.0:
                zb = zb * qscale
            o_ref[:, b * LANES:(b + 1) * LANES] = zb.astype(BF16)


def _project(x, mod, w, gains, cos, sin, gmat, groups, tm):
    n = x.shape[0]
    win = w.shape[1]
    out_shape = [jax.ShapeDtypeStruct((n, g[1]), BF16) for g in groups]
    out_specs = [pl.BlockSpec((tm, g[1]), lambda i: (i, 0)) for g in groups]
    return pl.pallas_call(
        functools.partial(_proj_kernel, groups),
        grid=(n // tm,),
        in_specs=[pl.BlockSpec((tm, D_MODEL), lambda i: (i, 0)),
                  _full((6, D_MODEL)),
                  _full((D_MODEL, win)),
                  _full(gains.shape),
                  pl.BlockSpec((tm, LANES), lambda i: (i, 0)),
                  pl.BlockSpec((tm, LANES), lambda i: (i, 0)),
                  _full((LANES, LANES))],
        out_specs=out_specs,
        out_shape=out_shape,
        compiler_params=_params(("parallel",)),
        name="qkv_project",
    )(x, mod, w, gains, cos, sin, gmat)


GA_TQ = 1024
GA_TK = 2048


GA_VROWS = 80
GA_KEY_CHUNK = 1024
GA_QUERY_CHUNK = 512


def _vt_variants(v):
    t = v.shape[0]
    tail = jnp.concatenate([jnp.ones((1, t), v.dtype), jnp.zeros((GA_VROWS - HEAD_DIM - 1, t), v.dtype)])
    return jnp.stack([jnp.concatenate([v[:, :HEAD_DIM].T, tail]), jnp.concatenate([v[:, HEAD_DIM:].T, tail])])


def _global_attn_kernel(q_ref, kc_ref, vct_ref, k_ref, vt_ref, o_ref, flag_ref, qx_ref, acc_ref):
    kk = pl.program_id(1)
    tq = q_ref.shape[0]
    nblk = q_ref.shape[1] // LANES

    def update(kb_ref, vtb_ref):
        nkeys = kb_ref.shape[0]
        kr = min(GA_KEY_CHUNK, nkeys)
        for j in range(nblk):
            for c0 in range(0, tq, GA_QUERY_CHUNK):
                cs = slice(c0, c0 + GA_QUERY_CHUNK)
                part = [None, None]
                for r0 in range(0, nkeys, kr):
                    st = [lax.dot_general(kb_ref[r0:r0 + kr, :], qx_ref[2 * j + half, cs, :], _NT,
                                          preferred_element_type=F32) for half in range(2)]
                    for half in range(2):
                        pv = jnp.dot(vtb_ref[half, :, r0:r0 + kr], jnp.exp2(st[half]).astype(BF16),
                                     preferred_element_type=F32)
                        part[half] = pv if part[half] is None else part[half] + pv
                for half in range(2):
                    acc_ref[2 * j + half, :, cs] += part[half]

    @pl.when(kk == 0)
    def _():
        lo = _lane_lo((tq, LANES))
        for j in range(nblk):
            qj = q_ref[:, j * LANES:(j + 1) * LANES]
            zero = jnp.zeros_like(qj)
            qx_ref[2 * j] = jnp.where(lo, qj, zero)
            qx_ref[2 * j + 1] = jnp.where(lo, zero, qj)
        acc_ref[...] = jnp.zeros(acc_ref.shape, F32)
        update(kc_ref, vct_ref)

    update(k_ref, vt_ref)

    @pl.when(kk == pl.num_programs(1) - 1)
    def _():
        high = jnp.zeros((1, LANES), F32)
        low = jnp.full((1, LANES), BIG, F32)
        for j in range(nblk):
            halves = []
            for half in range(2):
                a = acc_ref[2 * j + half]
                den = a[HEAD_DIM:HEAD_DIM + 1, :]
                high, low = _fold_denominators(high, low, den)
                halves.append(a[0:HEAD_DIM, :] / den)
            o_ref[:, j * LANES:(j + 1) * LANES] = jnp.concatenate(halves, axis=0).T.astype(o_ref.dtype)
        _store_flags(flag_ref, high, low)


def _global_attn_exact_kernel(q_ref, kc_ref, vc_ref, k_ref, v_ref, o_ref, qx_ref, c_ref, acc_ref):
    kk = pl.program_id(1)
    tq = q_ref.shape[0]
    nblk = q_ref.shape[1] // LANES
    nh = 2 * nblk
    lane = lax.broadcasted_iota(jnp.int32, (tq, LANES), 1)

    def moving_reference_update(kx_ref, vx_ref, first):
        for h in range(nh):
            half = h % 2
            unit = _unit_lane(half)
            sp = lax.dot_general(qx_ref[h], kx_ref[half], _NT, preferred_element_type=F32)
            mb = jnp.max(sp, axis=-1, keepdims=True)
            c_old = c_ref[h]
            target = mb if first else jnp.maximum(c_old, c_old + mb)
            qx_ref[h] = jnp.where(lane == unit, -target, qx_ref[h].astype(F32)).astype(BF16)
            c_new = -qx_ref[h][:, unit:unit + 1].astype(F32)
            d = c_new - c_old
            pv = jnp.dot(jnp.exp2(sp - d).astype(BF16), vx_ref[half], preferred_element_type=F32)
            acc_ref[h] = pv if first else acc_ref[h] * jnp.exp2(-d) + pv
            c_ref[h] = c_new

    @pl.when(kk == 0)
    def _():
        lo = _lane_lo((tq, LANES))
        for j in range(nblk):
            qj = q_ref[:, j * LANES:(j + 1) * LANES]
            zero = jnp.zeros_like(qj)
            qx_ref[2 * j] = jnp.where(lo, qj, zero)
            qx_ref[2 * j + 1] = jnp.where(lo, zero, qj)
        c_ref[...] = jnp.zeros(c_ref.shape, F32)
        moving_reference_update(kc_ref, vc_ref, True)

    moving_reference_update(k_ref, v_ref, False)

    @pl.when(kk == pl.num_programs(1) - 1)
    def _():
        lo = _lane_lo((tq, LANES))
        for j in range(nblk):
            a_lo = acc_ref[2 * j]
            a_hi = acc_ref[2 * j + 1]
            o_lo = a_lo / a_lo[:, HEAD_DIM:HEAD_DIM + 1]
            o_hi = a_hi / a_hi[:, 0:1]
            o_ref[:, j * LANES:(j + 1) * LANES] = jnp.where(lo, o_lo, o_hi).astype(o_ref.dtype)


def _global_attention(q, k, v, kc, vc):
    n, qw = q.shape
    nc = kc.shape[0]
    nh = 2 * (qw // LANES)
    nq = n // GA_TQ
    grid = (nq, n // GA_TK)
    q_spec = pl.BlockSpec((GA_TQ, qw), lambda i, j: (i, 0))

    y, flags = pl.pallas_call(
        _global_attn_kernel,
        grid=grid,
        in_specs=[q_spec,
                  _full((nc, LANES)), _full((2, GA_VROWS, nc)),
                  pl.BlockSpec((GA_TK, LANES), lambda i, j: (j, 0)),
                  pl.BlockSpec((2, GA_VROWS, GA_TK), lambda i, j: (0, 0, j))],
        out_specs=[q_spec, pl.BlockSpec((2 * SUBLANES, LANES), lambda i, j: (i, 0))],
        out_shape=[jax.ShapeDtypeStruct((n, qw), BF16),
                   jax.ShapeDtypeStruct((nq * 2 * SUBLANES, LANES), F32)],
        scratch_shapes=[pltpu.VMEM((nh, GA_TQ, LANES), BF16),
                        pltpu.VMEM((nh, GA_VROWS, GA_TQ), F32)],
        compiler_params=_params(("parallel", "arbitrary")),
        name="global_attention",
    )(q, kc, _vt_variants(vc), k, _vt_variants(v))

    def exact():
        return pl.pallas_call(
            _global_attn_exact_kernel,
            grid=grid,
            in_specs=[q_spec,
                      _full((2, nc, LANES)), _full((2, nc, LANES)),
                      pl.BlockSpec((2, GA_TK, LANES), lambda i, j: (0, j, 0)),
                      pl.BlockSpec((2, GA_TK, LANES), lambda i, j: (0, j, 0))],
            out_specs=q_spec,
            out_shape=jax.ShapeDtypeStruct((n, qw), BF16),
            scratch_shapes=[pltpu.VMEM((nh, GA_TQ, LANES), BF16),
                            pltpu.VMEM((nh, GA_TQ, 1), F32),
                            pltpu.VMEM((nh, GA_TQ, LANES), F32)],
            compiler_params=_params(("parallel", "arbitrary")),
            name="global_attention_exact",
        )(q, _unit_lane_variants(kc), _unit_lane_variants(vc), _unit_lane_variants(k), _unit_lane_variants(v))

    return lax.cond(_outside_safe_range(flags), exact, lambda: y)


NB_QROWS = 8
NB_KROWS = NB_QROWS + NB_KH
NB_MASKED = 2 * NB_KH - 1
NB_TILES = 3 * (NB_MASKED + 1)


def _nbr_bias_tiles(rpb):
    nh = rpb.shape[0]
    col = np.arange(GRID_W)
    cs = np.clip(col - NB_KW // 2, 0, GRID_W - NB_KW)
    col_ok = (col[:, None] >= cs[None, :]) & (col[:, None] < cs[None, :] + NB_KW)
    dcol = col[:, None] - col[None, :] + NB_KW - 1
    sel_col = (dcol[:, :, None] == np.arange(2 * NB_KW - 1)) & col_ok[:, :, None]
    base = jnp.einsum('kqj,hdj->hdkq', jnp.asarray(sel_col, F32), rpb, precision=lax.Precision.HIGHEST)
    base = jnp.where(jnp.asarray(col_ok), base, NEG)
    masked = jnp.full((nh, 1, GRID_W, GRID_W), NEG, F32)
    base = jnp.concatenate([base, masked], axis=1)
    prev = jnp.concatenate([masked, base[:, :-1]], axis=1)
    allmasked = jnp.full_like(base, NEG)
    return jnp.concatenate([jnp.concatenate([base, prev], -1),
                            jnp.concatenate([base, allmasked], -1),
                            jnp.concatenate([allmasked, base], -1)], axis=1)


def _nbr_tile_index(rows):
    nq = rows // NB_QROWS
    idx = np.zeros((3, NB_KROWS, NB_QROWS // 2), np.int64)
    for v, blk in enumerate((0, 1, nq - 1)):
        r0 = blk * NB_QROWS
        k0 = int(np.clip(r0 - NB_KH // 2, 0, rows - NB_KROWS))
        for kr in range(NB_KROWS):
            ka = k0 + kr

            def code(qr):
                qa = r0 + qr
                rs = int(np.clip(qa - NB_KH // 2, 0, rows - NB_KH))
                return ka - qa + NB_KH - 1 if rs <= ka < rs + NB_KH else NB_MASKED

            for qp in range(NB_QROWS // 2):
                left, right = code(2 * qp), code(2 * qp + 1)
                if left < NB_MASKED and right < NB_MASKED:
                    assert right == left - 1
                    idx[v, kr, qp] = left
                elif left < NB_MASKED:
                    idx[v, kr, qp] = (NB_MASKED + 1) + left
                else:
                    idx[v, kr, qp] = 2 * (NB_MASKED + 1) + right
    return tuple(tuple(tuple(int(t) for t in row) for row in var) for var in idx)


def _nbr_attn_kernel(exact, tile_index, q_ref, k_ref, vt_ref, kc_ref, vct_ref, tiles_ref, o_ref, flag_ref,
                     bias_ref):
    i = pl.program_id(1)
    nq = pl.num_programs(1)
    rows = k_ref.shape[0] // GRID_W
    span = NB_KROWS * GRID_W

    def fill_bias(variant):
        for half in range(2):
            for kr in range(NB_KROWS):
                for qp in range(NB_QROWS // 2):
                    bias_ref[half, kr * GRID_W:(kr + 1) * GRID_W, qp * LANES:(qp + 1) * LANES] = (
                        tiles_ref[half, tile_index[variant][kr][qp]])

    for variant, at in enumerate((0, 1, nq - 1)):
        pl.when(i == at)(functools.partial(fill_bias, variant))

    r0 = i * NB_QROWS
    ks = pl.multiple_of(jnp.clip(r0 - NB_KH // 2, 0, rows - NB_KROWS) * GRID_W, 2 * LANES)
    kw = k_ref[pl.ds(ks, span), :]
    kc = kc_ref[...]
    q = q_ref[...]
    tq = q.shape[0]
    lo = _lane_lo(q.shape)
    zero = jnp.zeros_like(q)
    high = jnp.zeros((1, LANES), F32)
    low = jnp.full((1, LANES), BIG, F32)
    halves = []
    logits = []
    for half in range(2):
        qm = jnp.where(lo, q, zero) if half == 0 else jnp.where(lo, zero, q)
        logits.append((lax.dot_general(kw, qm, _NT, preferred_element_type=F32),
                       lax.dot_general(kc, qm, _NT, preferred_element_type=F32)))
    for half in range(2):
        s_nb = logits[half][0] + bias_ref[half]
        s_cx = logits[half][1]
        if exact:
            m = jnp.maximum(jnp.max(s_nb, axis=0, keepdims=True), jnp.max(s_cx, axis=0, keepdims=True))
            s_nb = s_nb - m
            s_cx = s_cx - m
        vtw = jnp.concatenate([vt_ref[0, half, ks // LANES + t] for t in range(span // LANES)], axis=1)
        ot = (jnp.dot(vtw, jnp.exp2(s_nb).astype(BF16), preferred_element_type=F32)
              + jnp.dot(vct_ref[0, half], jnp.exp2(s_cx).astype(BF16), preferred_element_type=F32))
        den = ot[HEAD_DIM:HEAD_DIM + 1, :]
        high, low = _fold_denominators(high, low, den)
        halves.append(ot[0:HEAD_DIM, :] / den)
    o_ref[...] = jnp.concatenate(halves, axis=0).T.astype(o_ref.dtype)
    _store_flags(flag_ref, high, low)


def _nbr_attention(q, k, v, kc, vc, tiles):
    n, qw = q.shape
    nc = kc.shape[0]
    tq = NB_QROWS * GRID_W
    nq = n // tq
    assert nq >= 3
    npair = qw // LANES
    tile_index = _nbr_tile_index(n // GRID_W)
    vt = jnp.stack([_vt_chunks(v[:, p * LANES:(p + 1) * LANES]) for p in range(npair)])
    vct = jnp.stack([_vt_variants(vc[:, p * LANES:(p + 1) * LANES]) for p in range(npair)])
    operands = (q, k, vt, kc, vct, tiles)

    def run(exact):
        return pl.pallas_call(
            functools.partial(_nbr_attn_kernel, exact, tile_index),
            grid=(npair, nq),
            in_specs=[pl.BlockSpec((tq, LANES), lambda p, i: (i, p)),
                      pl.BlockSpec((n, LANES), lambda p, i: (0, p)),
                      pl.BlockSpec((1, 2, n // LANES, GA_VROWS, LANES), lambda p, i: (p, 0, 0, 0, 0)),
                      pl.BlockSpec((nc, LANES), lambda p, i: (0, p)),
                      pl.BlockSpec((1, 2, GA_VROWS, nc), lambda p, i: (p, 0, 0, 0)),
                      pl.BlockSpec((2, NB_TILES, GRID_W, LANES), lambda p, i: (p, 0, 0, 0))],
            out_specs=[pl.BlockSpec((tq, LANES), lambda p, i: (i, p)),
                       pl.BlockSpec((2 * SUBLANES, LANES), lambda p, i: (p * nq + i, 0))],
            out_shape=[jax.ShapeDtypeStruct((n, qw), BF16),
                       jax.ShapeDtypeStruct((npair * nq * 2 * SUBLANES, LANES), F32)],
            scratch_shapes=[pltpu.VMEM((2, NB_KROWS * GRID_W, tq), F32)],
            compiler_params=_params(("parallel", "arbitrary")),
            name="neighbourhood_attention_exact" if exact else "neighbourhood_attention",
        )(*operands)

    y, flags = run(False)
    return lax.cond(_outside_safe_range(flags), lambda: run(True)[0], lambda: y)


WIN_TQ = 256
WIN_SPAN = WIN_TQ + 2 * C_WINDOW


def _window_attn_kernel(exact, sink_ref, q_ref, k_ref, vt_ref, kc_ref, vct_ref, o_ref, flag_ref):
    i = pl.program_id(0)
    n = k_ref.shape[0]
    tq = q_ref.shape[0]
    nblk = q_ref.shape[1] // LANES
    nchunk = WIN_SPAN // LANES
    q0 = i * tq
    ks = pl.multiple_of(jnp.clip(q0 - C_WINDOW, 0, n - WIN_SPAN), C_WINDOW)
    kw = k_ref[pl.ds(ks, WIN_SPAN), :]
    kc = kc_ref[...]
    kpos = ks + lax.broadcasted_iota(jnp.int32, (WIN_SPAN, tq), 0)
    qpos = q0 + lax.broadcasted_iota(jnp.int32, (WIN_SPAN, tq), 1)
    valid = jnp.abs(kpos - qpos) <= C_WINDOW
    lo = _lane_lo((tq, LANES))
    high = jnp.zeros((1, LANES), F32)
    low = jnp.full((1, LANES), BIG, F32)
    for j in range(nblk):
        q = q_ref[:, j * LANES:(j + 1) * LANES]
        zero = jnp.zeros_like(q)
        halves = []
        logits = []
        for half in range(2):
            qm = jnp.where(lo, q, zero) if half == 0 else jnp.where(lo, zero, q)
            logits.append((lax.dot_general(kw, qm, _NT, preferred_element_type=F32),
                           lax.dot_general(kc, qm, _NT, preferred_element_type=F32)))
        for half in range(2):
            sink = jnp.full((1, tq), sink_ref[j + nblk * half], F32)
            s_w = jnp.where(valid, logits[half][0], NEG)
            s_c = logits[half][1]
            if exact:
                m = jnp.maximum(jnp.maximum(jnp.max(s_w, axis=0, keepdims=True),
                                            jnp.max(s_c, axis=0, keepdims=True)), sink)
                s_w = s_w - m
                s_c = s_c - m
                sink = sink - m
            vtw = jnp.concatenate([vt_ref[half, ks // LANES + t] for t in range(nchunk)], axis=1)
            ot = (jnp.dot(vtw, jnp.exp2(s_w).astype(BF16), preferred_element_type=F32)
                  + jnp.dot(vct_ref[half], jnp.exp2(s_c).astype(BF16), preferred_element_type=F32))
            den = ot[HEAD_DIM:HEAD_DIM + 1, :] + jnp.exp2(sink)
            high, low = _fold_denominators(high, low, den)
            halves.append(ot[0:HEAD_DIM, :] / den)
        o_ref[:, j * LANES:(j + 1) * LANES] = jnp.concatenate(halves, axis=0).T.astype(o_ref.dtype)
    _store_flags(flag_ref, high, low)


def _vt_chunks(v):
    t = v.shape[0]
    return _vt_variants(v).reshape(2, GA_VROWS, t // LANES, LANES).transpose(0, 2, 1, 3)


def _window_attention(q, k, v, kc, vc, sink):
    n, qw = q.shape
    nc = kc.shape[0]
    nq = n // WIN_TQ
    operands = (sink, q, k, _vt_chunks(v), kc, _vt_variants(vc))

    def run(exact):
        return pl.pallas_call(
            functools.partial(_window_attn_kernel, exact),
            grid=(nq,),
            in_specs=[pl.BlockSpec(memory_space=pltpu.SMEM),
                      pl.BlockSpec((WIN_TQ, qw), lambda i: (i, 0)),
                      _full((n, LANES)), _full((2, n // LANES, GA_VROWS, LANES)),
                      _full((nc, LANES)), _full((2, GA_VROWS, nc))],
            out_specs=[pl.BlockSpec((WIN_TQ, qw), lambda i: (i, 0)),
                       pl.BlockSpec((2 * SUBLANES, LANES), lambda i: (i, 0))],
            out_shape=[jax.ShapeDtypeStruct((n, qw), BF16),
                       jax.ShapeDtypeStruct((nq * 2 * SUBLANES, LANES), F32)],
            compiler_params=_params(("parallel",)),
            name="window_attention_exact" if exact else "window_attention",
        )(*operands)

    y, flags = run(False)
    return lax.cond(_outside_safe_range(flags), lambda: run(True)[0], lambda: y)


def _ctx_attn_kernel(sink_ref, q_ref, k_ref, v_ref, o_ref):
    j = pl.program_id(0)
    nblk = pl.num_programs(0)
    q = q_ref[...]
    k = k_ref[...]
    v = v_ref[...]
    lo = _lane_lo(q.shape)
    zero = jnp.zeros_like(q)
    outs = []
    for half in range(2):
        qm = jnp.where(lo, q, zero) if half == 0 else jnp.where(lo, zero, q)
        sink = sink_ref[j + nblk * half]
        s = lax.dot_general(qm, k, _NT, preferred_element_type=F32)
        m = jnp.maximum(jnp.max(s, axis=-1, keepdims=True), sink)
        p = jnp.exp2(s - m)
        den = jnp.sum(p, axis=-1, keepdims=True) + jnp.exp2(sink - m)
        outs.append(jnp.dot(p.astype(BF16), v, preferred_element_type=F32) / den)
    o_ref[...] = jnp.where(lo, outs[0], outs[1]).astype(o_ref.dtype)


def _ctx_attention(q, k, v, sink, kv_per_block):
    c, qw = q.shape
    nblk = qw // LANES
    return pl.pallas_call(
        _ctx_attn_kernel,
        grid=(nblk,),
        in_specs=[pl.BlockSpec(memory_space=pltpu.SMEM),
                  pl.BlockSpec((c, LANES), lambda j: (0, j)),
                  pl.BlockSpec((c, LANES), lambda j: (0, j * kv_per_block)),
                  pl.BlockSpec((c, LANES), lambda j: (0, j * kv_per_block))],
        out_specs=pl.BlockSpec((c, LANES), lambda j: (0, j)),
        out_shape=jax.ShapeDtypeStruct((c, qw), BF16),
        compiler_params=_params(("parallel",)),
        name="context_attention",
    )(sink, q, k, v)


def _residual_layer_norm(x, f, gate, g, b):
    z = x + (gate * (1.0 / DN_ALPHA)) * f
    mu = jnp.mean(z, axis=-1, keepdims=True)
    zc = z - mu
    var = jnp.mean(zc * zc, axis=-1, keepdims=True)
    return zc * lax.rsqrt(var + LN_EPS / (DN_ALPHA * DN_ALPHA)) * g + b


def _outproj_kernel(x_ref, y1_ref, y2_ref, w1_ref, w2_ref, mod_ref, g_ref, b_ref, o_ref):
    rows = x_ref.shape[0] // 2
    for s in range(2):
        rs = slice(s * rows, (s + 1) * rows)
        f = (jnp.dot(y1_ref[rs, :], w1_ref[...], preferred_element_type=F32)
             + jnp.dot(y2_ref[rs, :], w2_ref[...], preferred_element_type=F32))
        o_ref[rs, :] = _residual_layer_norm(x_ref[rs, :], f, mod_ref[2:3, :], g_ref[...], b_ref[...])


def _out_project(x, y1, y2, c2, w, mod, g, b, tm):
    n = x.shape[0]
    half = D_MODEL // 2
    return pl.pallas_call(
        _outproj_kernel,
        grid=(n // tm,),
        in_specs=[pl.BlockSpec((tm, D_MODEL), lambda i: (i, 0)),
                  pl.BlockSpec((tm, half), lambda i: (i, 0)),
                  pl.BlockSpec((tm, half), lambda i: (i, c2)),
                  pl.BlockSpec((half, D_MODEL), lambda i: (0, 0)),
                  pl.BlockSpec((half, D_MODEL), lambda i: (1, 0)),
                  _full((6, D_MODEL)), _full((1, D_MODEL)), _full((1, D_MODEL))],
        out_specs=pl.BlockSpec((tm, D_MODEL), lambda i: (i, 0)),
        out_shape=jax.ShapeDtypeStruct((n, D_MODEL), F32),
        compiler_params=_params(("parallel",)),
        name="out_project_ln",
    )(x, y1, y2, w, w, mod, g, b)


FFN_CHUNKS = 2
FFN_FC = D_FF // FFN_CHUNKS


FFN_CHAINS = 1


def _ffn_kernel(x_ref, xp_ref, xn_ref, mod_ref, wu_ref, cw_ref, cb_ref, wd_ref, g_ref, b_ref, o_ref,
                h_ref, ua_ref, ug_ref):
    i = pl.program_id(0)
    tm = x_ref.shape[0]
    rows = tm // FFN_CHAINS
    shift = mod_ref[3:4, :]
    scale = 1.0 + mod_ref[4:5, :]
    hp = jnp.where(i > 0, xp_ref[...] * scale + shift, 0.0)
    hn = jnp.where(i < pl.num_programs(0) - 1, xn_ref[...] * scale + shift, 0.0)
    hm = x_ref[...] * scale + shift
    h_ref[...] = jnp.concatenate([hp, hm, hn], axis=0).astype(BF16)

    def conv(u_ref, col0):
        cols = slice(col0, col0 + FFN_FC)
        return (cb_ref[:, cols]
                + u_ref[HALO - 1:HALO - 1 + rows, :] * cw_ref[0:1, cols]
                + u_ref[HALO:HALO + rows, :] * cw_ref[1:2, cols]
                + u_ref[HALO + 1:HALO + 1 + rows, :] * cw_ref[2:3, cols])

    for s in range(FFN_CHAINS):
        r0 = s * rows
        hs = h_ref[r0:r0 + rows + 2 * HALO, :]
        f = None
        for c in range(FFN_CHUNKS):
            a0 = c * FFN_FC
            g0 = D_FF + c * FFN_FC
            ua = ua_ref.at[s, c]
            ug = ug_ref.at[s, c]
            ua[...] = jnp.dot(hs, wu_ref[:, a0:a0 + FFN_FC], preferred_element_type=F32)
            ug[...] = jnp.dot(hs, wu_ref[:, g0:g0 + FFN_FC], preferred_element_type=F32)
            gt = conv(ug, g0)
            act = (gt * jax.nn.sigmoid(gt) * conv(ua, a0)).astype(BF16)
            part = jnp.dot(act, wd_ref[a0:a0 + FFN_FC, :], preferred_element_type=F32)
            f = part if f is None else f + part
        o_ref[r0:r0 + rows, :] = _residual_layer_norm(x_ref[r0:r0 + rows, :], f, mod_ref[5:6, :],
                                                      g_ref[...], b_ref[...])


def _conv_ffn(x, mod, w_up, conv_w, conv_b, w_down, g, b, tm):
    n = x.shape[0]
    nh = n // HALO
    per = tm // HALO
    rows = tm // FFN_CHAINS
    resident = pl.Buffered(1)
    return pl.pallas_call(
        _ffn_kernel,
        grid=(n // tm,),
        in_specs=[pl.BlockSpec((tm, D_MODEL), lambda i: (i, 0)),
                  pl.BlockSpec((HALO, D_MODEL), lambda i: (jnp.maximum(i * per - 1, 0), 0)),
                  pl.BlockSpec((HALO, D_MODEL), lambda i: (jnp.minimum((i + 1) * per, nh - 1), 0)),
                  _full((6, D_MODEL)),
                  pl.BlockSpec((D_MODEL, 2 * D_FF), lambda i: (0, 0), pipeline_mode=resident),
                  _full((3, 2 * D_FF)), _full((1, 2 * D_FF)),
                  pl.BlockSpec((D_FF, D_MODEL), lambda i: (0, 0), pipeline_mode=resident),
                  _full((1, D_MODEL)), _full((1, D_MODEL))],
        out_specs=pl.BlockSpec((tm, D_MODEL), lambda i: (i, 0)),
        out_shape=jax.ShapeDtypeStruct((n, D_MODEL), F32),
        scratch_shapes=[pltpu.VMEM((tm + 2 * HALO, D_MODEL), BF16),
                        pltpu.VMEM((FFN_CHAINS, FFN_CHUNKS, rows + 2 * HALO, FFN_FC), F32),
                        pltpu.VMEM((FFN_CHAINS, FFN_CHUNKS, rows + 2 * HALO, FFN_FC), F32)],
        compiler_params=_params(("parallel",)),
        name="conv_ffn_ln",
    )(x, x, x, mod, w_up, conv_w, conv_b, w_down, g, b)


def _pair_perm(n_heads, kv_heads):
    per = n_heads // kv_heads
    order = []
    for j in range(per):
        for g in range(kv_heads):
            order.append(g * per + j)
    cols = np.concatenate([np.arange(h * HEAD_DIM, (h + 1) * HEAD_DIM) for h in order])
    return cols


_PERM_A = _pair_perm(A_HEADS, A_KV_HEADS)
_PERM_C = _pair_perm(C_HEADS, C_KV_HEADS)


def _rope_tables(n):
    t = np.arange(n)
    row = (t // GRID_W).astype(np.float64)
    col = (t % GRID_W).astype(np.float64)
    half = HEAD_DIM // 2
    inv = ROPE_THETA ** (-np.arange(0, half, 2, dtype=np.float64) / half)
    ang_r = row[:, None] * inv
    ang_c = col[:, None] * inv
    ang = np.concatenate([ang_r, ang_r, ang_c, ang_c], -1)
    ang = np.concatenate([ang, ang], -1)
    sign = np.where(np.arange(LANES) % 32 < 16, -1.0, 1.0)
    return jnp.asarray(np.cos(ang), F32), jnp.asarray(np.sin(ang) * sign, F32)


_EVEN_GROUPS = (
    (0, A_Q, 0, True, QK_SCALE),
    (A_Q, A_KV, A_Q, True, 1.0),
    (A_Q + A_KV, A_KV, None, False, 1.0),
    (A_Q + 2 * A_KV, B_W, None, False, QK_SCALE),
    (A_Q + 2 * A_KV + B_W, B_W, None, False, 1.0),
    (A_Q + 2 * A_KV + 2 * B_W, B_W, None, False, 1.0),
)
_EVEN_GROUPS_CTX = tuple((c0, w, g0, False, s) for (c0, w, g0, _, s) in _EVEN_GROUPS)
_ODD_GROUPS = (
    (0, C_Q, None, True, QK_SCALE),
    (C_Q, C_KV, None, True, 1.0),
    (C_Q + C_KV, C_KV, None, False, 1.0),
)
_ODD_GROUPS_CTX = tuple((c0, w, g0, False, s) for (c0, w, g0, _, s) in _ODD_GROUPS)

LAT_TM = 512


def kernel(x, c, ctx, c_ctx, ada_w, ada_b, ln_g, ln_b, ev_w_in, ev_w_out, ev_q_gain, ev_k_gain, ev_rpb,
           od_w_in, od_w_out, od_sink, ffn_w_up, ffn_conv_w, ffn_conv_b, ffn_w_down):
    n = x.shape[1]
    nc = ctx.shape[1]
    x_lat = x[0]
    x_ctx = ctx[0]

    cvec = jnp.zeros((8, D_MODEL), F32).at[0].set(c[0]).at[1].set(c_ctx)
    mods = _modulation(cvec, ada_w, ada_b)

    cos, sin = _rope_tables(n)
    cos_c, sin_c = cos[:nc], sin[:nc]
    lane = np.arange(LANES)
    gmat = jnp.asarray((lane[:, None] // HEAD_DIM) == (lane[None, :] // HEAD_DIM), BF16)
    no_sink = jnp.full((C_HEADS,), NEG, F32)

    for l in range(DEPTH):
        i = l // 2
        ctx_out = l < DEPTH - 1
        m_lat = mods[l, 0].reshape(6, D_MODEL)
        m_ctx = mods[l, 1].reshape(6, D_MODEL)
        g0, b0 = ln_g[l, 0][None], ln_b[l, 0][None]
        g1, b1 = ln_g[l, 1][None], ln_b[l, 1][None]
        if l % 2 == 0:
            w_in = ev_w_in[i]
            w_in = jnp.concatenate([w_in[:, :A_Q][:, _PERM_A], w_in[:, A_Q:]], axis=1).astype(BF16)
            w_out = ev_w_out[i]
            w_out = jnp.concatenate([w_out[:A_Q][_PERM_A], w_out[A_Q:]], axis=0).astype(BF16)
            gains = jnp.concatenate([jnp.tile(ev_q_gain[i], A_HEADS), jnp.tile(ev_k_gain[i], A_KV_HEADS)])[None]
            qa, ka, va, qb, kb, vb = _project(x_lat, m_lat, w_in, gains, cos, sin, gmat, _EVEN_GROUPS, LAT_TM)
            qa_c, ka_c, va_c, qb_c, kb_c, vb_c = _project(x_ctx, m_ctx, w_in, gains, cos_c, sin_c, gmat,
                                                           _EVEN_GROUPS_CTX, nc)
            ya = _global_attention(qa, ka, va, ka_c, va_c)
            yb = _nbr_attention(qb, kb, vb, kb_c, vb_c, _nbr_bias_tiles(ev_rpb[i] * LOG2E))
            x_lat = _out_project(x_lat, ya, yb, 0, w_out, m_lat, g0, b0, LAT_TM)
            if ctx_out:
                ya_c = _ctx_attention(qa_c, ka_c, va_c, no_sink, 0)
                yb_c = _ctx_attention(qb_c, kb_c, vb_c, no_sink, 1)
                x_ctx = _out_project(x_ctx, ya_c, yb_c, 0, w_out, m_ctx, g0, b0, nc)
        else:
            w_in = od_w_in[i]
            w_in = jnp.concatenate([w_in[:, :C_Q][:, _PERM_C], w_in[:, C_Q:]], axis=1).astype(BF16)
            w_out = od_w_out[i][_PERM_C].astype(BF16)
            sink = od_sink[i] * LOG2E
            gains = jnp.ones((1, A_Q + A_KV), F32)
            q, k, v = _project(x_lat, m_lat, w_in, gains, cos, sin, gmat, _ODD_GROUPS, LAT_TM)
            q_c, k_c, v_c = _project(x_ctx, m_ctx, w_in, gains, cos_c, sin_c, gmat, _ODD_GROUPS_CTX, nc)
            y = _window_attention(q, k, v, k_c, v_c, sink)
            x_lat = _out_project(x_lat, y, y, 1, w_out, m_lat, g0, b0, LAT_TM)
            if ctx_out:
                y_c = _ctx_attention(q_c, k_c, v_c, sink, 0)
                x_ctx = _out_project(x_ctx, y_c, y_c, 1, w_out, m_ctx, g0, b0, nc)
        w_up = ffn_w_up[l].astype(BF16)
        w_down = ffn_w_down[l].astype(BF16)
        cb = ffn_conv_b[l][None]
        x_lat = _conv_ffn(x_lat, m_lat, w_up, ffn_conv_w[l], cb, w_down, g1, b1, LAT_TM)
        if ctx_out:
            x_ctx = _conv_ffn(x_ctx, m_ctx, w_up, ffn_conv_w[l], cb, w_down, g1, b1, nc)
    return x_lat[None]
```

```python
import functools

import numpy as np
import jax
import jax.numpy as jnp
from jax import lax
from jax.experimental import pallas as pl
from jax.experimental.pallas import tpu as pltpu

D_MODEL = 1024
DEPTH = 4
GRID_W = 64
HEAD_DIM = 64
A_HEADS = 8
A_KV_HEADS = 2
B_HEADS = 8
NB_KH = 8
NB_KW = 16
C_HEADS = 16
C_KV_HEADS = 2
C_WINDOW = 128
D_FF = 2816
ROPE_THETA = 10000.0
LN_EPS = 1e-5
RMS_EPS = 1e-6
NEG = -1e30
BIG = 1e30
DN_ALPHA = (2 * DEPTH) ** 0.25
A_Q = A_HEADS * HEAD_DIM
A_KV = A_KV_HEADS * HEAD_DIM
B_W = B_HEADS * HEAD_DIM
C_Q = C_HEADS * HEAD_DIM
C_KV = C_KV_HEADS * HEAD_DIM
LOG2E = 1.4426950408889634
QK_SCALE = HEAD_DIM ** -0.5 * LOG2E
DEN_HIGH_LOG2 = 100.0
DEN_LOW_LOG2 = 64.0

LANES = 128
SUBLANES = 8
HALO = SUBLANES
VMEM_LIMIT = 52 * 1024 * 1024

BF16 = jnp.bfloat16
F32 = jnp.float32

_NT = (((1,), (1,)), ((), ()))


def _params(sem):
    return pltpu.CompilerParams(dimension_semantics=sem, vmem_limit_bytes=VMEM_LIMIT)


def _full(shape):
    return pl.BlockSpec(shape, lambda *_: (0,) * len(shape))


def _lane_lo(shape):
    return lax.broadcasted_iota(jnp.int32, shape, len(shape) - 1) < HEAD_DIM


def _unit_lane(half):
    return HEAD_DIM if half == 0 else 0


def _unit_lane_variants(x):
    lane = jnp.arange(x.shape[-1]) % LANES
    one = jnp.ones((), x.dtype)
    zero = jnp.zeros((), x.dtype)
    lo = jnp.where(lane < HEAD_DIM, x, jnp.where(lane == HEAD_DIM, one, zero))
    hi = jnp.where(lane >= HEAD_DIM, x, jnp.where(lane == 0, one, zero))
    return jnp.stack([lo, hi])


def _fold_denominators(high, low, den):
    for t in range(den.shape[1] // LANES):
        piece = den[:, t * LANES:(t + 1) * LANES]
        high = jnp.maximum(high, piece)
        low = jnp.minimum(low, piece)
    return high, low


def _store_flags(flag_ref, high, low):
    flag_ref[0:SUBLANES, :] = jnp.broadcast_to(high, (SUBLANES, LANES))
    flag_ref[SUBLANES:2 * SUBLANES, :] = jnp.broadcast_to(low, (SUBLANES, LANES))


def _outside_safe_range(flags):
    f = flags.reshape(-1, 2, SUBLANES, LANES)
    return jnp.logical_not((jnp.max(f[:, 0]) < 2.0 ** DEN_HIGH_LOG2) & (jnp.min(f[:, 1]) > 2.0 ** -DEN_LOW_LOG2))


def _mod_kernel(c_ref, w_ref, b_ref, o_ref):
    s = c_ref[...]
    s = s * jax.nn.sigmoid(s)
    o_ref[0] = jnp.dot(s, w_ref[0], preferred_element_type=F32, precision=lax.Precision.HIGHEST) + b_ref[0]


def _modulation(cvec, ada_w, ada_b):
    nb = 6
    return pl.pallas_call(
        _mod_kernel,
        grid=(DEPTH, nb),
        in_specs=[pl.BlockSpec((8, D_MODEL), lambda l, j: (0, 0)),
                  pl.BlockSpec((1, D_MODEL, D_MODEL), lambda l, j: (l, 0, j)),
                  pl.BlockSpec((1, 1, D_MODEL), lambda l, j: (l, 0, j))],
        out_specs=pl.BlockSpec((1, 8, D_MODEL), lambda l, j: (l, 0, j)),
        out_shape=jax.ShapeDtypeStruct((DEPTH, 8, 6 * D_MODEL), F32),
        compiler_params=_params(("arbitrary", "arbitrary")),
        name="modulation",
    )(cvec, ada_w, ada_b.reshape(DEPTH, 1, 6 * D_MODEL))


def _group_sumsq(z, gmat):
    x2 = z * z
    hi = x2.astype(BF16)
    lo = (x2 - hi.astype(F32)).astype(BF16)
    return (jnp.dot(hi, gmat, preferred_element_type=F32) + jnp.dot(lo, gmat, preferred_element_type=F32))


VT_ROWS = 80


def _proj_kernel(groups, x_ref, mod_ref, w_ref, gain_ref, cos_ref, sin_ref, gmat_ref, *out_refs):
    shift = mod_ref[0:1, :]
    scale = mod_ref[1:2, :]
    h = (x_ref[...] * (1.0 + scale) + shift).astype(BF16)
    tm = h.shape[0]
    cos = cos_ref[...]
    sin = sin_ref[...]
    gmat = gmat_ref[...]
    first = lax.broadcasted_iota(jnp.int32, cos.shape, 1) % 32 < 16
    tail_row = lax.broadcasted_iota(jnp.int32, (VT_ROWS - HEAD_DIM, LANES), 0)
    tail = jnp.where(tail_row == 0, 1.0, 0.0).astype(BF16)
    outs = iter(out_refs)
    for (c0, width, g0, rope, qscale, transposed_copy) in groups:
        o_ref = next(outs)
        t_ref = next(outs) if transposed_copy else None
        z = jnp.dot(h, w_ref[:, c0:c0 + width], preferred_element_type=F32)
        for b in range(width // LANES):
            zb = z[:, b * LANES:(b + 1) * LANES]
            if g0 is not None:
                ms = _group_sumsq(zb, gmat) * (1.0 / HEAD_DIM)
                zb = zb * lax.rsqrt(ms + RMS_EPS) * gain_ref[:, g0 + b * LANES:g0 + (b + 1) * LANES]
            if rope:
                rot = jnp.where(first, pltpu.roll(zb, LANES - 16, 1), pltpu.roll(zb, 16, 1))
                zb = zb * cos + rot * sin
            if qscale != 1.0:
                zb = zb * qscale
            o_ref[:, b * LANES:(b + 1) * LANES] = zb.astype(BF16)
            if transposed_copy:
                zt = zb.T.astype(BF16)
                for half in range(2):
                    for t in range(tm // LANES):
                        t_ref[b, half, t, 0:HEAD_DIM, :] = zt[half * HEAD_DIM:(half + 1) * HEAD_DIM,
                                                              t * LANES:(t + 1) * LANES]
                        t_ref[b, half, t, HEAD_DIM:VT_ROWS, :] = tail


def _project(x, mod, w, gains, cos, sin, gmat, groups, tm):
    n = x.shape[0]
    win = w.shape[1]
    out_shape, out_specs = [], []
    for g in groups:
        out_shape.append(jax.ShapeDtypeStruct((n, g[1]), BF16))
        out_specs.append(pl.BlockSpec((tm, g[1]), lambda i: (i, 0)))
        if g[5]:
            nb = g[1] // LANES
            out_shape.append(jax.ShapeDtypeStruct((nb, 2, n // LANES, VT_ROWS, LANES), BF16))
            out_specs.append(pl.BlockSpec((nb, 2, tm // LANES, VT_ROWS, LANES), lambda i: (0, 0, i, 0, 0)))
    return pl.pallas_call(
        functools.partial(_proj_kernel, groups),
        grid=(n // tm,),
        in_specs=[pl.BlockSpec((tm, D_MODEL), lambda i: (i, 0)),
                  _full((6, D_MODEL)),
                  _full((D_MODEL, win)),
                  _full(gains.shape),
                  pl.BlockSpec((tm, LANES), lambda i: (i, 0)),
                  pl.BlockSpec((tm, LANES), lambda i: (i, 0)),
                  _full((LANES, LANES))],
        out_specs=out_specs,
        out_shape=out_shape,
        compiler_params=_params(("parallel",)),
        name="qkv_project",
    )(x, mod, w, gains, cos, sin, gmat)


GA_TQ = 1024
GA_TK = 2048


GA_KEY_CHUNK = 1024
GA_QUERY_CHUNK = 512


def _vt_window(vt_ref, lead, first_chunk, nchunks):
    return jnp.concatenate([vt_ref[lead + (first_chunk + t,)] for t in range(nchunks)], axis=1)


def _global_attn_kernel(q_ref, kc_ref, vct_ref, k_ref, vt_ref, o_ref, flag_ref, qx_ref, acc_ref):
    kk = pl.program_id(1)
    tq = q_ref.shape[0]
    nblk = q_ref.shape[1] // LANES

    def update(kb_ref, vtb_ref):
        nkeys = kb_ref.shape[0]
        kr = min(GA_KEY_CHUNK, nkeys)
        for j in range(nblk):
            for c0 in range(0, tq, GA_QUERY_CHUNK):
                cs = slice(c0, c0 + GA_QUERY_CHUNK)
                part = [None, None]
                for r0 in range(0, nkeys, kr):
                    st = [lax.dot_general(kb_ref[r0:r0 + kr, :], qx_ref[2 * j + half, cs, :], _NT,
                                          preferred_element_type=F32) for half in range(2)]
                    for half in range(2):
                        pv = jnp.dot(_vt_window(vtb_ref, (half,), r0 // LANES, kr // LANES),
                                     jnp.exp2(st[half]).astype(BF16), preferred_element_type=F32)
                        part[half] = pv if part[half] is None else part[half] + pv
                for half in range(2):
                    acc_ref[2 * j + half, :, cs] += part[half]

    @pl.when(kk == 0)
    def _():
        lo = _lane_lo((tq, LANES))
        for j in range(nblk):
            qj = q_ref[:, j * LANES:(j + 1) * LANES]
            zero = jnp.zeros_like(qj)
            qx_ref[2 * j] = jnp.where(lo, qj, zero)
            qx_ref[2 * j + 1] = jnp.where(lo, zero, qj)
        acc_ref[...] = jnp.zeros(acc_ref.shape, F32)
        update(kc_ref, vct_ref)

    update(k_ref, vt_ref)

    @pl.when(kk == pl.num_programs(1) - 1)
    def _():
        high = jnp.zeros((1, LANES), F32)
        low = jnp.full((1, LANES), BIG, F32)
        for j in range(nblk):
            halves = []
            for half in range(2):
                a = acc_ref[2 * j + half]
                den = a[HEAD_DIM:HEAD_DIM + 1, :]
                high, low = _fold_denominators(high, low, den)
                halves.append(a[0:HEAD_DIM, :] / den)
            o_ref[:, j * LANES:(j + 1) * LANES] = jnp.concatenate(halves, axis=0).T.astype(o_ref.dtype)
        _store_flags(flag_ref, high, low)


def _global_attn_exact_kernel(q_ref, kc_ref, vc_ref, k_ref, v_ref, o_ref, qx_ref, c_ref, acc_ref):
    kk = pl.program_id(1)
    tq = q_ref.shape[0]
    nblk = q_ref.shape[1] // LANES
    nh = 2 * nblk
    lane = lax.broadcasted_iota(jnp.int32, (tq, LANES), 1)

    def moving_reference_update(kx_ref, vx_ref, first):
        for h in range(nh):
            half = h % 2
            unit = _unit_lane(half)
            sp = lax.dot_general(qx_ref[h], kx_ref[half], _NT, preferred_element_type=F32)
            mb = jnp.max(sp, axis=-1, keepdims=True)
            c_old = c_ref[h]
            target = mb if first else jnp.maximum(c_old, c_old + mb)
            qx_ref[h] = jnp.where(lane == unit, -target, qx_ref[h].astype(F32)).astype(BF16)
            c_new = -qx_ref[h][:, unit:unit + 1].astype(F32)
            d = c_new - c_old
            pv = jnp.dot(jnp.exp2(sp - d).astype(BF16), vx_ref[half], preferred_element_type=F32)
            acc_ref[h] = pv if first else acc_ref[h] * jnp.exp2(-d) + pv
            c_ref[h] = c_new

    @pl.when(kk == 0)
    def _():
        lo = _lane_lo((tq, LANES))
        for j in range(nblk):
            qj = q_ref[:, j * LANES:(j + 1) * LANES]
            zero = jnp.zeros_like(qj)
            qx_ref[2 * j] = jnp.where(lo, qj, zero)
            qx_ref[2 * j + 1] = jnp.where(lo, zero, qj)
        c_ref[...] = jnp.zeros(c_ref.shape, F32)
        moving_reference_update(kc_ref, vc_ref, True)

    moving_reference_update(k_ref, v_ref, False)

    @pl.when(kk == pl.num_programs(1) - 1)
    def _():
        lo = _lane_lo((tq, LANES))
        for j in range(nblk):
            a_lo = acc_ref[2 * j]
            a_hi = acc_ref[2 * j + 1]
            o_lo = a_lo / a_lo[:, HEAD_DIM:HEAD_DIM + 1]
            o_hi = a_hi / a_hi[:, 0:1]
            o_ref[:, j * LANES:(j + 1) * LANES] = jnp.where(lo, o_lo, o_hi).astype(o_ref.dtype)


def _global_attention(q, k, v, vt, kc, vc, vct):
    n, qw = q.shape
    nc = kc.shape[0]
    nh = 2 * (qw // LANES)
    nq = n // GA_TQ
    grid = (nq, n // GA_TK)
    q_spec = pl.BlockSpec((GA_TQ, qw), lambda i, j: (i, 0))

    y, flags = pl.pallas_call(
        _global_attn_kernel,
        grid=grid,
        in_specs=[q_spec,
                  _full((nc, LANES)), _full((2, nc // LANES, VT_ROWS, LANES)),
                  pl.BlockSpec((GA_TK, LANES), lambda i, j: (j, 0)),
                  pl.BlockSpec((2, GA_TK // LANES, VT_ROWS, LANES), lambda i, j: (0, j, 0, 0))],
        out_specs=[q_spec, pl.BlockSpec((2 * SUBLANES, LANES), lambda i, j: (i, 0))],
        out_shape=[jax.ShapeDtypeStruct((n, qw), BF16),
                   jax.ShapeDtypeStruct((nq * 2 * SUBLANES, LANES), F32)],
        scratch_shapes=[pltpu.VMEM((nh, GA_TQ, LANES), BF16),
                        pltpu.VMEM((nh, VT_ROWS, GA_TQ), F32)],
        compiler_params=_params(("parallel", "arbitrary")),
        name="global_attention",
    )(q, kc, vct[0], k, vt[0])

    def exact():
        return pl.pallas_call(
            _global_attn_exact_kernel,
            grid=grid,
            in_specs=[q_spec,
                      _full((2, nc, LANES)), _full((2, nc, LANES)),
                      pl.BlockSpec((2, GA_TK, LANES), lambda i, j: (0, j, 0)),
                      pl.BlockSpec((2, GA_TK, LANES), lambda i, j: (0, j, 0))],
            out_specs=q_spec,
            out_shape=jax.ShapeDtypeStruct((n, qw), BF16),
            scratch_shapes=[pltpu.VMEM((nh, GA_TQ, LANES), BF16),
                            pltpu.VMEM((nh, GA_TQ, 1), F32),
                            pltpu.VMEM((nh, GA_TQ, LANES), F32)],
            compiler_params=_params(("parallel", "arbitrary")),
            name="global_attention_exact",
        )(q, _unit_lane_variants(kc), _unit_lane_variants(vc), _unit_lane_variants(k), _unit_lane_variants(v))

    return lax.cond(_outside_safe_range(flags), exact, lambda: y)


NB_QROWS = 8
NB_KROWS = NB_QROWS + NB_KH
NB_MASKED = 2 * NB_KH - 1
NB_TILES = 3 * (NB_MASKED + 1)


def _nbr_bias_tiles(rpb):
    nh = rpb.shape[0]
    col = np.arange(GRID_W)
    cs = np.clip(col - NB_KW // 2, 0, GRID_W - NB_KW)
    col_ok = (col[:, None] >= cs[None, :]) & (col[:, None] < cs[None, :] + NB_KW)
    dcol = col[:, None] - col[None, :] + NB_KW - 1
    sel_col = (dcol[:, :, None] == np.arange(2 * NB_KW - 1)) & col_ok[:, :, None]
    base = jnp.einsum('kqj,hdj->hdkq', jnp.asarray(sel_col, F32), rpb, precision=lax.Precision.HIGHEST)
    base = jnp.where(jnp.asarray(col_ok), base, NEG)
    masked = jnp.full((nh, 1, GRID_W, GRID_W), NEG, F32)
    base = jnp.concatenate([base, masked], axis=1)
    prev = jnp.concatenate([masked, base[:, :-1]], axis=1)
    allmasked = jnp.full_like(base, NEG)
    return jnp.concatenate([jnp.concatenate([base, prev], -1),
                            jnp.concatenate([base, allmasked], -1),
                            jnp.concatenate([allmasked, base], -1)], axis=1)


def _nbr_tile_index(rows):
    nq = rows // NB_QROWS
    idx = np.zeros((3, NB_KROWS, NB_QROWS // 2), np.int64)
    for v, blk in enumerate((0, 1, nq - 1)):
        r0 = blk * NB_QROWS
        k0 = int(np.clip(r0 - NB_KH // 2, 0, rows - NB_KROWS))
        for kr in range(NB_KROWS):
            ka = k0 + kr

            def code(qr):
                qa = r0 + qr
                rs = int(np.clip(qa - NB_KH // 2, 0, rows - NB_KH))
                return ka - qa + NB_KH - 1 if rs <= ka < rs + NB_KH else NB_MASKED

            for qp in range(NB_QROWS // 2):
                left, right = code(2 * qp), code(2 * qp + 1)
                if left < NB_MASKED and right < NB_MASKED:
                    assert right == left - 1
                    idx[v, kr, qp] = left
                elif left < NB_MASKED:
                    idx[v, kr, qp] = (NB_MASKED + 1) + left
                else:
                    idx[v, kr, qp] = 2 * (NB_MASKED + 1) + right
    return tuple(tuple(tuple(int(t) for t in row) for row in var) for var in idx)


def _nbr_attn_kernel(exact, tile_index, q_ref, k_ref, vt_ref, kc_ref, vct_ref, tiles_ref, o_ref, flag_ref,
                     bias_ref):
    i = pl.program_id(1)
    nq = pl.num_programs(1)
    rows = k_ref.shape[0] // GRID_W
    span = NB_KROWS * GRID_W

    def fill_bias(variant):
        for half in range(2):
            for kr in range(NB_KROWS):
                for qp in range(NB_QROWS // 2):
                    bias_ref[half, kr * GRID_W:(kr + 1) * GRID_W, qp * LANES:(qp + 1) * LANES] = (
                        tiles_ref[half, tile_index[variant][kr][qp]])

    for variant, at in enumerate((0, 1, nq - 1)):
        pl.when(i == at)(functools.partial(fill_bias, variant))

    r0 = i * NB_QROWS
    ks = pl.multiple_of(jnp.clip(r0 - NB_KH // 2, 0, rows - NB_KROWS) * GRID_W, 2 * LANES)
    kw = k_ref[pl.ds(ks, span), :]
    kc = kc_ref[...]
    q = q_ref[...]
    tq = q.shape[0]
    lo = _lane_lo(q.shape)
    zero = jnp.zeros_like(q)
    high = jnp.zeros((1, LANES), F32)
    low = jnp.full((1, LANES), BIG, F32)
    halves = []
    logits = []
    for half in range(2):
        qm = jnp.where(lo, q, zero) if half == 0 else jnp.where(lo, zero, q)
        logits.append((lax.dot_general(kw, qm, _NT, preferred_element_type=F32),
                       lax.dot_general(kc, qm, _NT, preferred_element_type=F32)))
    for half in range(2):
        s_nb = logits[half][0] + bias_ref[half]
        s_cx = logits[half][1]
        if exact:
            m = jnp.maximum(jnp.max(s_nb, axis=0, keepdims=True), jnp.max(s_cx, axis=0, keepdims=True))
            s_nb = s_nb - m
            s_cx = s_cx - m
        vtw = _vt_window(vt_ref, (0, half), ks // LANES, span // LANES)
        vtc = _vt_window(vct_ref, (0, half), 0, kc.shape[0] // LANES)
        ot = (jnp.dot(vtw, jnp.exp2(s_nb).astype(BF16), preferred_element_type=F32)
              + jnp.dot(vtc, jnp.exp2(s_cx).astype(BF16), preferred_element_type=F32))
        den = ot[HEAD_DIM:HEAD_DIM + 1, :]
        high, low = _fold_denominators(high, low, den)
        halves.append(ot[0:HEAD_DIM, :] / den)
    o_ref[...] = jnp.concatenate(halves, axis=0).T.astype(o_ref.dtype)
    _store_flags(flag_ref, high, low)


def _nbr_attention(q, k, vt, kc, vct, tiles):
    n, qw = q.shape
    nc = kc.shape[0]
    tq = NB_QROWS * GRID_W
    nq = n // tq
    assert nq >= 3
    npair = qw // LANES
    tile_index = _nbr_tile_index(n // GRID_W)
    operands = (q, k, vt, kc, vct, tiles)

    def run(exact):
        return pl.pallas_call(
            functools.partial(_nbr_attn_kernel, exact, tile_index),
            grid=(npair, nq),
            in_specs=[pl.BlockSpec((tq, LANES), lambda p, i: (i, p)),
                      pl.BlockSpec((n, LANES), lambda p, i: (0, p)),
                      pl.BlockSpec((1, 2, n // LANES, VT_ROWS, LANES), lambda p, i: (p, 0, 0, 0, 0)),
                      pl.BlockSpec((nc, LANES), lambda p, i: (0, p)),
                      pl.BlockSpec((1, 2, nc // LANES, VT_ROWS, LANES), lambda p, i: (p, 0, 0, 0, 0)),
                      pl.BlockSpec((2, NB_TILES, GRID_W, LANES), lambda p, i: (p, 0, 0, 0))],
            out_specs=[pl.BlockSpec((tq, LANES), lambda p, i: (i, p)),
                       pl.BlockSpec((2 * SUBLANES, LANES), lambda p, i: (p * nq + i, 0))],
            out_shape=[jax.ShapeDtypeStruct((n, qw), BF16),
                       jax.ShapeDtypeStruct((npair * nq * 2 * SUBLANES, LANES), F32)],
            scratch_shapes=[pltpu.VMEM((2, NB_KROWS * GRID_W, tq), F32)],
            compiler_params=_params(("parallel", "arbitrary")),
            name="neighbourhood_attention_exact" if exact else "neighbourhood_attention",
        )(*operands)

    y, flags = run(False)
    return lax.cond(_outside_safe_range(flags), lambda: run(True)[0], lambda: y)


WIN_TQ = 256
WIN_SPAN = WIN_TQ + 2 * C_WINDOW


def _window_attn_kernel(exact, sink_ref, q_ref, k_ref, vt_ref, kc_ref, vct_ref, o_ref, flag_ref):
    i = pl.program_id(0)
    n = k_ref.shape[0]
    tq = q_ref.shape[0]
    nblk = q_ref.shape[1] // LANES
    nchunk = WIN_SPAN // LANES
    q0 = i * tq
    ks = pl.multiple_of(jnp.clip(q0 - C_WINDOW, 0, n - WIN_SPAN), C_WINDOW)
    kw = k_ref[pl.ds(ks, WIN_SPAN), :]
    kc = kc_ref[...]
    kpos = ks + lax.broadcasted_iota(jnp.int32, (WIN_SPAN, tq), 0)
    qpos = q0 + lax.broadcasted_iota(jnp.int32, (WIN_SPAN, tq), 1)
    valid = jnp.abs(kpos - qpos) <= C_WINDOW
    lo = _lane_lo((tq, LANES))
    high = jnp.zeros((1, LANES), F32)
    low = jnp.full((1, LANES), BIG, F32)
    for j in range(nblk):
        q = q_ref[:, j * LANES:(j + 1) * LANES]
        zero = jnp.zeros_like(q)
        halves = []
        logits = []
        for half in range(2):
            qm = jnp.where(lo, q, zero) if half == 0 else jnp.where(lo, zero, q)
            logits.append((lax.dot_general(kw, qm, _NT, preferred_element_type=F32),
                           lax.dot_general(kc, qm, _NT, preferred_element_type=F32)))
        for half in range(2):
            sink = jnp.full((1, tq), sink_ref[j + nblk * half], F32)
            s_w = jnp.where(valid, logits[half][0], NEG)
            s_c = logits[half][1]
            if exact:
                m = jnp.maximum(jnp.maximum(jnp.max(s_w, axis=0, keepdims=True),
                                            jnp.max(s_c, axis=0, keepdims=True)), sink)
                s_w = s_w - m
                s_c = s_c - m
                sink = sink - m
            vtw = _vt_window(vt_ref, (half,), ks // LANES, nchunk)
            vtc = _vt_window(vct_ref, (half,), 0, kc.shape[0] // LANES)
            ot = (jnp.dot(vtw, jnp.exp2(s_w).astype(BF16), preferred_element_type=F32)
                  + jnp.dot(vtc, jnp.exp2(s_c).astype(BF16), preferred_element_type=F32))
            den = ot[HEAD_DIM:HEAD_DIM + 1, :] + jnp.exp2(sink)
            high, low = _fold_denominators(high, low, den)
            halves.append(ot[0:HEAD_DIM, :] / den)
        o_ref[:, j * LANES:(j + 1) * LANES] = jnp.concatenate(halves, axis=0).T.astype(o_ref.dtype)
    _store_flags(flag_ref, high, low)


def _window_attention(q, k, vt, kc, vct, sink):
    n, qw = q.shape
    nc = kc.shape[0]
    nq = n // WIN_TQ
    operands = (sink, q, k, vt[0], kc, vct[0])

    def run(exact):
        return pl.pallas_call(
            functools.partial(_window_attn_kernel, exact),
            grid=(nq,),
            in_specs=[pl.BlockSpec(memory_space=pltpu.SMEM),
                      pl.BlockSpec((WIN_TQ, qw), lambda i: (i, 0)),
                      _full((n, LANES)), _full((2, n // LANES, VT_ROWS, LANES)),
                      _full((nc, LANES)), _full((2, nc // LANES, VT_ROWS, LANES))],
            out_specs=[pl.BlockSpec((WIN_TQ, qw), lambda i: (i, 0)),
                       pl.BlockSpec((2 * SUBLANES, LANES), lambda i: (i, 0))],
            out_shape=[jax.ShapeDtypeStruct((n, qw), BF16),
                       jax.ShapeDtypeStruct((nq * 2 * SUBLANES, LANES), F32)],
            compiler_params=_params(("parallel",)),
            name="window_attention_exact" if exact else "window_attention",
        )(*operands)

    y, flags = run(False)
    return lax.cond(_outside_safe_range(flags), lambda: run(True)[0], lambda: y)


def _ctx_attn_kernel(sink_ref, q_ref, k_ref, v_ref, o_ref):
    j = pl.program_id(0)
    nblk = pl.num_programs(0)
    q = q_ref[...]
    k = k_ref[...]
    v = v_ref[...]
    lo = _lane_lo(q.shape)
    zero = jnp.zeros_like(q)
    outs = []
    for half in range(2):
        qm = jnp.where(lo, q, zero) if half == 0 else jnp.where(lo, zero, q)
        sink = sink_ref[j + nblk * half]
        s = lax.dot_general(qm, k, _NT, preferred_element_type=F32)
        m = jnp.maximum(jnp.max(s, axis=-1, keepdims=True), sink)
        p = jnp.exp2(s - m)
        den = jnp.sum(p, axis=-1, keepdims=True) + jnp.exp2(sink - m)
        outs.append(jnp.dot(p.astype(BF16), v, preferred_element_type=F32) / den)
    o_ref[...] = jnp.where(lo, outs[0], outs[1]).astype(o_ref.dtype)


def _ctx_attention(q, k, v, sink, kv_per_block):
    c, qw = q.shape
    nblk = qw // LANES
    return pl.pallas_call(
        _ctx_attn_kernel,
        grid=(nblk,),
        in_specs=[pl.BlockSpec(memory_space=pltpu.SMEM),
                  pl.BlockSpec((c, LANES), lambda j: (0, j)),
                  pl.BlockSpec((c, LANES), lambda j: (0, j * kv_per_block)),
                  pl.BlockSpec((c, LANES), lambda j: (0, j * kv_per_block))],
        out_specs=pl.BlockSpec((c, LANES), lambda j: (0, j)),
        out_shape=jax.ShapeDtypeStruct((c, qw), BF16),
        compiler_params=_params(("parallel",)),
        name="context_attention",
    )(sink, q, k, v)


def _residual_layer_norm(x, f, gate, g, b):
    z = x + (gate * (1.0 / DN_ALPHA)) * f
    mu = jnp.mean(z, axis=-1, keepdims=True)
    zc = z - mu
    var = jnp.mean(zc * zc, axis=-1, keepdims=True)
    return zc * lax.rsqrt(var + LN_EPS / (DN_ALPHA * DN_ALPHA)) * g + b


def _outproj_kernel(x_ref, y1_ref, y2_ref, w1_ref, w2_ref, mod_ref, g_ref, b_ref, o_ref):
    rows = x_ref.shape[0] // 2
    for s in range(2):
        rs = slice(s * rows, (s + 1) * rows)
        f = (jnp.dot(y1_ref[rs, :], w1_ref[...], preferred_element_type=F32)
             + jnp.dot(y2_ref[rs, :], w2_ref[...], preferred_element_type=F32))
        o_ref[rs, :] = _residual_layer_norm(x_ref[rs, :], f, mod_ref[2:3, :], g_ref[...], b_ref[...])


def _out_project(x, y1, y2, c2, w, mod, g, b, tm):
    n = x.shape[0]
    half = D_MODEL // 2
    return pl.pallas_call(
        _outproj_kernel,
        grid=(n // tm,),
        in_specs=[pl.BlockSpec((tm, D_MODEL), lambda i: (i, 0)),
                  pl.BlockSpec((tm, half), lambda i: (i, 0)),
                  pl.BlockSpec((tm, half), lambda i: (i, c2)),
                  pl.BlockSpec((half, D_MODEL), lambda i: (0, 0)),
                  pl.BlockSpec((half, D_MODEL), lambda i: (1, 0)),
                  _full((6, D_MODEL)), _full((1, D_MODEL)), _full((1, D_MODEL))],
        out_specs=pl.BlockSpec((tm, D_MODEL), lambda i: (i, 0)),
        out_shape=jax.ShapeDtypeStruct((n, D_MODEL), F32),
        compiler_params=_params(("parallel",)),
        name="out_project_ln",
    )(x, y1, y2, w, w, mod, g, b)


FFN_CHUNKS = 2
FFN_FC = D_FF // FFN_CHUNKS


FFN_CHAINS = 1


def _ffn_kernel(x_ref, xp_ref, xn_ref, mod_ref, wu_ref, cw_ref, cb_ref, wd_ref, g_ref, b_ref, o_ref,
                h_ref, ua_ref, ug_ref):
    i = pl.program_id(0)
    tm = x_ref.shape[0]
    rows = tm // FFN_CHAINS
    shift = mod_ref[3:4, :]
    scale = 1.0 + mod_ref[4:5, :]
    hp = jnp.where(i > 0, xp_ref[...] * scale + shift, 0.0)
    hn = jnp.where(i < pl.num_programs(0) - 1, xn_ref[...] * scale + shift, 0.0)
    hm = x_ref[...] * scale + shift
    h_ref[...] = jnp.concatenate([hp, hm, hn], axis=0).astype(BF16)

    def conv(u_ref, col0):
        cols = slice(col0, col0 + FFN_FC)
        return (cb_ref[:, cols]
                + u_ref[HALO - 1:HALO - 1 + rows, :] * cw_ref[0:1, cols]
                + u_ref[HALO:HALO + rows, :] * cw_ref[1:2, cols]
                + u_ref[HALO + 1:HALO + 1 + rows, :] * cw_ref[2:3, cols])

    for s in range(FFN_CHAINS):
        r0 = s * rows
        hs = h_ref[r0:r0 + rows + 2 * HALO, :]
        f = None
        for c in range(FFN_CHUNKS):
            a0 = c * FFN_FC
            g0 = D_FF + c * FFN_FC
            ua = ua_ref.at[s, c]
            ug = ug_ref.at[s, c]
            ua[...] = jnp.dot(hs, wu_ref[:, a0:a0 + FFN_FC], preferred_element_type=F32)
            ug[...] = jnp.dot(hs, wu_ref[:, g0:g0 + FFN_FC], preferred_element_type=F32)
            gt = conv(ug, g0)
            act = (gt * jax.nn.sigmoid(gt) * conv(ua, a0)).astype(BF16)
            part = jnp.dot(act, wd_ref[a0:a0 + FFN_FC, :], preferred_element_type=F32)
            f = part if f is None else f + part
        o_ref[r0:r0 + rows, :] = _residual_layer_norm(x_ref[r0:r0 + rows, :], f, mod_ref[5:6, :],
                                                      g_ref[...], b_ref[...])


def _conv_ffn(x, mod, w_up, conv_w, conv_b, w_down, g, b, tm):
    n = x.shape[0]
    nh = n // HALO
    per = tm // HALO
    rows = tm // FFN_CHAINS
    resident = pl.Buffered(1)
    return pl.pallas_call(
        _ffn_kernel,
        grid=(n // tm,),
        in_specs=[pl.BlockSpec((tm, D_MODEL), lambda i: (i, 0)),
                  pl.BlockSpec((HALO, D_MODEL), lambda i: (jnp.maximum(i * per - 1, 0), 0)),
                  pl.BlockSpec((HALO, D_MODEL), lambda i: (jnp.minimum((i + 1) * per, nh - 1), 0)),
                  _full((6, D_MODEL)),
                  pl.BlockSpec((D_MODEL, 2 * D_FF), lambda i: (0, 0), pipeline_mode=resident),
                  _full((3, 2 * D_FF)), _full((1, 2 * D_FF)),
                  pl.BlockSpec((D_FF, D_MODEL), lambda i: (0, 0), pipeline_mode=resident),
                  _full((1, D_MODEL)), _full((1, D_MODEL))],
        out_specs=pl.BlockSpec((tm, D_MODEL), lambda i: (i, 0)),
        out_shape=jax.ShapeDtypeStruct((n, D_MODEL), F32),
        scratch_shapes=[pltpu.VMEM((tm + 2 * HALO, D_MODEL), BF16),
                        pltpu.VMEM((FFN_CHAINS, FFN_CHUNKS, rows + 2 * HALO, FFN_FC), F32),
                        pltpu.VMEM((FFN_CHAINS, FFN_CHUNKS, rows + 2 * HALO, FFN_FC), F32)],
        compiler_params=_params(("parallel",)),
        name="conv_ffn_ln",
    )(x, x, x, mod, w_up, conv_w, conv_b, w_down, g, b)


def _pair_perm(n_heads, kv_heads):
    per = n_heads // kv_heads
    order = []
    for j in range(per):
        for g in range(kv_heads):
            order.append(g * per + j)
    cols = np.concatenate([np.arange(h * HEAD_DIM, (h + 1) * HEAD_DIM) for h in order])
    return cols


_PERM_A = _pair_perm(A_HEADS, A_KV_HEADS)
_PERM_C = _pair_perm(C_HEADS, C_KV_HEADS)


def _rope_tables(n):
    t = np.arange(n)
    row = (t // GRID_W).astype(np.float64)
    col = (t % GRID_W).astype(np.float64)
    half = HEAD_DIM // 2
    inv = ROPE_THETA ** (-np.arange(0, half, 2, dtype=np.float64) / half)
    ang_r = row[:, None] * inv
    ang_c = col[:, None] * inv
    ang = np.concatenate([ang_r, ang_r, ang_c, ang_c], -1)
    ang = np.concatenate([ang, ang], -1)
    sign = np.where(np.arange(LANES) % 32 < 16, -1.0, 1.0)
    return jnp.asarray(np.cos(ang), F32), jnp.asarray(np.sin(ang) * sign, F32)


_EVEN_GROUPS = (
    (0, A_Q, 0, True, QK_SCALE, False),
    (A_Q, A_KV, A_Q, True, 1.0, False),
    (A_Q + A_KV, A_KV, None, False, 1.0, True),
    (A_Q + 2 * A_KV, B_W, None, False, QK_SCALE, False),
    (A_Q + 2 * A_KV + B_W, B_W, None, False, 1.0, False),
    (A_Q + 2 * A_KV + 2 * B_W, B_W, None, False, 1.0, True),
)
_EVEN_GROUPS_CTX = tuple((c0, w, g0, False, s, t) for (c0, w, g0, _, s, t) in _EVEN_GROUPS)
_ODD_GROUPS = (
    (0, C_Q, None, True, QK_SCALE, False),
    (C_Q, C_KV, None, True, 1.0, False),
    (C_Q + C_KV, C_KV, None, False, 1.0, True),
)
_ODD_GROUPS_CTX = tuple((c0, w, g0, False, s, t) for (c0, w, g0, _, s, t) in _ODD_GROUPS)

LAT_TM = 512


def kernel(x, c, ctx, c_ctx, ada_w, ada_b, ln_g, ln_b, ev_w_in, ev_w_out, ev_q_gain, ev_k_gain, ev_rpb,
           od_w_in, od_w_out, od_sink, ffn_w_up, ffn_conv_w, ffn_conv_b, ffn_w_down):
    n = x.shape[1]
    nc = ctx.shape[1]
    x_lat = x[0]
    x_ctx = ctx[0]

    cvec = jnp.zeros((8, D_MODEL), F32).at[0].set(c[0]).at[1].set(c_ctx)
    mods = _modulation(cvec, ada_w, ada_b)

    cos, sin = _rope_tables(n)
    cos_c, sin_c = cos[:nc], sin[:nc]
    lane = np.arange(LANES)
    gmat = jnp.asarray((lane[:, None] // HEAD_DIM) == (lane[None, :] // HEAD_DIM), BF16)
    no_sink = jnp.full((C_HEADS,), NEG, F32)

    for l in range(DEPTH):
        i = l // 2
        ctx_out = l < DEPTH - 1
        m_lat = mods[l, 0].reshape(6, D_MODEL)
        m_ctx = mods[l, 1].reshape(6, D_MODEL)
        g0, b0 = ln_g[l, 0][None], ln_b[l, 0][None]
        g1, b1 = ln_g[l, 1][None], ln_b[l, 1][None]
        if l % 2 == 0:
            w_in = ev_w_in[i]
            w_in = jnp.concatenate([w_in[:, :A_Q][:, _PERM_A], w_in[:, A_Q:]], axis=1).astype(BF16)
            w_out = ev_w_out[i]
            w_out = jnp.concatenate([w_out[:A_Q][_PERM_A], w_out[A_Q:]], axis=0).astype(BF16)
            gains = jnp.concatenate([jnp.tile(ev_q_gain[i], A_HEADS), jnp.tile(ev_k_gain[i], A_KV_HEADS)])[None]
            qa, ka, va, vat, qb, kb, vb, vbt = _project(x_lat, m_lat, w_in, gains, cos, sin, gmat, _EVEN_GROUPS,
                                                        LAT_TM)
            qa_c, ka_c, va_c, vat_c, qb_c, kb_c, vb_c, vbt_c = _project(x_ctx, m_ctx, w_in, gains, cos_c, sin_c,
                                                                        gmat, _EVEN_GROUPS_CTX, nc)
            ya = _global_attention(qa, ka, va, vat, ka_c, va_c, vat_c)
            yb = _nbr_attention(qb, kb, vbt, kb_c, vbt_c, _nbr_bias_tiles(ev_rpb[i] * LOG2E))
            x_lat = _out_project(x_lat, ya, yb, 0, w_out, m_lat, g0, b0, LAT_TM)
            if ctx_out:
                ya_c = _ctx_attention(qa_c, ka_c, va_c, no_sink, 0)
                yb_c = _ctx_attention(qb_c, kb_c, vb_c, no_sink, 1)
                x_ctx = _out_project(x_ctx, ya_c, yb_c, 0, w_out, m_ctx, g0, b0, nc)
        else:
            w_in = od_w_in[i]
            w_in = jnp.concatenate([w_in[:, :C_Q][:, _PERM_C], w_in[:, C_Q:]], axis=1).astype(BF16)
            w_out = od_w_out[i][_PERM_C].astype(BF16)
            sink = od_sink[i] * LOG2E
            gains = jnp.ones((1, A_Q + A_KV), F32)
            q, k, _, vt = _project(x_lat, m_lat, w_in, gains, cos, sin, gmat, _ODD_GROUPS, LAT_TM)
            q_c, k_c, v_c, vt_c = _project(x_ctx, m_ctx, w_in, gains, cos_c, sin_c, gmat, _ODD_GROUPS_CTX, nc)
            y = _window_attention(q, k, vt, k_c, vt_c, sink)
            x_lat = _out_project(x_lat, y, y, 1, w_out, m_lat, g0, b0, LAT_TM)
            if ctx_out:
                y_c = _ctx_attention(q_c, k_c, v_c, sink, 0)
                x_ctx = _out_project(x_ctx, y_c, y_c, 1, w_out, m_ctx, g0, b0, nc)
        w_up = ffn_w_up[l].astype(BF16)
        w_down = ffn_w_down[l].astype(BF16)
        cb = ffn_conv_b[l][None]
        x_lat = _conv_ffn(x_lat, m_lat, w_up, ffn_conv_w[l], cb, w_down, g1, b1, LAT_TM)
        if ctx_out:
            x_ctx = _conv_ffn(x_ctx, m_ctx, w_up, ffn_conv_w[l], cb, w_down, g1, b1, nc)
    return x_lat[None]
```

```python
import functools

import numpy as np
import jax
import jax.numpy as jnp
from jax import lax
from jax.experimental import pallas as pl
from jax.experimental.pallas import tpu as pltpu

D_MODEL = 1024
DEPTH = 4
GRID_W = 64
HEAD_DIM = 64
A_HEADS = 8
A_KV_HEADS = 2
B_HEADS = 8
NB_KH = 8
NB_KW = 16
C_HEADS = 16
C_KV_HEADS = 2
C_WINDOW = 128
D_FF = 2816
ROPE_THETA = 10000.0
LN_EPS = 1e-5
RMS_EPS = 1e-6
NEG = -1e30
BIG = 1e30
DN_ALPHA = (2 * DEPTH) ** 0.25
A_Q = A_HEADS * HEAD_DIM
A_KV = A_KV_HEADS * HEAD_DIM
B_W = B_HEADS * HEAD_DIM
C_Q = C_HEADS * HEAD_DIM
C_KV = C_KV_HEADS * HEAD_DIM
LOG2E = 1.4426950408889634
QK_SCALE = HEAD_DIM ** -0.5 * LOG2E
DEN_HIGH_LOG2 = 100.0
DEN_LOW_LOG2 = 64.0

LANES = 128
SUBLANES = 8
HALO = SUBLANES
VMEM_LIMIT = 52 * 1024 * 1024

BF16 = jnp.bfloat16
F32 = jnp.float32

_NT = (((1,), (1,)), ((), ()))


def _params(sem):
    return pltpu.CompilerParams(dimension_semantics=sem, vmem_limit_bytes=VMEM_LIMIT)


def _full(shape):
    return pl.BlockSpec(shape, lambda *_: (0,) * len(shape))


def _lane_lo(shape):
    return lax.broadcasted_iota(jnp.int32, shape, len(shape) - 1) < HEAD_DIM


def _unit_lane(half):
    return HEAD_DIM if half == 0 else 0


def _unit_lane_variants(x):
    lane = jnp.arange(x.shape[-1]) % LANES
    one = jnp.ones((), x.dtype)
    zero = jnp.zeros((), x.dtype)
    lo = jnp.where(lane < HEAD_DIM, x, jnp.where(lane == HEAD_DIM, one, zero))
    hi = jnp.where(lane >= HEAD_DIM, x, jnp.where(lane == 0, one, zero))
    return jnp.stack([lo, hi])


def _fold_denominators(high, low, den):
    for t in range(den.shape[1] // LANES):
        piece = den[:, t * LANES:(t + 1) * LANES]
        high = jnp.maximum(high, piece)
        low = jnp.minimum(low, piece)
    return high, low


def _store_flags(flag_ref, high, low):
    flag_ref[0:SUBLANES, :] = jnp.broadcast_to(high, (SUBLANES, LANES))
    flag_ref[SUBLANES:2 * SUBLANES, :] = jnp.broadcast_to(low, (SUBLANES, LANES))


def _outside_safe_range(flags):
    f = flags.reshape(-1, 2, SUBLANES, LANES)
    return jnp.logical_not((jnp.max(f[:, 0]) < 2.0 ** DEN_HIGH_LOG2) & (jnp.min(f[:, 1]) > 2.0 ** -DEN_LOW_LOG2))


def _mod_kernel(c_ref, w_ref, b_ref, o_ref):
    s = c_ref[...]
    s = s * jax.nn.sigmoid(s)
    o_ref[0] = jnp.dot(s, w_ref[0], preferred_element_type=F32, precision=lax.Precision.HIGHEST) + b_ref[0]


def _modulation(cvec, ada_w, ada_b):
    nb = 6
    return pl.pallas_call(
        _mod_kernel,
        grid=(DEPTH, nb),
        in_specs=[pl.BlockSpec((8, D_MODEL), lambda l, j: (0, 0)),
                  pl.BlockSpec((1, D_MODEL, D_MODEL), lambda l, j: (l, 0, j)),
                  pl.BlockSpec((1, 1, D_MODEL), lambda l, j: (l, 0, j))],
        out_specs=pl.BlockSpec((1, 8, D_MODEL), lambda l, j: (l, 0, j)),
        out_shape=jax.ShapeDtypeStruct((DEPTH, 8, 6 * D_MODEL), F32),
        compiler_params=_params(("arbitrary", "arbitrary")),
        name="modulation",
    )(cvec, ada_w, ada_b.reshape(DEPTH, 1, 6 * D_MODEL))


def _group_sumsq(z, gmat):
    x2 = z * z
    hi = x2.astype(BF16)
    lo = (x2 - hi.astype(F32)).astype(BF16)
    return (jnp.dot(hi, gmat, preferred_element_type=F32) + jnp.dot(lo, gmat, preferred_element_type=F32))


VT_ROWS = 80
PROJ_ROWS = 512
PROJ_TM = 1024


def _proj_kernel(groups, x_ref, mod_ref, w_ref, gain_ref, cos_ref, sin_ref, gmat_ref, *out_refs):
    shift = mod_ref[0:1, :]
    scale = 1.0 + mod_ref[1:2, :]
    gmat = gmat_ref[...]
    rows = min(PROJ_ROWS, x_ref.shape[0])
    first = lax.broadcasted_iota(jnp.int32, (rows, LANES), 1) % 32 < 16
    tail_row = lax.broadcasted_iota(jnp.int32, (VT_ROWS - HEAD_DIM, LANES), 0)
    tail = jnp.where(tail_row == 0, 1.0, 0.0).astype(BF16)
    for sub in range(x_ref.shape[0] // rows):
        rs = slice(sub * rows, (sub + 1) * rows)
        h = (x_ref[rs, :] * scale + shift).astype(BF16)
        cos = cos_ref[rs, :]
        sin = sin_ref[rs, :]
        outs = iter(out_refs)
        for (c0, width, g0, rope, qscale, transposed_copy) in groups:
            o_ref = next(outs)
            t_ref = next(outs) if transposed_copy else None
            z = jnp.dot(h, w_ref[:, c0:c0 + width], preferred_element_type=F32)
            for b in range(width // LANES):
                zb = z[:, b * LANES:(b + 1) * LANES]
                if g0 is not None:
                    ms = _group_sumsq(zb, gmat) * (1.0 / HEAD_DIM)
                    zb = zb * lax.rsqrt(ms + RMS_EPS) * gain_ref[:, g0 + b * LANES:g0 + (b + 1) * LANES]
                if rope:
                    rot = jnp.where(first, pltpu.roll(zb, LANES - 16, 1), pltpu.roll(zb, 16, 1))
                    zb = zb * cos + rot * sin
                if qscale != 1.0:
                    zb = zb * qscale
                o_ref[rs, b * LANES:(b + 1) * LANES] = zb.astype(BF16)
                if transposed_copy:
                    zt = zb.T.astype(BF16)
                    for half in range(2):
                        for t in range(rows // LANES):
                            chunk = sub * (rows // LANES) + t
                            t_ref[b, half, chunk, 0:HEAD_DIM, :] = zt[half * HEAD_DIM:(half + 1) * HEAD_DIM,
                                                                      t * LANES:(t + 1) * LANES]
                            t_ref[b, half, chunk, HEAD_DIM:VT_ROWS, :] = tail


def _project(x, mod, w, gains, cos, sin, gmat, groups, tm):
    n = x.shape[0]
    win = w.shape[1]
    out_shape, out_specs = [], []
    for g in groups:
        out_shape.append(jax.ShapeDtypeStruct((n, g[1]), BF16))
        out_specs.append(pl.BlockSpec((tm, g[1]), lambda i: (i, 0)))
        if g[5]:
            nb = g[1] // LANES
            out_shape.append(jax.ShapeDtypeStruct((nb, 2, n // LANES, VT_ROWS, LANES), BF16))
            out_specs.append(pl.BlockSpec((nb, 2, tm // LANES, VT_ROWS, LANES), lambda i: (0, 0, i, 0, 0)))
    return pl.pallas_call(
        functools.partial(_proj_kernel, groups),
        grid=(n // tm,),
        in_specs=[pl.BlockSpec((tm, D_MODEL), lambda i: (i, 0)),
                  _full((6, D_MODEL)),
                  _full((D_MODEL, win)),
                  _full(gains.shape),
                  pl.BlockSpec((tm, LANES), lambda i: (i, 0)),
                  pl.BlockSpec((tm, LANES), lambda i: (i, 0)),
                  _full((LANES, LANES))],
        out_specs=out_specs,
        out_shape=out_shape,
        compiler_params=_params(("parallel",)),
        name="qkv_project",
    )(x, mod, w, gains, cos, sin, gmat)


GA_TQ = 1024
GA_TK = 2048


GA_KEY_CHUNK = 1024
GA_QUERY_CHUNK = 512


def _vt_window(vt_ref, lead, first_chunk, nchunks):
    return jnp.concatenate([vt_ref[lead + (first_chunk + t,)] for t in range(nchunks)], axis=1)


def _global_attn_kernel(q_ref, kc_ref, vct_ref, k_ref, vt_ref, o_ref, flag_ref, qx_ref, acc_ref):
    kk = pl.program_id(1)
    tq = q_ref.shape[0]
    nblk = q_ref.shape[1] // LANES

    def update(kb_ref, vtb_ref):
        nkeys = kb_ref.shape[0]
        kr = min(GA_KEY_CHUNK, nkeys)
        for j in range(nblk):
            for c0 in range(0, tq, GA_QUERY_CHUNK):
                cs = slice(c0, c0 + GA_QUERY_CHUNK)
                part = [None, None]
                for r0 in range(0, nkeys, kr):
                    st = [lax.dot_general(kb_ref[r0:r0 + kr, :], qx_ref[2 * j + half, cs, :], _NT,
                                          preferred_element_type=F32) for half in range(2)]
                    for half in range(2):
                        pv = jnp.dot(_vt_window(vtb_ref, (half,), r0 // LANES, kr // LANES),
                                     jnp.exp2(st[half]).astype(BF16), preferred_element_type=F32)
                        part[half] = pv if part[half] is None else part[half] + pv
                for half in range(2):
                    acc_ref[2 * j + half, :, cs] += part[half]

    @pl.when(kk == 0)
    def _():
        lo = _lane_lo((tq, LANES))
        for j in range(nblk):
            qj = q_ref[:, j * LANES:(j + 1) * LANES]
            zero = jnp.zeros_like(qj)
            qx_ref[2 * j] = jnp.where(lo, qj, zero)
            qx_ref[2 * j + 1] = jnp.where(lo, zero, qj)
        acc_ref[...] = jnp.zeros(acc_ref.shape, F32)
        update(kc_ref, vct_ref)

    update(k_ref, vt_ref)

    @pl.when(kk == pl.num_programs(1) - 1)
    def _():
        high = jnp.zeros((1, LANES), F32)
        low = jnp.full((1, LANES), BIG, F32)
        for j in range(nblk):
            halves = []
            for half in range(2):
                a = acc_ref[2 * j + half]
                den = a[HEAD_DIM:HEAD_DIM + 1, :]
                high, low = _fold_denominators(high, low, den)
                halves.append(a[0:HEAD_DIM, :] / den)
            o_ref[:, j * LANES:(j + 1) * LANES] = jnp.concatenate(halves, axis=0).T.astype(o_ref.dtype)
        _store_flags(flag_ref, high, low)


def _global_attn_exact_kernel(q_ref, kc_ref, vc_ref, k_ref, v_ref, o_ref, qx_ref, c_ref, acc_ref):
    kk = pl.program_id(1)
    tq = q_ref.shape[0]
    nblk = q_ref.shape[1] // LANES
    nh = 2 * nblk
    lane = lax.broadcasted_iota(jnp.int32, (tq, LANES), 1)

    def moving_reference_update(kx_ref, vx_ref, first):
        for h in range(nh):
            half = h % 2
            unit = _unit_lane(half)
            sp = lax.dot_general(qx_ref[h], kx_ref[half], _NT, preferred_element_type=F32)
            mb = jnp.max(sp, axis=-1, keepdims=True)
            c_old = c_ref[h]
            target = mb if first else jnp.maximum(c_old, c_old + mb)
            qx_ref[h] = jnp.where(lane == unit, -target, qx_ref[h].astype(F32)).astype(BF16)
            c_new = -qx_ref[h][:, unit:unit + 1].astype(F32)
            d = c_new - c_old
            pv = jnp.dot(jnp.exp2(sp - d).astype(BF16), vx_ref[half], preferred_element_type=F32)
            acc_ref[h] = pv if first else acc_ref[h] * jnp.exp2(-d) + pv
            c_ref[h] = c_new

    @pl.when(kk == 0)
    def _():
        lo = _lane_lo((tq, LANES))
        for j in range(nblk):
            qj = q_ref[:, j * LANES:(j + 1) * LANES]
            zero = jnp.zeros_like(qj)
            qx_ref[2 * j] = jnp.where(lo, qj, zero)
            qx_ref[2 * j + 1] = jnp.where(lo, zero, qj)
        c_ref[...] = jnp.zeros(c_ref.shape, F32)
        moving_reference_update(kc_ref, vc_ref, True)

    moving_reference_update(k_ref, v_ref, False)

    @pl.when(kk == pl.num_programs(1) - 1)
    def _():
        lo = _lane_lo((tq, LANES))
        for j in range(nblk):
            a_lo = acc_ref[2 * j]
            a_hi = acc_ref[2 * j + 1]
            o_lo = a_lo / a_lo[:, HEAD_DIM:HEAD_DIM + 1]
            o_hi = a_hi / a_hi[:, 0:1]
            o_ref[:, j * LANES:(j + 1) * LANES] = jnp.where(lo, o_lo, o_hi).astype(o_ref.dtype)


def _global_attention(q, k, v, vt, kc, vc, vct):
    n, qw = q.shape
    nc = kc.shape[0]
    nh = 2 * (qw // LANES)
    nq = n // GA_TQ
    grid = (nq, n // GA_TK)
    q_spec = pl.BlockSpec((GA_TQ, qw), lambda i, j: (i, 0))

    y, flags = pl.pallas_call(
        _global_attn_kernel,
        grid=grid,
        in_specs=[q_spec,
                  _full((nc, LANES)), _full((2, nc // LANES, VT_ROWS, LANES)),
                  pl.BlockSpec((GA_TK, LANES), lambda i, j: (j, 0)),
                  pl.BlockSpec((2, GA_TK // LANES, VT_ROWS, LANES), lambda i, j: (0, j, 0, 0))],
        out_specs=[q_spec, pl.BlockSpec((2 * SUBLANES, LANES), lambda i, j: (i, 0))],
        out_shape=[jax.ShapeDtypeStruct((n, qw), BF16),
                   jax.ShapeDtypeStruct((nq * 2 * SUBLANES, LANES), F32)],
        scratch_shapes=[pltpu.VMEM((nh, GA_TQ, LANES), BF16),
                        pltpu.VMEM((nh, VT_ROWS, GA_TQ), F32)],
        compiler_params=_params(("parallel", "arbitrary")),
        name="global_attention",
    )(q, kc, vct[0], k, vt[0])

    def exact():
        return pl.pallas_call(
            _global_attn_exact_kernel,
            grid=grid,
            in_specs=[q_spec,
                      _full((2, nc, LANES)), _full((2, nc, LANES)),
                      pl.BlockSpec((2, GA_TK, LANES), lambda i, j: (0, j, 0)),
                      pl.BlockSpec((2, GA_TK, LANES), lambda i, j: (0, j, 0))],
            out_specs=q_spec,
            out_shape=jax.ShapeDtypeStruct((n, qw), BF16),
            scratch_shapes=[pltpu.VMEM((nh, GA_TQ, LANES), BF16),
                            pltpu.VMEM((nh, GA_TQ, 1), F32),
                            pltpu.VMEM((nh, GA_TQ, LANES), F32)],
            compiler_params=_params(("parallel", "arbitrary")),
            name="global_attention_exact",
        )(q, _unit_lane_variants(kc), _unit_lane_variants(vc), _unit_lane_variants(k), _unit_lane_variants(v))

    return lax.cond(_outside_safe_range(flags), exact, lambda: y)


NB_QROWS = 8
NB_KROWS = NB_QROWS + NB_KH
NB_MASKED = 2 * NB_KH - 1
NB_TILES = 3 * (NB_MASKED + 1)


def _nbr_bias_tiles(rpb):
    nh = rpb.shape[0]
    col = np.arange(GRID_W)
    cs = np.clip(col - NB_KW // 2, 0, GRID_W - NB_KW)
    col_ok = (col[:, None] >= cs[None, :]) & (col[:, None] < cs[None, :] + NB_KW)
    dcol = col[:, None] - col[None, :] + NB_KW - 1
    sel_col = (dcol[:, :, None] == np.arange(2 * NB_KW - 1)) & col_ok[:, :, None]
    base = jnp.einsum('kqj,hdj->hdkq', jnp.asarray(sel_col, F32), rpb, precision=lax.Precision.HIGHEST)
    base = jnp.where(jnp.asarray(col_ok), base, NEG)
    masked = jnp.full((nh, 1, GRID_W, GRID_W), NEG, F32)
    base = jnp.concatenate([base, masked], axis=1)
    prev = jnp.concatenate([masked, base[:, :-1]], axis=1)
    allmasked = jnp.full_like(base, NEG)
    return jnp.concatenate([jnp.concatenate([base, prev], -1),
                            jnp.concatenate([base, allmasked], -1),
                            jnp.concatenate([allmasked, base], -1)], axis=1)


def _nbr_tile_index(rows):
    nq = rows // NB_QROWS
    idx = np.zeros((3, NB_KROWS, NB_QROWS // 2), np.int64)
    for v, blk in enumerate((0, 1, nq - 1)):
        r0 = blk * NB_QROWS
        k0 = int(np.clip(r0 - NB_KH // 2, 0, rows - NB_KROWS))
        for kr in range(NB_KROWS):
            ka = k0 + kr

            def code(qr):
                qa = r0 + qr
                rs = int(np.clip(qa - NB_KH // 2, 0, rows - NB_KH))
                return ka - qa + NB_KH - 1 if rs <= ka < rs + NB_KH else NB_MASKED

            for qp in range(NB_QROWS // 2):
                left, right = code(2 * qp), code(2 * qp + 1)
                if left < NB_MASKED and right < NB_MASKED:
                    assert right == left - 1
                    idx[v, kr, qp] = left
                elif left < NB_MASKED:
                    idx[v, kr, qp] = (NB_MASKED + 1) + left
                else:
                    idx[v, kr, qp] = 2 * (NB_MASKED + 1) + right
    return tuple(tuple(tuple(int(t) for t in row) for row in var) for var in idx)


def _nbr_attn_kernel(exact, tile_index, q_ref, k_ref, vt_ref, kc_ref, vct_ref, tiles_ref, o_ref, flag_ref,
                     bias_ref):
    i = pl.program_id(1)
    nq = pl.num_programs(1)
    rows = k_ref.shape[0] // GRID_W
    span = NB_KROWS * GRID_W

    def fill_bias(variant):
        for half in range(2):
            for kr in range(NB_KROWS):
                for qp in range(NB_QROWS // 2):
                    bias_ref[half, kr * GRID_W:(kr + 1) * GRID_W, qp * LANES:(qp + 1) * LANES] = (
                        tiles_ref[half, tile_index[variant][kr][qp]])

    for variant, at in enumerate((0, 1, nq - 1)):
        pl.when(i == at)(functools.partial(fill_bias, variant))

    r0 = i * NB_QROWS
    ks = pl.multiple_of(jnp.clip(r0 - NB_KH // 2, 0, rows - NB_KROWS) * GRID_W, 2 * LANES)
    kw = k_ref[pl.ds(ks, span), :]
    kc = kc_ref[...]
    q = q_ref[...]
    tq = q.shape[0]
    lo = _lane_lo(q.shape)
    zero = jnp.zeros_like(q)
    high = jnp.zeros((1, LANES), F32)
    low = jnp.full((1, LANES), BIG, F32)
    halves = []
    logits = []
    for half in range(2):
        qm = jnp.where(lo, q, zero) if half == 0 else jnp.where(lo, zero, q)
        logits.append((lax.dot_general(kw, qm, _NT, preferred_element_type=F32),
                       lax.dot_general(kc, qm, _NT, preferred_element_type=F32)))
    for half in range(2):
        s_nb = logits[half][0] + bias_ref[half]
        s_cx = logits[half][1]
        if exact:
            m = jnp.maximum(jnp.max(s_nb, axis=0, keepdims=True), jnp.max(s_cx, axis=0, keepdims=True))
            s_nb = s_nb - m
            s_cx = s_cx - m
        vtw = _vt_window(vt_ref, (0, half), ks // LANES, span // LANES)
        vtc = _vt_window(vct_ref, (0, half), 0, kc.shape[0] // LANES)
        ot = (jnp.dot(vtw, jnp.exp2(s_nb).astype(BF16), preferred_element_type=F32)
              + jnp.dot(vtc, jnp.exp2(s_cx).astype(BF16), preferred_element_type=F32))
        den = ot[HEAD_DIM:HEAD_DIM + 1, :]
        high, low = _fold_denominators(high, low, den)
        halves.append(ot[0:HEAD_DIM, :] / den)
    o_ref[...] = jnp.concatenate(halves, axis=0).T.astype(o_ref.dtype)
    _store_flags(flag_ref, high, low)


def _nbr_attention(q, k, vt, kc, vct, tiles):
    n, qw = q.shape
    nc = kc.shape[0]
    tq = NB_QROWS * GRID_W
    nq = n // tq
    assert nq >= 3
    npair = qw // LANES
    tile_index = _nbr_tile_index(n // GRID_W)
    operands = (q, k, vt, kc, vct, tiles)

    def run(exact):
        return pl.pallas_call(
            functools.partial(_nbr_attn_kernel, exact, tile_index),
            grid=(npair, nq),
            in_specs=[pl.BlockSpec((tq, LANES), lambda p, i: (i, p)),
                      pl.BlockSpec((n, LANES), lambda p, i: (0, p)),
                      pl.BlockSpec((1, 2, n // LANES, VT_ROWS, LANES), lambda p, i: (p, 0, 0, 0, 0)),
                      pl.BlockSpec((nc, LANES), lambda p, i: (0, p)),
                      pl.BlockSpec((1, 2, nc // LANES, VT_ROWS, LANES), lambda p, i: (p, 0, 0, 0, 0)),
                      pl.BlockSpec((2, NB_TILES, GRID_W, LANES), lambda p, i: (p, 0, 0, 0))],
            out_specs=[pl.BlockSpec((tq, LANES), lambda p, i: (i, p)),
                       pl.BlockSpec((2 * SUBLANES, LANES), lambda p, i: (p * nq + i, 0))],
            out_shape=[jax.ShapeDtypeStruct((n, qw), BF16),
                       jax.ShapeDtypeStruct((npair * nq * 2 * SUBLANES, LANES), F32)],
            scratch_shapes=[pltpu.VMEM((2, NB_KROWS * GRID_W, tq), F32)],
            compiler_params=_params(("parallel", "arbitrary")),
            name="neighbourhood_attention_exact" if exact else "neighbourhood_attention",
        )(*operands)

    y, flags = run(False)
    return lax.cond(_outside_safe_range(flags), lambda: run(True)[0], lambda: y)


WIN_TQ = 256
WIN_SPAN = WIN_TQ + 2 * C_WINDOW


def _window_attn_kernel(exact, sink_ref, q_ref, k_ref, vt_ref, kc_ref, vct_ref, o_ref, flag_ref):
    i = pl.program_id(0)
    n = k_ref.shape[0]
    tq = q_ref.shape[0]
    nblk = q_ref.shape[1] // LANES
    nchunk = WIN_SPAN // LANES
    q0 = i * tq
    ks = pl.multiple_of(jnp.clip(q0 - C_WINDOW, 0, n - WIN_SPAN), C_WINDOW)
    kw = k_ref[pl.ds(ks, WIN_SPAN), :]
    kc = kc_ref[...]
    kpos = ks + lax.broadcasted_iota(jnp.int32, (WIN_SPAN, tq), 0)
    qpos = q0 + lax.broadcasted_iota(jnp.int32, (WIN_SPAN, tq), 1)
    valid = jnp.abs(kpos - qpos) <= C_WINDOW
    lo = _lane_lo((tq, LANES))
    high = jnp.zeros((1, LANES), F32)
    low = jnp.full((1, LANES), BIG, F32)
    for j in range(nblk):
        q = q_ref[:, j * LANES:(j + 1) * LANES]
        zero = jnp.zeros_like(q)
        halves = []
        logits = []
        for half in range(2):
            qm = jnp.where(lo, q, zero) if half == 0 else jnp.where(lo, zero, q)
            logits.append((lax.dot_general(kw, qm, _NT, preferred_element_type=F32),
                           lax.dot_general(kc, qm, _NT, preferred_element_type=F32)))
        for half in range(2):
            sink = jnp.full((1, tq), sink_ref[j + nblk * half], F32)
            s_w = jnp.where(valid, logits[half][0], NEG)
            s_c = logits[half][1]
            if exact:
                m = jnp.maximum(jnp.maximum(jnp.max(s_w, axis=0, keepdims=True),
                                            jnp.max(s_c, axis=0, keepdims=True)), sink)
                s_w = s_w - m
                s_c = s_c - m
                sink = sink - m
            vtw = _vt_window(vt_ref, (half,), ks // LANES, nchunk)
            vtc = _vt_window(vct_ref, (half,), 0, kc.shape[0] // LANES)
            ot = (jnp.dot(vtw, jnp.exp2(s_w).astype(BF16), preferred_element_type=F32)
                  + jnp.dot(vtc, jnp.exp2(s_c).astype(BF16), preferred_element_type=F32))
            den = ot[HEAD_DIM:HEAD_DIM + 1, :] + jnp.exp2(sink)
            high, low = _fold_denominators(high, low, den)
            halves.append(ot[0:HEAD_DIM, :] / den)
        o_ref[:, j * LANES:(j + 1) * LANES] = jnp.concatenate(halves, axis=0).T.astype(o_ref.dtype)
    _store_flags(flag_ref, high, low)


def _window_attention(q, k, vt, kc, vct, sink):
    n, qw = q.shape
    nc = kc.shape[0]
    nq = n // WIN_TQ
    operands = (sink, q, k, vt[0], kc, vct[0])

    def run(exact):
        return pl.pallas_call(
            functools.partial(_window_attn_kernel, exact),
            grid=(nq,),
            in_specs=[pl.BlockSpec(memory_space=pltpu.SMEM),
                      pl.BlockSpec((WIN_TQ, qw), lambda i: (i, 0)),
                      _full((n, LANES)), _full((2, n // LANES, VT_ROWS, LANES)),
                      _full((nc, LANES)), _full((2, nc // LANES, VT_ROWS, LANES))],
            out_specs=[pl.BlockSpec((WIN_TQ, qw), lambda i: (i, 0)),
                       pl.BlockSpec((2 * SUBLANES, LANES), lambda i: (i, 0))],
            out_shape=[jax.ShapeDtypeStruct((n, qw), BF16),
                       jax.ShapeDtypeStruct((nq * 2 * SUBLANES, LANES), F32)],
            compiler_params=_params(("parallel",)),
            name="window_attention_exact" if exact else "window_attention",
        )(*operands)

    y, flags = run(False)
    return lax.cond(_outside_safe_range(flags), lambda: run(True)[0], lambda: y)


def _ctx_attn_kernel(sink_ref, q_ref, k_ref, v_ref, o_ref):
    j = pl.program_id(0)
    nblk = pl.num_programs(0)
    q = q_ref[...]
    k = k_ref[...]
    v = v_ref[...]
    lo = _lane_lo(q.shape)
    zero = jnp.zeros_like(q)
    outs = []
    for half in range(2):
        qm = jnp.where(lo, q, zero) if half == 0 else jnp.where(lo, zero, q)
        sink = sink_ref[j + nblk * half]
        s = lax.dot_general(qm, k, _NT, preferred_element_type=F32)
        m = jnp.maximum(jnp.max(s, axis=-1, keepdims=True), sink)
        p = jnp.exp2(s - m)
        den = jnp.sum(p, axis=-1, keepdims=True) + jnp.exp2(sink - m)
        outs.append(jnp.dot(p.astype(BF16), v, preferred_element_type=F32) / den)
    o_ref[...] = jnp.where(lo, outs[0], outs[1]).astype(o_ref.dtype)


def _ctx_attention(q, k, v, sink, kv_per_block):
    c, qw = q.shape
    nblk = qw // LANES
    return pl.pallas_call(
        _ctx_attn_kernel,
        grid=(nblk,),
        in_specs=[pl.BlockSpec(memory_space=pltpu.SMEM),
                  pl.BlockSpec((c, LANES), lambda j: (0, j)),
                  pl.BlockSpec((c, LANES), lambda j: (0, j * kv_per_block)),
                  pl.BlockSpec((c, LANES), lambda j: (0, j * kv_per_block))],
        out_specs=pl.BlockSpec((c, LANES), lambda j: (0, j)),
        out_shape=jax.ShapeDtypeStruct((c, qw), BF16),
        compiler_params=_params(("parallel",)),
        name="context_attention",
    )(sink, q, k, v)


def _residual_layer_norm(x, f, gate, g, b):
    z = x + (gate * (1.0 / DN_ALPHA)) * f
    mu = jnp.mean(z, axis=-1, keepdims=True)
    zc = z - mu
    var = jnp.mean(zc * zc, axis=-1, keepdims=True)
    return zc * lax.rsqrt(var + LN_EPS / (DN_ALPHA * DN_ALPHA)) * g + b


def _outproj_kernel(x_ref, y1_ref, y2_ref, w1_ref, w2_ref, mod_ref, g_ref, b_ref, o_ref):
    rows = x_ref.shape[0] // 2
    for s in range(2):
        rs = slice(s * rows, (s + 1) * rows)
        f = (jnp.dot(y1_ref[rs, :], w1_ref[...], preferred_element_type=F32)
             + jnp.dot(y2_ref[rs, :], w2_ref[...], preferred_element_type=F32))
        o_ref[rs, :] = _residual_layer_norm(x_ref[rs, :], f, mod_ref[2:3, :], g_ref[...], b_ref[...])


def _out_project(x, y1, y2, c2, w, mod, g, b, tm):
    n = x.shape[0]
    half = D_MODEL // 2
    return pl.pallas_call(
        _outproj_kernel,
        grid=(n // tm,),
        in_specs=[pl.BlockSpec((tm, D_MODEL), lambda i: (i, 0)),
                  pl.BlockSpec((tm, half), lambda i: (i, 0)),
                  pl.BlockSpec((tm, half), lambda i: (i, c2)),
                  pl.BlockSpec((half, D_MODEL), lambda i: (0, 0)),
                  pl.BlockSpec((half, D_MODEL), lambda i: (1, 0)),
                  _full((6, D_MODEL)), _full((1, D_MODEL)), _full((1, D_MODEL))],
        out_specs=pl.BlockSpec((tm, D_MODEL), lambda i: (i, 0)),
        out_shape=jax.ShapeDtypeStruct((n, D_MODEL), F32),
        compiler_params=_params(("parallel",)),
        name="out_project_ln",
    )(x, y1, y2, w, w, mod, g, b)


FFN_CHUNKS = 2
FFN_FC = D_FF // FFN_CHUNKS


FFN_CHAINS = 1
FFN_TM = 512


def _ffn_kernel(x_ref, xp_ref, xn_ref, mod_ref, wu_ref, cw_ref, cb_ref, wd_ref, g_ref, b_ref, o_ref,
                h_ref, ua_ref, ug_ref):
    i = pl.program_id(0)
    tm = x_ref.shape[0]
    rows = tm // FFN_CHAINS
    shift = mod_ref[3:4, :]
    scale = 1.0 + mod_ref[4:5, :]
    hp = jnp.where(i > 0, xp_ref[...] * scale + shift, 0.0)
    hn = jnp.where(i < pl.num_programs(0) - 1, xn_ref[...] * scale + shift, 0.0)
    hm = x_ref[...] * scale + shift
    h_ref[...] = jnp.concatenate([hp, hm, hn], axis=0).astype(BF16)

    def conv(u_ref, col0):
        cols = slice(col0, col0 + FFN_FC)
        return (cb_ref[:, cols]
                + u_ref[HALO - 1:HALO - 1 + rows, :] * cw_ref[0:1, cols]
                + u_ref[HALO:HALO + rows, :] * cw_ref[1:2, cols]
                + u_ref[HALO + 1:HALO + 1 + rows, :] * cw_ref[2:3, cols])

    for s in range(FFN_CHAINS):
        r0 = s * rows
        hs = h_ref[r0:r0 + rows + 2 * HALO, :]
        f = None
        for c in range(FFN_CHUNKS):
            a0 = c * FFN_FC
            g0 = D_FF + c * FFN_FC
            ua = ua_ref.at[c]
            ug = ug_ref.at[c]
            ua[...] = jnp.dot(hs, wu_ref[:, a0:a0 + FFN_FC], preferred_element_type=F32)
            ug[...] = jnp.dot(hs, wu_ref[:, g0:g0 + FFN_FC], preferred_element_type=F32)
            gt = conv(ug, g0)
            act = (gt * jax.nn.sigmoid(gt) * conv(ua, a0)).astype(BF16)
            part = jnp.dot(act, wd_ref[a0:a0 + FFN_FC, :], preferred_element_type=F32)
            f = part if f is None else f + part
        o_ref[r0:r0 + rows, :] = _residual_layer_norm(x_ref[r0:r0 + rows, :], f, mod_ref[5:6, :],
                                                      g_ref[...], b_ref[...])


def _conv_ffn(x, mod, w_up, conv_w, conv_b, w_down, g, b, tm):
    n = x.shape[0]
    nh = n // HALO
    per = tm // HALO
    rows = tm // FFN_CHAINS
    resident = pl.Buffered(1)
    return pl.pallas_call(
        _ffn_kernel,
        grid=(n // tm,),
        in_specs=[pl.BlockSpec((tm, D_MODEL), lambda i: (i, 0)),
                  pl.BlockSpec((HALO, D_MODEL), lambda i: (jnp.maximum(i * per - 1, 0), 0)),
                  pl.BlockSpec((HALO, D_MODEL), lambda i: (jnp.minimum((i + 1) * per, nh - 1), 0)),
                  _full((6, D_MODEL)),
                  pl.BlockSpec((D_MODEL, 2 * D_FF), lambda i: (0, 0), pipeline_mode=resident),
                  _full((3, 2 * D_FF)), _full((1, 2 * D_FF)),
                  pl.BlockSpec((D_FF, D_MODEL), lambda i: (0, 0), pipeline_mode=resident),
                  _full((1, D_MODEL)), _full((1, D_MODEL))],
        out_specs=pl.BlockSpec((tm, D_MODEL), lambda i: (i, 0)),
        out_shape=jax.ShapeDtypeStruct((n, D_MODEL), F32),
        scratch_shapes=[pltpu.VMEM((tm + 2 * HALO, D_MODEL), BF16),
                        pltpu.VMEM((FFN_CHUNKS, rows + 2 * HALO, FFN_FC), F32),
                        pltpu.VMEM((FFN_CHUNKS, rows + 2 * HALO, FFN_FC), F32)],
        compiler_params=_params(("parallel",)),
        name="conv_ffn_ln",
    )(x, x, x, mod, w_up, conv_w, conv_b, w_down, g, b)


def _pair_perm(n_heads, kv_heads):
    per = n_heads // kv_heads
    order = []
    for j in range(per):
        for g in range(kv_heads):
            order.append(g * per + j)
    cols = np.concatenate([np.arange(h * HEAD_DIM, (h + 1) * HEAD_DIM) for h in order])
    return cols


_PERM_A = _pair_perm(A_HEADS, A_KV_HEADS)
_PERM_C = _pair_perm(C_HEADS, C_KV_HEADS)


def _rope_tables(n):
    t = np.arange(n)
    row = (t // GRID_W).astype(np.float64)
    col = (t % GRID_W).astype(np.float64)
    half = HEAD_DIM // 2
    inv = ROPE_THETA ** (-np.arange(0, half, 2, dtype=np.float64) / half)
    ang_r = row[:, None] * inv
    ang_c = col[:, None] * inv
    ang = np.concatenate([ang_r, ang_r, ang_c, ang_c], -1)
    ang = np.concatenate([ang, ang], -1)
    sign = np.where(np.arange(LANES) % 32 < 16, -1.0, 1.0)
    return jnp.asarray(np.cos(ang), F32), jnp.asarray(np.sin(ang) * sign, F32)


_EVEN_GROUPS = (
    (0, A_Q, 0, True, QK_SCALE, False),
    (A_Q, A_KV, A_Q, True, 1.0, False),
    (A_Q + A_KV, A_KV, None, False, 1.0, True),
    (A_Q + 2 * A_KV, B_W, None, False, QK_SCALE, False),
    (A_Q + 2 * A_KV + B_W, B_W, None, False, 1.0, False),
    (A_Q + 2 * A_KV + 2 * B_W, B_W, None, False, 1.0, True),
)
_EVEN_GROUPS_CTX = tuple((c0, w, g0, False, s, t) for (c0, w, g0, _, s, t) in _EVEN_GROUPS)
_ODD_GROUPS = (
    (0, C_Q, None, True, QK_SCALE, False),
    (C_Q, C_KV, None, True, 1.0, False),
    (C_Q + C_KV, C_KV, None, False, 1.0, True),
)
_ODD_GROUPS_CTX = tuple((c0, w, g0, False, s, t) for (c0, w, g0, _, s, t) in _ODD_GROUPS)

LAT_TM = 1024


def kernel(x, c, ctx, c_ctx, ada_w, ada_b, ln_g, ln_b, ev_w_in, ev_w_out, ev_q_gain, ev_k_gain, ev_rpb,
           od_w_in, od_w_out, od_sink, ffn_w_up, ffn_conv_w, ffn_conv_b, ffn_w_down):
    n = x.shape[1]
    nc = ctx.shape[1]
    x_lat = x[0]
    x_ctx = ctx[0]

    cvec = jnp.zeros((8, D_MODEL), F32).at[0].set(c[0]).at[1].set(c_ctx)
    mods = _modulation(cvec, ada_w, ada_b)

    cos, sin = _rope_tables(n)
    cos_c, sin_c = cos[:nc], sin[:nc]
    lane = np.arange(LANES)
    gmat = jnp.asarray((lane[:, None] // HEAD_DIM) == (lane[None, :] // HEAD_DIM), BF16)
    no_sink = jnp.full((C_HEADS,), NEG, F32)

    for l in range(DEPTH):
        i = l // 2
        ctx_out = l < DEPTH - 1
        m_lat = mods[l, 0].reshape(6, D_MODEL)
        m_ctx = mods[l, 1].reshape(6, D_MODEL)
        g0, b0 = ln_g[l, 0][None], ln_b[l, 0][None]
        g1, b1 = ln_g[l, 1][None], ln_b[l, 1][None]
        if l % 2 == 0:
            w_in = ev_w_in[i]
            w_in = jnp.concatenate([w_in[:, :A_Q][:, _PERM_A], w_in[:, A_Q:]], axis=1).astype(BF16)
            w_out = ev_w_out[i]
            w_out = jnp.concatenate([w_out[:A_Q][_PERM_A], w_out[A_Q:]], axis=0).astype(BF16)
            gains = jnp.concatenate([jnp.tile(ev_q_gain[i], A_HEADS), jnp.tile(ev_k_gain[i], A_KV_HEADS)])[None]
            qa, ka, va, vat, qb, kb, vb, vbt = _project(x_lat, m_lat, w_in, gains, cos, sin, gmat, _EVEN_GROUPS,
                                                        PROJ_TM)
            qa_c, ka_c, va_c, vat_c, qb_c, kb_c, vb_c, vbt_c = _project(x_ctx, m_ctx, w_in, gains, cos_c, sin_c,
                                                                        gmat, _EVEN_GROUPS_CTX, nc)
            ya = _global_attention(qa, ka, va, vat, ka_c, va_c, vat_c)
            yb = _nbr_attention(qb, kb, vbt, kb_c, vbt_c, _nbr_bias_tiles(ev_rpb[i] * LOG2E))
            x_lat = _out_project(x_lat, ya, yb, 0, w_out, m_lat, g0, b0, LAT_TM)
            if ctx_out:
                ya_c = _ctx_attention(qa_c, ka_c, va_c, no_sink, 0)
                yb_c = _ctx_attention(qb_c, kb_c, vb_c, no_sink, 1)
                x_ctx = _out_project(x_ctx, ya_c, yb_c, 0, w_out, m_ctx, g0, b0, nc)
        else:
            w_in = od_w_in[i]
            w_in = jnp.concatenate([w_in[:, :C_Q][:, _PERM_C], w_in[:, C_Q:]], axis=1).astype(BF16)
            w_out = od_w_out[i][_PERM_C].astype(BF16)
            sink = od_sink[i] * LOG2E
            gains = jnp.ones((1, A_Q + A_KV), F32)
            q, k, _, vt = _project(x_lat, m_lat, w_in, gains, cos, sin, gmat, _ODD_GROUPS, PROJ_TM)
            q_c, k_c, v_c, vt_c = _project(x_ctx, m_ctx, w_in, gains, cos_c, sin_c, gmat, _ODD_GROUPS_CTX, nc)
            y = _window_attention(q, k, vt, k_c, vt_c, sink)
            x_lat = _out_project(x_lat, y, y, 1, w_out, m_lat, g0, b0, LAT_TM)
            if ctx_out:
                y_c = _ctx_attention(q_c, k_c, v_c, sink, 0)
                x_ctx = _out_project(x_ctx, y_c, y_c, 1, w_out, m_ctx, g0, b0, nc)
        w_up = ffn_w_up[l].astype(BF16)
        w_down = ffn_w_down[l].astype(BF16)
        cb = ffn_conv_b[l][None]
        x_lat = _conv_ffn(x_lat, m_lat, w_up, ffn_conv_w[l], cb, w_down, g1, b1, FFN_TM)
        if ctx_out:
            x_ctx = _conv_ffn(x_ctx, m_ctx, w_up, ffn_conv_w[l], cb, w_down, g1, b1, nc)
    return x_lat[None]
```

```python
import functools

import numpy as np
import jax
import jax.numpy as jnp
from jax import lax
from jax.experimental import pallas as pl
from jax.experimental.pallas import tpu as pltpu

D_MODEL = 1024
DEPTH = 4
GRID_W = 64
HEAD_DIM = 64
A_HEADS = 8
A_KV_HEADS = 2
B_HEADS = 8
NB_KH = 8
NB_KW = 16
C_HEADS = 16
C_KV_HEADS = 2
C_WINDOW = 128
D_FF = 2816
ROPE_THETA = 10000.0
LN_EPS = 1e-5
RMS_EPS = 1e-6
NEG = -1e30
BIG = 1e30
DN_ALPHA = (2 * DEPTH) ** 0.25
A_Q = A_HEADS * HEAD_DIM
A_KV = A_KV_HEADS * HEAD_DIM
B_W = B_HEADS * HEAD_DIM
C_Q = C_HEADS * HEAD_DIM
C_KV = C_KV_HEADS * HEAD_DIM
LOG2E = 1.4426950408889634
QK_SCALE = HEAD_DIM ** -0.5 * LOG2E
DEN_HIGH_LOG2 = 100.0
DEN_LOW_LOG2 = 64.0

LANES = 128
SUBLANES = 8
HALO = SUBLANES
VMEM_LIMIT = 52 * 1024 * 1024

BF16 = jnp.bfloat16
F32 = jnp.float32

_NT = (((1,), (1,)), ((), ()))


def _params(sem):
    return pltpu.CompilerParams(dimension_semantics=sem, vmem_limit_bytes=VMEM_LIMIT)


def _full(shape):
    return pl.BlockSpec(shape, lambda *_: (0,) * len(shape))


def _lane_lo(shape):
    return lax.broadcasted_iota(jnp.int32, shape, len(shape) - 1) < HEAD_DIM


def _unit_lane(half):
    return HEAD_DIM if half == 0 else 0


def _unit_lane_variants(x):
    lane = jnp.arange(x.shape[-1]) % LANES
    one = jnp.ones((), x.dtype)
    zero = jnp.zeros((), x.dtype)
    lo = jnp.where(lane < HEAD_DIM, x, jnp.where(lane == HEAD_DIM, one, zero))
    hi = jnp.where(lane >= HEAD_DIM, x, jnp.where(lane == 0, one, zero))
    return jnp.stack([lo, hi])


def _fold_denominators(high, low, den):
    for t in range(den.shape[1] // LANES):
        piece = den[:, t * LANES:(t + 1) * LANES]
        high = jnp.maximum(high, piece)
        low = jnp.minimum(low, piece)
    return high, low


def _store_flags(flag_ref, high, low):
    flag_ref[0:SUBLANES, :] = jnp.broadcast_to(high, (SUBLANES, LANES))
    flag_ref[SUBLANES:2 * SUBLANES, :] = jnp.broadcast_to(low, (SUBLANES, LANES))


def _outside_safe_range(flags):
    f = flags.reshape(-1, 2, SUBLANES, LANES)
    return jnp.logical_not((jnp.max(f[:, 0]) < 2.0 ** DEN_HIGH_LOG2) & (jnp.min(f[:, 1]) > 2.0 ** -DEN_LOW_LOG2))


def _mod_kernel(c_ref, w_ref, b_ref, o_ref):
    s = c_ref[...]
    s = s * jax.nn.sigmoid(s)
    o_ref[0] = jnp.dot(s, w_ref[0], preferred_element_type=F32, precision=lax.Precision.HIGHEST) + b_ref[0]


def _modulation(cvec, ada_w, ada_b):
    nb = 6
    return pl.pallas_call(
        _mod_kernel,
        grid=(DEPTH, nb),
        in_specs=[pl.BlockSpec((8, D_MODEL), lambda l, j: (0, 0)),
                  pl.BlockSpec((1, D_MODEL, D_MODEL), lambda l, j: (l, 0, j)),
                  pl.BlockSpec((1, 1, D_MODEL), lambda l, j: (l, 0, j))],
        out_specs=pl.BlockSpec((1, 8, D_MODEL), lambda l, j: (l, 0, j)),
        out_shape=jax.ShapeDtypeStruct((DEPTH, 8, 6 * D_MODEL), F32),
        compiler_params=_params(("arbitrary", "arbitrary")),
        name="modulation",
    )(cvec, ada_w, ada_b.reshape(DEPTH, 1, 6 * D_MODEL))


def _group_sumsq(z, gmat):
    x2 = z * z
    hi = x2.astype(BF16)
    lo = (x2 - hi.astype(F32)).astype(BF16)
    return (jnp.dot(hi, gmat, preferred_element_type=F32) + jnp.dot(lo, gmat, preferred_element_type=F32))


VT_ROWS = 80
PROJ_ROWS = 512
PROJ_TM = 1024


def _proj_kernel(groups, x_ref, mod_ref, w_ref, gain_ref, cos_ref, sin_ref, gmat_ref, *out_refs):
    shift = mod_ref[0:1, :]
    scale = 1.0 + mod_ref[1:2, :]
    gmat = gmat_ref[...]
    rows = min(PROJ_ROWS, x_ref.shape[0])
    first = lax.broadcasted_iota(jnp.int32, (rows, LANES), 1) % 32 < 16
    tail_row = lax.broadcasted_iota(jnp.int32, (VT_ROWS - HEAD_DIM, LANES), 0)
    tail = jnp.where(tail_row == 0, 1.0, 0.0).astype(BF16)
    for sub in range(x_ref.shape[0] // rows):
        rs = slice(sub * rows, (sub + 1) * rows)
        h = (x_ref[rs, :] * scale + shift).astype(BF16)
        cos = cos_ref[rs, :]
        sin = sin_ref[rs, :]
        outs = iter(out_refs)
        for (c0, width, g0, rope, qscale, transposed_copy) in groups:
            o_ref = next(outs)
            t_ref = next(outs) if transposed_copy else None
            z = jnp.dot(h, w_ref[:, c0:c0 + width], preferred_element_type=F32)
            for b in range(width // LANES):
                zb = z[:, b * LANES:(b + 1) * LANES]
                if g0 is not None:
                    ms = _group_sumsq(zb, gmat) * (1.0 / HEAD_DIM)
                    zb = zb * lax.rsqrt(ms + RMS_EPS) * gain_ref[:, g0 + b * LANES:g0 + (b + 1) * LANES]
                if rope:
                    rot = jnp.where(first, pltpu.roll(zb, LANES - 16, 1), pltpu.roll(zb, 16, 1))
                    zb = zb * cos + rot * sin
                if qscale != 1.0:
                    zb = zb * qscale
                o_ref[rs, b * LANES:(b + 1) * LANES] = zb.astype(BF16)
                if transposed_copy:
                    zt = zb.T.astype(BF16)
                    for half in range(2):
                        for t in range(rows // LANES):
                            chunk = sub * (rows // LANES) + t
                            t_ref[b, half, chunk, 0:HEAD_DIM, :] = zt[half * HEAD_DIM:(half + 1) * HEAD_DIM,
                                                                      t * LANES:(t + 1) * LANES]
                            t_ref[b, half, chunk, HEAD_DIM:VT_ROWS, :] = tail


def _project(x, mod, w, gains, cos, sin, gmat, groups, tm):
    n = x.shape[0]
    win = w.shape[1]
    out_shape, out_specs = [], []
    for g in groups:
        out_shape.append(jax.ShapeDtypeStruct((n, g[1]), BF16))
        out_specs.append(pl.BlockSpec((tm, g[1]), lambda i: (i, 0)))
        if g[5]:
            nb = g[1] // LANES
            out_shape.append(jax.ShapeDtypeStruct((nb, 2, n // LANES, VT_ROWS, LANES), BF16))
            out_specs.append(pl.BlockSpec((nb, 2, tm // LANES, VT_ROWS, LANES), lambda i: (0, 0, i, 0, 0)))
    return pl.pallas_call(
        functools.partial(_proj_kernel, groups),
        grid=(n // tm,),
        in_specs=[pl.BlockSpec((tm, D_MODEL), lambda i: (i, 0)),
                  _full((6, D_MODEL)),
                  _full((D_MODEL, win)),
                  _full(gains.shape),
                  pl.BlockSpec((tm, LANES), lambda i: (i, 0)),
                  pl.BlockSpec((tm, LANES), lambda i: (i, 0)),
                  _full((LANES, LANES))],
        out_specs=out_specs,
        out_shape=out_shape,
        compiler_params=_params(("parallel",)),
        name="qkv_project",
    )(x, mod, w, gains, cos, sin, gmat)


GA_TQ = 1024
GA_TK = 2048


GA_KEY_CHUNK = 1024
GA_QUERY_CHUNK = 512


def _vt_window(vt_ref, lead, first_chunk, nchunks):
    return jnp.concatenate([vt_ref[lead + (first_chunk + t,)] for t in range(nchunks)], axis=1)


def _global_attn_kernel(q_ref, kc_ref, vct_ref, k_ref, vt_ref, o_ref, flag_ref, qx_ref, acc_ref):
    kk = pl.program_id(1)
    tq = q_ref.shape[0]
    nblk = q_ref.shape[1] // LANES

    def update(kb_ref, vtb_ref):
        nkeys = kb_ref.shape[0]
        kr = min(GA_KEY_CHUNK, nkeys)
        for j in range(nblk):
            for c0 in range(0, tq, GA_QUERY_CHUNK):
                cs = slice(c0, c0 + GA_QUERY_CHUNK)
                part = [None, None]
                for r0 in range(0, nkeys, kr):
                    st = [lax.dot_general(kb_ref[r0:r0 + kr, :], qx_ref[2 * j + half, cs, :], _NT,
                                          preferred_element_type=F32) for half in range(2)]
                    for half in range(2):
                        pv = jnp.dot(_vt_window(vtb_ref, (half,), r0 // LANES, kr // LANES),
                                     jnp.exp2(st[half]).astype(BF16), preferred_element_type=F32)
                        part[half] = pv if part[half] is None else part[half] + pv
                for half in range(2):
                    acc_ref[2 * j + half, :, cs] += part[half]

    @pl.when(kk == 0)
    def _():
        lo = _lane_lo((tq, LANES))
        for j in range(nblk):
            qj = q_ref[:, j * LANES:(j + 1) * LANES]
            zero = jnp.zeros_like(qj)
            qx_ref[2 * j] = jnp.where(lo, qj, zero)
            qx_ref[2 * j + 1] = jnp.where(lo, zero, qj)
        acc_ref[...] = jnp.zeros(acc_ref.shape, F32)
        update(kc_ref, vct_ref)

    update(k_ref, vt_ref)

    @pl.when(kk == pl.num_programs(1) - 1)
    def _():
        high = jnp.zeros((1, LANES), F32)
        low = jnp.full((1, LANES), BIG, F32)
        for j in range(nblk):
            halves = []
            for half in range(2):
                a = acc_ref[2 * j + half]
                den = a[HEAD_DIM:HEAD_DIM + 1, :]
                high, low = _fold_denominators(high, low, den)
                halves.append(a[0:HEAD_DIM, :] / den)
            o_ref[:, j * LANES:(j + 1) * LANES] = jnp.concatenate(halves, axis=0).T.astype(o_ref.dtype)
        _store_flags(flag_ref, high, low)


def _global_attn_exact_kernel(q_ref, kc_ref, vc_ref, k_ref, v_ref, o_ref, qx_ref, c_ref, acc_ref):
    kk = pl.program_id(1)
    tq = q_ref.shape[0]
    nblk = q_ref.shape[1] // LANES
    nh = 2 * nblk
    lane = lax.broadcasted_iota(jnp.int32, (tq, LANES), 1)

    def moving_reference_update(kx_ref, vx_ref, first):
        for h in range(nh):
            half = h % 2
            unit = _unit_lane(half)
            sp = lax.dot_general(qx_ref[h], kx_ref[half], _NT, preferred_element_type=F32)
            mb = jnp.max(sp, axis=-1, keepdims=True)
            c_old = c_ref[h]
            target = mb if first else jnp.maximum(c_old, c_old + mb)
            qx_ref[h] = jnp.where(lane == unit, -target, qx_ref[h].astype(F32)).astype(BF16)
            c_new = -qx_ref[h][:, unit:unit + 1].astype(F32)
            d = c_new - c_old
            pv = jnp.dot(jnp.exp2(sp - d).astype(BF16), vx_ref[half], preferred_element_type=F32)
            acc_ref[h] = pv if first else acc_ref[h] * jnp.exp2(-d) + pv
            c_ref[h] = c_new

    @pl.when(kk == 0)
    def _():
        lo = _lane_lo((tq, LANES))
        for j in range(nblk):
            qj = q_ref[:, j * LANES:(j + 1) * LANES]
            zero = jnp.zeros_like(qj)
            qx_ref[2 * j] = jnp.where(lo, qj, zero)
            qx_ref[2 * j + 1] = jnp.where(lo, zero, qj)
        c_ref[...] = jnp.zeros(c_ref.shape, F32)
        moving_reference_update(kc_ref, vc_ref, True)

    moving_reference_update(k_ref, v_ref, False)

    @pl.when(kk == pl.num_programs(1) - 1)
    def _():
        lo = _lane_lo((tq, LANES))
        for j in range(nblk):
            a_lo = acc_ref[2 * j]
            a_hi = acc_ref[2 * j + 1]
            o_lo = a_lo / a_lo[:, HEAD_DIM:HEAD_DIM + 1]
            o_hi = a_hi / a_hi[:, 0:1]
            o_ref[:, j * LANES:(j + 1) * LANES] = jnp.where(lo, o_lo, o_hi).astype(o_ref.dtype)


def _global_attention(q, k, v, vt, kc, vc, vct):
    n, qw = q.shape
    nc = kc.shape[0]
    nh = 2 * (qw // LANES)
    nq = n // GA_TQ
    grid = (nq, n // GA_TK)
    q_spec = pl.BlockSpec((GA_TQ, qw), lambda i, j: (i, 0))

    y, flags = pl.pallas_call(
        _global_attn_kernel,
        grid=grid,
        in_specs=[q_spec,
                  _full((nc, LANES)), _full((2, nc // LANES, VT_ROWS, LANES)),
                  pl.BlockSpec((GA_TK, LANES), lambda i, j: (j, 0)),
                  pl.BlockSpec((2, GA_TK // LANES, VT_ROWS, LANES), lambda i, j: (0, j, 0, 0))],
        out_specs=[q_spec, pl.BlockSpec((2 * SUBLANES, LANES), lambda i, j: (i, 0))],
        out_shape=[jax.ShapeDtypeStruct((n, qw), BF16),
                   jax.ShapeDtypeStruct((nq * 2 * SUBLANES, LANES), F32)],
        scratch_shapes=[pltpu.VMEM((nh, GA_TQ, LANES), BF16),
                        pltpu.VMEM((nh, VT_ROWS, GA_TQ), F32)],
        compiler_params=_params(("parallel", "arbitrary")),
        name="global_attention",
    )(q, kc, vct[0], k, vt[0])

    def exact():
        return pl.pallas_call(
            _global_attn_exact_kernel,
            grid=grid,
            in_specs=[q_spec,
                      _full((2, nc, LANES)), _full((2, nc, LANES)),
                      pl.BlockSpec((2, GA_TK, LANES), lambda i, j: (0, j, 0)),
                      pl.BlockSpec((2, GA_TK, LANES), lambda i, j: (0, j, 0))],
            out_specs=q_spec,
            out_shape=jax.ShapeDtypeStruct((n, qw), BF16),
            scratch_shapes=[pltpu.VMEM((nh, GA_TQ, LANES), BF16),
                            pltpu.VMEM((nh, GA_TQ, 1), F32),
                            pltpu.VMEM((nh, GA_TQ, LANES), F32)],
            compiler_params=_params(("parallel", "arbitrary")),
            name="global_attention_exact",
        )(q, _unit_lane_variants(kc), _unit_lane_variants(vc), _unit_lane_variants(k), _unit_lane_variants(v))

    return lax.cond(_outside_safe_range(flags), exact, lambda: y)


NB_QROWS = 8
NB_KROWS = NB_QROWS + NB_KH
NB_MASKED = 2 * NB_KH - 1
NB_TILES = 3 * (NB_MASKED + 1)


def _nbr_bias_tiles(rpb):
    nh = rpb.shape[0]
    col = np.arange(GRID_W)
    cs = np.clip(col - NB_KW // 2, 0, GRID_W - NB_KW)
    col_ok = (col[:, None] >= cs[None, :]) & (col[:, None] < cs[None, :] + NB_KW)
    dcol = col[:, None] - col[None, :] + NB_KW - 1
    sel_col = (dcol[:, :, None] == np.arange(2 * NB_KW - 1)) & col_ok[:, :, None]
    base = jnp.einsum('kqj,hdj->hdkq', jnp.asarray(sel_col, F32), rpb, precision=lax.Precision.HIGHEST)
    base = jnp.where(jnp.asarray(col_ok), base, NEG)
    masked = jnp.full((nh, 1, GRID_W, GRID_W), NEG, F32)
    base = jnp.concatenate([base, masked], axis=1)
    prev = jnp.concatenate([masked, base[:, :-1]], axis=1)
    allmasked = jnp.full_like(base, NEG)
    return jnp.concatenate([jnp.concatenate([base, prev], -1),
                            jnp.concatenate([base, allmasked], -1),
                            jnp.concatenate([allmasked, base], -1)], axis=1)


def _nbr_tile_index(rows):
    nq = rows // NB_QROWS
    idx = np.zeros((3, NB_KROWS, NB_QROWS // 2), np.int64)
    for v, blk in enumerate((0, 1, nq - 1)):
        r0 = blk * NB_QROWS
        k0 = int(np.clip(r0 - NB_KH // 2, 0, rows - NB_KROWS))
        for kr in range(NB_KROWS):
            ka = k0 + kr

            def code(qr):
                qa = r0 + qr
                rs = int(np.clip(qa - NB_KH // 2, 0, rows - NB_KH))
                return ka - qa + NB_KH - 1 if rs <= ka < rs + NB_KH else NB_MASKED

            for qp in range(NB_QROWS // 2):
                left, right = code(2 * qp), code(2 * qp + 1)
                if left < NB_MASKED and right < NB_MASKED:
                    assert right == left - 1
                    idx[v, kr, qp] = left
                elif left < NB_MASKED:
                    idx[v, kr, qp] = (NB_MASKED + 1) + left
                else:
                    idx[v, kr, qp] = 2 * (NB_MASKED + 1) + right
    return tuple(tuple(tuple(int(t) for t in row) for row in var) for var in idx)


def _nbr_attn_kernel(exact, tile_index, q_ref, k_ref, vt_ref, kc_ref, vct_ref, tiles_ref, o_ref, flag_ref,
                     bias_ref):
    i = pl.program_id(1)
    nq = pl.num_programs(1)
    rows = k_ref.shape[0] // GRID_W
    span = NB_KROWS * GRID_W

    def fill_bias(variant):
        for half in range(2):
            for kr in range(NB_KROWS):
                for qp in range(NB_QROWS // 2):
                    bias_ref[half, kr * GRID_W:(kr + 1) * GRID_W, qp * LANES:(qp + 1) * LANES] = (
                        tiles_ref[half, tile_index[variant][kr][qp]])

    for variant, at in enumerate((0, 1, nq - 1)):
        pl.when(i == at)(functools.partial(fill_bias, variant))

    r0 = i * NB_QROWS
    ks = pl.multiple_of(jnp.clip(r0 - NB_KH // 2, 0, rows - NB_KROWS) * GRID_W, 2 * LANES)
    kw = k_ref[pl.ds(ks, span), :]
    kc = kc_ref[...]
    q = q_ref[...]
    tq = q.shape[0]
    lo = _lane_lo(q.shape)
    zero = jnp.zeros_like(q)
    high = jnp.zeros((1, LANES), F32)
    low = jnp.full((1, LANES), BIG, F32)
    halves = []
    logits = []
    for half in range(2):
        qm = jnp.where(lo, q, zero) if half == 0 else jnp.where(lo, zero, q)
        logits.append((lax.dot_general(kw, qm, _NT, preferred_element_type=F32),
                       lax.dot_general(kc, qm, _NT, preferred_element_type=F32)))
    for half in range(2):
        s_nb = logits[half][0] + bias_ref[half]
        s_cx = logits[half][1]
        if exact:
            m = jnp.maximum(jnp.max(s_nb, axis=0, keepdims=True), jnp.max(s_cx, axis=0, keepdims=True))
            s_nb = s_nb - m
            s_cx = s_cx - m
        vtw = _vt_window(vt_ref, (0, half), ks // LANES, span // LANES)
        vtc = _vt_window(vct_ref, (0, half), 0, kc.shape[0] // LANES)
        ot = (jnp.dot(vtw, jnp.exp2(s_nb).astype(BF16), preferred_element_type=F32)
              + jnp.dot(vtc, jnp.exp2(s_cx).astype(BF16), preferred_element_type=F32))
        den = ot[HEAD_DIM:HEAD_DIM + 1, :]
        high, low = _fold_denominators(high, low, den)
        halves.append(ot[0:HEAD_DIM, :] / den)
    o_ref[...] = jnp.concatenate(halves, axis=0).T.astype(o_ref.dtype)
    _store_flags(flag_ref, high, low)


def _nbr_attention(q, k, vt, kc, vct, tiles):
    n, qw = q.shape
    nc = kc.shape[0]
    tq = NB_QROWS * GRID_W
    nq = n // tq
    assert nq >= 3
    npair = qw // LANES
    tile_index = _nbr_tile_index(n // GRID_W)
    operands = (q, k, vt, kc, vct, tiles)

    def run(exact):
        return pl.pallas_call(
            functools.partial(_nbr_attn_kernel, exact, tile_index),
            grid=(npair, nq),
            in_specs=[pl.BlockSpec((tq, LANES), lambda p, i: (i, p)),
                      pl.BlockSpec((n, LANES), lambda p, i: (0, p)),
                      pl.BlockSpec((1, 2, n // LANES, VT_ROWS, LANES), lambda p, i: (p, 0, 0, 0, 0)),
                      pl.BlockSpec((nc, LANES), lambda p, i: (0, p)),
                      pl.BlockSpec((1, 2, nc // LANES, VT_ROWS, LANES), lambda p, i: (p, 0, 0, 0, 0)),
                      pl.BlockSpec((2, NB_TILES, GRID_W, LANES), lambda p, i: (p, 0, 0, 0))],
            out_specs=[pl.BlockSpec((tq, LANES), lambda p, i: (i, p)),
                       pl.BlockSpec((2 * SUBLANES, LANES), lambda p, i: (p * nq + i, 0))],
            out_shape=[jax.ShapeDtypeStruct((n, qw), BF16),
                       jax.ShapeDtypeStruct((npair * nq * 2 * SUBLANES, LANES), F32)],
            scratch_shapes=[pltpu.VMEM((2, NB_KROWS * GRID_W, tq), F32)],
            compiler_params=_params(("parallel", "arbitrary")),
            name="neighbourhood_attention_exact" if exact else "neighbourhood_attention",
        )(*operands)

    y, flags = run(False)
    return lax.cond(_outside_safe_range(flags), lambda: run(True)[0], lambda: y)


WIN_TQ = 256
WIN_SPAN = WIN_TQ + 2 * C_WINDOW
WIN_STEP_ROWS = 2 * WIN_TQ


def _window_attn_kernel(exact, sink_ref, q_ref, k_ref, vt_ref, kc_ref, vct_ref, o_ref, flag_ref):
    i = pl.program_id(0)
    n = k_ref.shape[0]
    tq = WIN_TQ
    nblk = q_ref.shape[1] // LANES
    nchunk = WIN_SPAN // LANES
    kc = kc_ref[...]
    lo = _lane_lo((tq, LANES))
    high = jnp.zeros((1, LANES), F32)
    low = jnp.full((1, LANES), BIG, F32)
    for sub in range(q_ref.shape[0] // tq):
        rs = slice(sub * tq, (sub + 1) * tq)
        q0 = i * q_ref.shape[0] + sub * tq
        ks = pl.multiple_of(jnp.clip(q0 - C_WINDOW, 0, n - WIN_SPAN), C_WINDOW)
        kw = k_ref[pl.ds(ks, WIN_SPAN), :]
        kpos = ks + lax.broadcasted_iota(jnp.int32, (WIN_SPAN, tq), 0)
        qpos = q0 + lax.broadcasted_iota(jnp.int32, (WIN_SPAN, tq), 1)
        valid = jnp.abs(kpos - qpos) <= C_WINDOW
        for j in range(nblk):
            q = q_ref[rs, j * LANES:(j + 1) * LANES]
            zero = jnp.zeros_like(q)
            halves = []
            logits = []
            for half in range(2):
                qm = jnp.where(lo, q, zero) if half == 0 else jnp.where(lo, zero, q)
                logits.append((lax.dot_general(kw, qm, _NT, preferred_element_type=F32),
                               lax.dot_general(kc, qm, _NT, preferred_element_type=F32)))
            for half in range(2):
                sink = jnp.full((1, tq), sink_ref[j + nblk * half], F32)
                s_w = jnp.where(valid, logits[half][0], NEG)
                s_c = logits[half][1]
                if exact:
                    m = jnp.maximum(jnp.maximum(jnp.max(s_w, axis=0, keepdims=True),
                                                jnp.max(s_c, axis=0, keepdims=True)), sink)
                    s_w = s_w - m
                    s_c = s_c - m
                    sink = sink - m
                vtw = _vt_window(vt_ref, (half,), ks // LANES, nchunk)
                vtc = _vt_window(vct_ref, (half,), 0, kc.shape[0] // LANES)
                ot = (jnp.dot(vtw, jnp.exp2(s_w).astype(BF16), preferred_element_type=F32)
                      + jnp.dot(vtc, jnp.exp2(s_c).astype(BF16), preferred_element_type=F32))
                den = ot[HEAD_DIM:HEAD_DIM + 1, :] + jnp.exp2(sink)
                high, low = _fold_denominators(high, low, den)
                halves.append(ot[0:HEAD_DIM, :] / den)
            o_ref[rs, j * LANES:(j + 1) * LANES] = jnp.concatenate(halves, axis=0).T.astype(o_ref.dtype)
    _store_flags(flag_ref, high, low)


def _window_attention(q, k, vt, kc, vct, sink):
    n, qw = q.shape
    nc = kc.shape[0]
    step = min(WIN_STEP_ROWS, n)
    nq = n // step
    operands = (sink, q, k, vt[0], kc, vct[0])

    def run(exact):
        return pl.pallas_call(
            functools.partial(_window_attn_kernel, exact),
            grid=(nq,),
            in_specs=[pl.BlockSpec(memory_space=pltpu.SMEM),
                      pl.BlockSpec((step, qw), lambda i: (i, 0)),
                      _full((n, LANES)), _full((2, n // LANES, VT_ROWS, LANES)),
                      _full((nc, LANES)), _full((2, nc // LANES, VT_ROWS, LANES))],
            out_specs=[pl.BlockSpec((step, qw), lambda i: (i, 0)),
                       pl.BlockSpec((2 * SUBLANES, LANES), lambda i: (i, 0))],
            out_shape=[jax.ShapeDtypeStruct((n, qw), BF16),
                       jax.ShapeDtypeStruct((nq * 2 * SUBLANES, LANES), F32)],
            compiler_params=_params(("parallel",)),
            name="window_attention_exact" if exact else "window_attention",
        )(*operands)

    y, flags = run(False)
    return lax.cond(_outside_safe_range(flags), lambda: run(True)[0], lambda: y)


def _ctx_attn_kernel(sink_ref, q_ref, k_ref, v_ref, o_ref):
    j = pl.program_id(0)
    nblk = pl.num_programs(0)
    q = q_ref[...]
    k = k_ref[...]
    v = v_ref[...]
    lo = _lane_lo(q.shape)
    zero = jnp.zeros_like(q)
    outs = []
    for half in range(2):
        qm = jnp.where(lo, q, zero) if half == 0 else jnp.where(lo, zero, q)
        sink = sink_ref[j + nblk * half]
        s = lax.dot_general(qm, k, _NT, preferred_element_type=F32)
        m = jnp.maximum(jnp.max(s, axis=-1, keepdims=True), sink)
        p = jnp.exp2(s - m)
        den = jnp.sum(p, axis=-1, keepdims=True) + jnp.exp2(sink - m)
        outs.append(jnp.dot(p.astype(BF16), v, preferred_element_type=F32) / den)
    o_ref[...] = jnp.where(lo, outs[0], outs[1]).astype(o_ref.dtype)


def _ctx_attention(q, k, v, sink, kv_per_block):
    c, qw = q.shape
    nblk = qw // LANES
    return pl.pallas_call(
        _ctx_attn_kernel,
        grid=(nblk,),
        in_specs=[pl.BlockSpec(memory_space=pltpu.SMEM),
                  pl.BlockSpec((c, LANES), lambda j: (0, j)),
                  pl.BlockSpec((c, LANES), lambda j: (0, j * kv_per_block)),
                  pl.BlockSpec((c, LANES), lambda j: (0, j * kv_per_block))],
        out_specs=pl.BlockSpec((c, LANES), lambda j: (0, j)),
        out_shape=jax.ShapeDtypeStruct((c, qw), BF16),
        compiler_params=_params(("parallel",)),
        name="context_attention",
    )(sink, q, k, v)


def _residual_layer_norm(x, f, gate, g, b):
    z = x + (gate * (1.0 / DN_ALPHA)) * f
    mu = jnp.mean(z, axis=-1, keepdims=True)
    zc = z - mu
    var = jnp.mean(zc * zc, axis=-1, keepdims=True)
    return zc * lax.rsqrt(var + LN_EPS / (DN_ALPHA * DN_ALPHA)) * g + b


def _outproj_kernel(x_ref, y1_ref, y2_ref, w1_ref, w2_ref, mod_ref, g_ref, b_ref, o_ref):
    rows = x_ref.shape[0] // 2
    for s in range(2):
        rs = slice(s * rows, (s + 1) * rows)
        f = (jnp.dot(y1_ref[rs, :], w1_ref[...], preferred_element_type=F32)
             + jnp.dot(y2_ref[rs, :], w2_ref[...], preferred_element_type=F32))
        o_ref[rs, :] = _residual_layer_norm(x_ref[rs, :], f, mod_ref[2:3, :], g_ref[...], b_ref[...])


def _out_project(x, y1, y2, c2, w, mod, g, b, tm):
    n = x.shape[0]
    half = D_MODEL // 2
    return pl.pallas_call(
        _outproj_kernel,
        grid=(n // tm,),
        in_specs=[pl.BlockSpec((tm, D_MODEL), lambda i: (i, 0)),
                  pl.BlockSpec((tm, half), lambda i: (i, 0)),
                  pl.BlockSpec((tm, half), lambda i: (i, c2)),
                  pl.BlockSpec((half, D_MODEL), lambda i: (0, 0)),
                  pl.BlockSpec((half, D_MODEL), lambda i: (1, 0)),
                  _full((6, D_MODEL)), _full((1, D_MODEL)), _full((1, D_MODEL))],
        out_specs=pl.BlockSpec((tm, D_MODEL), lambda i: (i, 0)),
        out_shape=jax.ShapeDtypeStruct((n, D_MODEL), F32),
        compiler_params=_params(("parallel",)),
        name="out_project_ln",
    )(x, y1, y2, w, w, mod, g, b)


FFN_CHUNKS = 2
FFN_FC = D_FF // FFN_CHUNKS


FFN_CHAINS = 1
FFN_TM = 512


def _ffn_kernel(x_ref, xp_ref, xn_ref, mod_ref, wu_ref, cw_ref, cb_ref, wd_ref, g_ref, b_ref, o_ref,
                h_ref, ua_ref, ug_ref):
    i = pl.program_id(0)
    tm = x_ref.shape[0]
    rows = tm // FFN_CHAINS
    shift = mod_ref[3:4, :]
    scale = 1.0 + mod_ref[4:5, :]
    hp = jnp.where(i > 0, xp_ref[...] * scale + shift, 0.0)
    hn = jnp.where(i < pl.num_programs(0) - 1, xn_ref[...] * scale + shift, 0.0)
    hm = x_ref[...] * scale + shift
    h_ref[...] = jnp.concatenate([hp, hm, hn], axis=0).astype(BF16)

    def conv(u_ref, col0):
        cols = slice(col0, col0 + FFN_FC)
        return (cb_ref[:, cols]
                + u_ref[HALO - 1:HALO - 1 + rows, :] * cw_ref[0:1, cols]
                + u_ref[HALO:HALO + rows, :] * cw_ref[1:2, cols]
                + u_ref[HALO + 1:HALO + 1 + rows, :] * cw_ref[2:3, cols])

    for s in range(FFN_CHAINS):
        r0 = s * rows
        hs = h_ref[r0:r0 + rows + 2 * HALO, :]
        f = None
        for c in range(FFN_CHUNKS):
            a0 = c * FFN_FC
            g0 = D_FF + c * FFN_FC
            ua = ua_ref.at[c]
            ug = ug_ref.at[c]
            ua[...] = jnp.dot(hs, wu_ref[:, a0:a0 + FFN_FC], preferred_element_type=F32)
            ug[...] = jnp.dot(hs, wu_ref[:, g0:g0 + FFN_FC], preferred_element_type=F32)
            gt = conv(ug, g0)
            act = (gt * jax.nn.sigmoid(gt) * conv(ua, a0)).astype(BF16)
            part = jnp.dot(act, wd_ref[a0:a0 + FFN_FC, :], preferred_element_type=F32)
            f = part if f is None else f + part
        o_ref[r0:r0 + rows, :] = _residual_layer_norm(x_ref[r0:r0 + rows, :], f, mod_ref[5:6, :],
                                                      g_ref[...], b_ref[...])


def _conv_ffn(x, mod, w_up, conv_w, conv_b, w_down, g, b, tm):
    n = x.shape[0]
    nh = n // HALO
    per = tm // HALO
    rows = tm // FFN_CHAINS
    resident = pl.Buffered(1)
    return pl.pallas_call(
        _ffn_kernel,
        grid=(n // tm,),
        in_specs=[pl.BlockSpec((tm, D_MODEL), lambda i: (i, 0)),
                  pl.BlockSpec((HALO, D_MODEL), lambda i: (jnp.maximum(i * per - 1, 0), 0)),
                  pl.BlockSpec((HALO, D_MODEL), lambda i: (jnp.minimum((i + 1) * per, nh - 1), 0)),
                  _full((6, D_MODEL)),
                  pl.BlockSpec((D_MODEL, 2 * D_FF), lambda i: (0, 0), pipeline_mode=resident),
                  _full((3, 2 * D_FF)), _full((1, 2 * D_FF)),
                  pl.BlockSpec((D_FF, D_MODEL), lambda i: (0, 0), pipeline_mode=resident),
                  _full((1, D_MODEL)), _full((1, D_MODEL))],
        out_specs=pl.BlockSpec((tm, D_MODEL), lambda i: (i, 0)),
        out_shape=jax.ShapeDtypeStruct((n, D_MODEL), F32),
        scratch_shapes=[pltpu.VMEM((tm + 2 * HALO, D_MODEL), BF16),
                        pltpu.VMEM((FFN_CHUNKS, rows + 2 * HALO, FFN_FC), F32),
                        pltpu.VMEM((FFN_CHUNKS, rows + 2 * HALO, FFN_FC), F32)],
        compiler_params=_params(("parallel",)),
        name="conv_ffn_ln",
    )(x, x, x, mod, w_up, conv_w, conv_b, w_down, g, b)


def _pair_perm(n_heads, kv_heads):
    per = n_heads // kv_heads
    order = []
    for j in range(per):
        for g in range(kv_heads):
            order.append(g * per + j)
    cols = np.concatenate([np.arange(h * HEAD_DIM, (h + 1) * HEAD_DIM) for h in order])
    return cols


_PERM_A = _pair_perm(A_HEADS, A_KV_HEADS)
_PERM_C = _pair_perm(C_HEADS, C_KV_HEADS)


def _rope_tables(n):
    t = np.arange(n)
    row = (t // GRID_W).astype(np.float64)
    col = (t % GRID_W).astype(np.float64)
    half = HEAD_DIM // 2
    inv = ROPE_THETA ** (-np.arange(0, half, 2, dtype=np.float64) / half)
    ang_r = row[:, None] * inv
    ang_c = col[:, None] * inv
    ang = np.concatenate([ang_r, ang_r, ang_c, ang_c], -1)
    ang = np.concatenate([ang, ang], -1)
    sign = np.where(np.arange(LANES) % 32 < 16, -1.0, 1.0)
    return jnp.asarray(np.cos(ang), F32), jnp.asarray(np.sin(ang) * sign, F32)


_EVEN_GROUPS = (
    (0, A_Q, 0, True, QK_SCALE, False),
    (A_Q, A_KV, A_Q, True, 1.0, False),
    (A_Q + A_KV, A_KV, None, False, 1.0, True),
    (A_Q + 2 * A_KV, B_W, None, False, QK_SCALE, False),
    (A_Q + 2 * A_KV + B_W, B_W, None, False, 1.0, False),
    (A_Q + 2 * A_KV + 2 * B_W, B_W, None, False, 1.0, True),
)
_EVEN_GROUPS_CTX = tuple((c0, w, g0, False, s, t) for (c0, w, g0, _, s, t) in _EVEN_GROUPS)
_ODD_GROUPS = (
    (0, C_Q, None, True, QK_SCALE, False),
    (C_Q, C_KV, None, True, 1.0, False),
    (C_Q + C_KV, C_KV, None, False, 1.0, True),
)
_ODD_GROUPS_CTX = tuple((c0, w, g0, False, s, t) for (c0, w, g0, _, s, t) in _ODD_GROUPS)

LAT_TM = 1024


def kernel(x, c, ctx, c_ctx, ada_w, ada_b, ln_g, ln_b, ev_w_in, ev_w_out, ev_q_gain, ev_k_gain, ev_rpb,
           od_w_in, od_w_out, od_sink, ffn_w_up, ffn_conv_w, ffn_conv_b, ffn_w_down):
    n = x.shape[1]
    nc = ctx.shape[1]
    x_lat = x[0]
    x_ctx = ctx[0]

    cvec = jnp.zeros((8, D_MODEL), F32).at[0].set(c[0]).at[1].set(c_ctx)
    mods = _modulation(cvec, ada_w, ada_b)

    cos, sin = _rope_tables(n)
    cos_c, sin_c = cos[:nc], sin[:nc]
    lane = np.arange(LANES)
    gmat = jnp.asarray((lane[:, None] // HEAD_DIM) == (lane[None, :] // HEAD_DIM), BF16)
    no_sink = jnp.full((C_HEADS,), NEG, F32)

    for l in range(DEPTH):
        i = l // 2
        ctx_out = l < DEPTH - 1
        m_lat = mods[l, 0].reshape(6, D_MODEL)
        m_ctx = mods[l, 1].reshape(6, D_MODEL)
        g0, b0 = ln_g[l, 0][None], ln_b[l, 0][None]
        g1, b1 = ln_g[l, 1][None], ln_b[l, 1][None]
        if l % 2 == 0:
            w_in = ev_w_in[i]
            w_in = jnp.concatenate([w_in[:, :A_Q][:, _PERM_A], w_in[:, A_Q:]], axis=1).astype(BF16)
            w_out = ev_w_out[i]
            w_out = jnp.concatenate([w_out[:A_Q][_PERM_A], w_out[A_Q:]], axis=0).astype(BF16)
            gains = jnp.concatenate([jnp.tile(ev_q_gain[i], A_HEADS), jnp.tile(ev_k_gain[i], A_KV_HEADS)])[None]
            qa, ka, va, vat, qb, kb, vb, vbt = _project(x_lat, m_lat, w_in, gains, cos, sin, gmat, _EVEN_GROUPS,
                                                        PROJ_TM)
            qa_c, ka_c, va_c, vat_c, qb_c, kb_c, vb_c, vbt_c = _project(x_ctx, m_ctx, w_in, gains, cos_c, sin_c,
                                                                        gmat, _EVEN_GROUPS_CTX, nc)
            ya = _global_attention(qa, ka, va, vat, ka_c, va_c, vat_c)
            yb = _nbr_attention(qb, kb, vbt, kb_c, vbt_c, _nbr_bias_tiles(ev_rpb[i] * LOG2E))
            x_lat = _out_project(x_lat, ya, yb, 0, w_out, m_lat, g0, b0, LAT_TM)
            if ctx_out:
                ya_c = _ctx_attention(qa_c, ka_c, va_c, no_sink, 0)
                yb_c = _ctx_attention(qb_c, kb_c, vb_c, no_sink, 1)
                x_ctx = _out_project(x_ctx, ya_c, yb_c, 0, w_out, m_ctx, g0, b0, nc)
        else:
            w_in = od_w_in[i]
            w_in = jnp.concatenate([w_in[:, :C_Q][:, _PERM_C], w_in[:, C_Q:]], axis=1).astype(BF16)
            w_out = od_w_out[i][_PERM_C].astype(BF16)
            sink = od_sink[i] * LOG2E
            gains = jnp.ones((1, A_Q + A_KV), F32)
            q, k, _, vt = _project(x_lat, m_lat, w_in, gains, cos, sin, gmat, _ODD_GROUPS, PROJ_TM)
            q_c, k_c, v_c, vt_c = _project(x_ctx, m_ctx, w_in, gains, cos_c, sin_c, gmat, _ODD_GROUPS_CTX, nc)
            y = _window_attention(q, k, vt, k_c, vt_c, sink)
            x_lat = _out_project(x_lat, y, y, 1, w_out, m_lat, g0, b0, LAT_TM)
            if ctx_out:
                y_c = _ctx_attention(q_c, k_c, v_c, sink, 0)
                x_ctx = _out_project(x_ctx, y_c, y_c, 1, w_out, m_ctx, g0, b0, nc)
        w_up = ffn_w_up[l].astype(BF16)
        w_down = ffn_w_down[l].astype(BF16)
        cb = ffn_conv_b[l][None]
        x_lat = _conv_ffn(x_lat, m_lat, w_up, ffn_conv_w[l], cb, w_down, g1, b1, FFN_TM)
        if ctx_out:
            x_ctx = _conv_ffn(x_ctx, m_ctx, w_up, ffn_conv_w[l], cb, w_down, g1, b1, nc)
    return x_lat[None]
```

```python
import functools

import numpy as np
import jax
import jax.numpy as jnp
from jax import lax
from jax.experimental import pallas as pl
from jax.experimental.pallas import tpu as pltpu

D_MODEL = 1024
DEPTH = 4
GRID_W = 64
HEAD_DIM = 64
A_HEADS = 8
A_KV_HEADS = 2
B_HEADS = 8
NB_KH = 8
NB_KW = 16
C_HEADS = 16
C_KV_HEADS = 2
C_WINDOW = 128
D_FF = 2816
ROPE_THETA = 10000.0
LN_EPS = 1e-5
RMS_EPS = 1e-6
NEG = -1e30
BIG = 1e30
DN_ALPHA = (2 * DEPTH) ** 0.25
A_Q = A_HEADS * HEAD_DIM
A_KV = A_KV_HEADS * HEAD_DIM
B_W = B_HEADS * HEAD_DIM
C_Q = C_HEADS * HEAD_DIM
C_KV = C_KV_HEADS * HEAD_DIM
LOG2E = 1.4426950408889634
QK_SCALE = HEAD_DIM ** -0.5 * LOG2E
DEN_HIGH_LOG2 = 100.0
DEN_LOW_LOG2 = 64.0

LANES = 128
SUBLANES = 8
HALO = SUBLANES
VMEM_LIMIT = 52 * 1024 * 1024

BF16 = jnp.bfloat16
F32 = jnp.float32

_NT = (((1,), (1,)), ((), ()))


def _params(sem):
    return pltpu.CompilerParams(dimension_semantics=sem, vmem_limit_bytes=VMEM_LIMIT)


def _full(shape):
    return pl.BlockSpec(shape, lambda *_: (0,) * len(shape))


def _lane_lo(shape):
    return lax.broadcasted_iota(jnp.int32, shape, len(shape) - 1) < HEAD_DIM


def _unit_lane(half):
    return HEAD_DIM if half == 0 else 0


def _unit_lane_variants(x):
    lane = jnp.arange(x.shape[-1]) % LANES
    one = jnp.ones((), x.dtype)
    zero = jnp.zeros((), x.dtype)
    lo = jnp.where(lane < HEAD_DIM, x, jnp.where(lane == HEAD_DIM, one, zero))
    hi = jnp.where(lane >= HEAD_DIM, x, jnp.where(lane == 0, one, zero))
    return jnp.stack([lo, hi])


def _fold_denominators(high, low, den):
    for t in range(den.shape[1] // LANES):
        piece = den[:, t * LANES:(t + 1) * LANES]
        high = jnp.maximum(high, piece)
        low = jnp.minimum(low, piece)
    return high, low


def _store_flags(flag_ref, high, low):
    flag_ref[0:SUBLANES, :] = jnp.broadcast_to(high, (SUBLANES, LANES))
    flag_ref[SUBLANES:2 * SUBLANES, :] = jnp.broadcast_to(low, (SUBLANES, LANES))


def _outside_safe_range(flags):
    f = flags.reshape(-1, 2, SUBLANES, LANES)
    return jnp.logical_not((jnp.max(f[:, 0]) < 2.0 ** DEN_HIGH_LOG2) & (jnp.min(f[:, 1]) > 2.0 ** -DEN_LOW_LOG2))


def _mod_kernel(c_ref, w_ref, b_ref, o_ref):
    s = c_ref[...]
    s = s * jax.nn.sigmoid(s)
    o_ref[0] = jnp.dot(s, w_ref[0], preferred_element_type=F32, precision=lax.Precision.HIGHEST) + b_ref[0]


def _modulation(cvec, ada_w, ada_b):
    nb = 6
    return pl.pallas_call(
        _mod_kernel,
        grid=(DEPTH, nb),
        in_specs=[pl.BlockSpec((8, D_MODEL), lambda l, j: (0, 0)),
                  pl.BlockSpec((1, D_MODEL, D_MODEL), lambda l, j: (l, 0, j)),
                  pl.BlockSpec((1, 1, D_MODEL), lambda l, j: (l, 0, j))],
        out_specs=pl.BlockSpec((1, 8, D_MODEL), lambda l, j: (l, 0, j)),
        out_shape=jax.ShapeDtypeStruct((DEPTH, 8, 6 * D_MODEL), F32),
        compiler_params=_params(("arbitrary", "arbitrary")),
        name="modulation",
    )(cvec, ada_w, ada_b.reshape(DEPTH, 1, 6 * D_MODEL))


def _group_sumsq(z, gmat):
    x2 = z * z
    hi = x2.astype(BF16)
    lo = (x2 - hi.astype(F32)).astype(BF16)
    return (jnp.dot(hi, gmat, preferred_element_type=F32) + jnp.dot(lo, gmat, preferred_element_type=F32))


ROPE_QUARTER = HEAD_DIM // 4
VT_ROWS = 80
PROJ_ROWS = 512
PROJ_TM = 1024


def _proj_kernel(groups, x_ref, mod_ref, w_ref, gain_ref, cos_ref, sin_ref, gmat_ref, *out_refs):
    shift = mod_ref[0:1, :]
    scale = 1.0 + mod_ref[1:2, :]
    gmat = gmat_ref[...]
    rows = min(PROJ_ROWS, x_ref.shape[0])
    first = lax.broadcasted_iota(jnp.int32, (rows, LANES), 1) % (2 * ROPE_QUARTER) < ROPE_QUARTER
    tail_row = lax.broadcasted_iota(jnp.int32, (VT_ROWS - HEAD_DIM, LANES), 0)
    tail = jnp.where(tail_row == 0, 1.0, 0.0).astype(BF16)
    for sub in range(x_ref.shape[0] // rows):
        rs = slice(sub * rows, (sub + 1) * rows)
        h = (x_ref[rs, :] * scale + shift).astype(BF16)
        cos = cos_ref[rs, :]
        sin = sin_ref[rs, :]
        outs = iter(out_refs)
        for (c0, width, g0, rope, qscale, transposed_copy) in groups:
            o_ref = next(outs)
            t_ref = next(outs) if transposed_copy else None
            z = jnp.dot(h, w_ref[:, c0:c0 + width], preferred_element_type=F32)
            for b in range(width // LANES):
                zb = z[:, b * LANES:(b + 1) * LANES]
                if g0 is not None:
                    ms = _group_sumsq(zb, gmat) * (1.0 / HEAD_DIM)
                    zb = zb * lax.rsqrt(ms + RMS_EPS) * gain_ref[:, g0 + b * LANES:g0 + (b + 1) * LANES]
                if rope:
                    rot = jnp.where(first, pltpu.roll(zb, LANES - ROPE_QUARTER, 1), pltpu.roll(zb, ROPE_QUARTER, 1))
                    zb = zb * cos + rot * sin
                if qscale != 1.0:
                    zb = zb * qscale
                o_ref[rs, b * LANES:(b + 1) * LANES] = zb.astype(BF16)
                if transposed_copy:
                    zt = zb.T.astype(BF16)
                    for half in range(2):
                        for t in range(rows // LANES):
                            chunk = sub * (rows // LANES) + t
                            t_ref[b, half, chunk, 0:HEAD_DIM, :] = zt[half * HEAD_DIM:(half + 1) * HEAD_DIM,
                                                                      t * LANES:(t + 1) * LANES]
                            t_ref[b, half, chunk, HEAD_DIM:VT_ROWS, :] = tail


def _project(x, mod, w, gains, cos, sin, gmat, groups, tm):
    n = x.shape[0]
    win = w.shape[1]
    out_shape, out_specs = [], []
    for g in groups:
        out_shape.append(jax.ShapeDtypeStruct((n, g[1]), BF16))
        out_specs.append(pl.BlockSpec((tm, g[1]), lambda i: (i, 0)))
        if g[5]:
            nb = g[1] // LANES
            out_shape.append(jax.ShapeDtypeStruct((nb, 2, n // LANES, VT_ROWS, LANES), BF16))
            out_specs.append(pl.BlockSpec((nb, 2, tm // LANES, VT_ROWS, LANES), lambda i: (0, 0, i, 0, 0)))
    return pl.pallas_call(
        functools.partial(_proj_kernel, groups),
        grid=(n // tm,),
        in_specs=[pl.BlockSpec((tm, D_MODEL), lambda i: (i, 0)),
                  _full((6, D_MODEL)),
                  _full((D_MODEL, win)),
                  _full(gains.shape),
                  pl.BlockSpec((tm, LANES), lambda i: (i, 0)),
                  pl.BlockSpec((tm, LANES), lambda i: (i, 0)),
                  _full((LANES, LANES))],
        out_specs=out_specs,
        out_shape=out_shape,
        compiler_params=_params(("parallel",)),
        name="qkv_project",
    )(x, mod, w, gains, cos, sin, gmat)


GA_TQ = 1024
GA_TK = 2048


GA_KEY_CHUNK = 1024
GA_QUERY_CHUNK = 512


def _vt_window(vt_ref, lead, first_chunk, nchunks):
    return jnp.concatenate([vt_ref[lead + (first_chunk + t,)] for t in range(nchunks)], axis=1)


def _global_attn_kernel(q_ref, kc_ref, vct_ref, k_ref, vt_ref, o_ref, flag_ref, qx_ref, acc_ref):
    kk = pl.program_id(1)
    tq = q_ref.shape[0]
    nblk = q_ref.shape[1] // LANES

    def update(kb_ref, vtb_ref):
        nkeys = kb_ref.shape[0]
        kr = min(GA_KEY_CHUNK, nkeys)
        for j in range(nblk):
            for c0 in range(0, tq, GA_QUERY_CHUNK):
                cs = slice(c0, c0 + GA_QUERY_CHUNK)
                part = [None, None]
                for r0 in range(0, nkeys, kr):
                    st = [lax.dot_general(kb_ref[r0:r0 + kr, :], qx_ref[2 * j + half, cs, :], _NT,
                                          preferred_element_type=F32) for half in range(2)]
                    for half in range(2):
                        pv = jnp.dot(_vt_window(vtb_ref, (half,), r0 // LANES, kr // LANES),
                                     jnp.exp2(st[half]).astype(BF16), preferred_element_type=F32)
                        part[half] = pv if part[half] is None else part[half] + pv
                for half in range(2):
                    acc_ref[2 * j + half, :, cs] += part[half]

    @pl.when(kk == 0)
    def _():
        lo = _lane_lo((tq, LANES))
        for j in range(nblk):
            qj = q_ref[:, j * LANES:(j + 1) * LANES]
            zero = jnp.zeros_like(qj)
            qx_ref[2 * j] = jnp.where(lo, qj, zero)
            qx_ref[2 * j + 1] = jnp.where(lo, zero, qj)
        acc_ref[...] = jnp.zeros(acc_ref.shape, F32)
        update(kc_ref, vct_ref)

    update(k_ref, vt_ref)

    @pl.when(kk == pl.num_programs(1) - 1)
    def _():
        high = jnp.zeros((1, LANES), F32)
        low = jnp.full((1, LANES), BIG, F32)
        for j in range(nblk):
            halves = []
            for half in range(2):
                a = acc_ref[2 * j + half]
                den = a[HEAD_DIM:HEAD_DIM + 1, :]
                high, low = _fold_denominators(high, low, den)
                halves.append(a[0:HEAD_DIM, :] / den)
            o_ref[:, j * LANES:(j + 1) * LANES] = jnp.concatenate(halves, axis=0).T.astype(o_ref.dtype)
        _store_flags(flag_ref, high, low)


def _global_attn_exact_kernel(q_ref, kc_ref, vc_ref, k_ref, v_ref, o_ref, qx_ref, c_ref, acc_ref):
    kk = pl.program_id(1)
    tq = q_ref.shape[0]
    nblk = q_ref.shape[1] // LANES
    nh = 2 * nblk
    lane = lax.broadcasted_iota(jnp.int32, (tq, LANES), 1)

    def moving_reference_update(kx_ref, vx_ref, first):
        for h in range(nh):
            half = h % 2
            unit = _unit_lane(half)
            sp = lax.dot_general(qx_ref[h], kx_ref[half], _NT, preferred_element_type=F32)
            mb = jnp.max(sp, axis=-1, keepdims=True)
            c_old = c_ref[h]
            target = mb if first else jnp.maximum(c_old, c_old + mb)
            qx_ref[h] = jnp.where(lane == unit, -target, qx_ref[h].astype(F32)).astype(BF16)
            c_new = -qx_ref[h][:, unit:unit + 1].astype(F32)
            d = c_new - c_old
            pv = jnp.dot(jnp.exp2(sp - d).astype(BF16), vx_ref[half], preferred_element_type=F32)
            acc_ref[h] = pv if first else acc_ref[h] * jnp.exp2(-d) + pv
            c_ref[h] = c_new

    @pl.when(kk == 0)
    def _():
        lo = _lane_lo((tq, LANES))
        for j in range(nblk):
            qj = q_ref[:, j * LANES:(j + 1) * LANES]
            zero = jnp.zeros_like(qj)
            qx_ref[2 * j] = jnp.where(lo, qj, zero)
            qx_ref[2 * j + 1] = jnp.where(lo, zero, qj)
        c_ref[...] = jnp.zeros(c_ref.shape, F32)
        moving_reference_update(kc_ref, vc_ref, True)

    moving_reference_update(k_ref, v_ref, False)

    @pl.when(kk == pl.num_programs(1) - 1)
    def _():
        lo = _lane_lo((tq, LANES))
        for j in range(nblk):
            a_lo = acc_ref[2 * j]
            a_hi = acc_ref[2 * j + 1]
            o_lo = a_lo / a_lo[:, HEAD_DIM:HEAD_DIM + 1]
            o_hi = a_hi / a_hi[:, 0:1]
            o_ref[:, j * LANES:(j + 1) * LANES] = jnp.where(lo, o_lo, o_hi).astype(o_ref.dtype)


def _global_attention(q, k, v, vt, kc, vc, vct):
    n, qw = q.shape
    nc = kc.shape[0]
    nh = 2 * (qw // LANES)
    nq = n // GA_TQ
    grid = (nq, n // GA_TK)
    q_spec = pl.BlockSpec((GA_TQ, qw), lambda i, j: (i, 0))

    y, flags = pl.pallas_call(
        _global_attn_kernel,
        grid=grid,
        in_specs=[q_spec,
                  _full((nc, LANES)), _full((2, nc // LANES, VT_ROWS, LANES)),
                  pl.BlockSpec((GA_TK, LANES), lambda i, j: (j, 0)),
                  pl.BlockSpec((2, GA_TK // LANES, VT_ROWS, LANES), lambda i, j: (0, j, 0, 0))],
        out_specs=[q_spec, pl.BlockSpec((2 * SUBLANES, LANES), lambda i, j: (i, 0))],
        out_shape=[jax.ShapeDtypeStruct((n, qw), BF16),
                   jax.ShapeDtypeStruct((nq * 2 * SUBLANES, LANES), F32)],
        scratch_shapes=[pltpu.VMEM((nh, GA_TQ, LANES), BF16),
                        pltpu.VMEM((nh, VT_ROWS, GA_TQ), F32)],
        compiler_params=_params(("parallel", "arbitrary")),
        name="global_attention",
    )(q, kc, vct[0], k, vt[0])

    def exact():
        return pl.pallas_call(
            _global_attn_exact_kernel,
            grid=grid,
            in_specs=[q_spec,
                      _full((2, nc, LANES)), _full((2, nc, LANES)),
                      pl.BlockSpec((2, GA_TK, LANES), lambda i, j: (0, j, 0)),
                      pl.BlockSpec((2, GA_TK, LANES), lambda i, j: (0, j, 0))],
            out_specs=q_spec,
            out_shape=jax.ShapeDtypeStruct((n, qw), BF16),
            scratch_shapes=[pltpu.VMEM((nh, GA_TQ, LANES), BF16),
                            pltpu.VMEM((nh, GA_TQ, 1), F32),
                            pltpu.VMEM((nh, GA_TQ, LANES), F32)],
            compiler_params=_params(("parallel", "arbitrary")),
            name="global_attention_exact",
        )(q, _unit_lane_variants(kc), _unit_lane_variants(vc), _unit_lane_variants(k), _unit_lane_variants(v))

    return lax.cond(_outside_safe_range(flags), exact, lambda: y)


NB_QROWS = 8
NB_KROWS = NB_QROWS + NB_KH
NB_MASKED = 2 * NB_KH - 1
NB_TILES = 3 * (NB_MASKED + 1)


def _nbr_bias_tiles(rpb):
    nh = rpb.shape[0]
    col = np.arange(GRID_W)
    cs = np.clip(col - NB_KW // 2, 0, GRID_W - NB_KW)
    col_ok = (col[:, None] >= cs[None, :]) & (col[:, None] < cs[None, :] + NB_KW)
    dcol = col[:, None] - col[None, :] + NB_KW - 1
    sel_col = (dcol[:, :, None] == np.arange(2 * NB_KW - 1)) & col_ok[:, :, None]
    base = jnp.einsum('kqj,hdj->hdkq', jnp.asarray(sel_col, F32), rpb, precision=lax.Precision.HIGHEST)
    base = jnp.where(jnp.asarray(col_ok), base, NEG)
    masked = jnp.full((nh, 1, GRID_W, GRID_W), NEG, F32)
    base = jnp.concatenate([base, masked], axis=1)
    prev = jnp.concatenate([masked, base[:, :-1]], axis=1)
    allmasked = jnp.full_like(base, NEG)
    return jnp.concatenate([jnp.concatenate([base, prev], -1),
                            jnp.concatenate([base, allmasked], -1),
                            jnp.concatenate([allmasked, base], -1)], axis=1)


def _nbr_tile_index(rows):
    nq = rows // NB_QROWS
    idx = np.zeros((3, NB_KROWS, NB_QROWS // 2), np.int64)
    for v, blk in enumerate((0, 1, nq - 1)):
        r0 = blk * NB_QROWS
        k0 = int(np.clip(r0 - NB_KH // 2, 0, rows - NB_KROWS))
        for kr in range(NB_KROWS):
            ka = k0 + kr

            def code(qr):
                qa = r0 + qr
                rs = int(np.clip(qa - NB_KH // 2, 0, rows - NB_KH))
                return ka - qa + NB_KH - 1 if rs <= ka < rs + NB_KH else NB_MASKED

            for qp in range(NB_QROWS // 2):
                left, right = code(2 * qp), code(2 * qp + 1)
                if left < NB_MASKED and right < NB_MASKED:
                    assert right == left - 1
                    idx[v, kr, qp] = left
                elif left < NB_MASKED:
                    idx[v, kr, qp] = (NB_MASKED + 1) + left
                else:
                    idx[v, kr, qp] = 2 * (NB_MASKED + 1) + right
    return tuple(tuple(tuple(int(t) for t in row) for row in var) for var in idx)


def _nbr_attn_kernel(exact, tile_index, q_ref, k_ref, vt_ref, kc_ref, vct_ref, tiles_ref, o_ref, flag_ref,
                     bias_ref):
    i = pl.program_id(1)
    nq = pl.num_programs(1)
    rows = k_ref.shape[0] // GRID_W
    span = NB_KROWS * GRID_W

    def fill_bias(variant):
        for half in range(2):
            for kr in range(NB_KROWS):
                for qp in range(NB_QROWS // 2):
                    bias_ref[half, kr * GRID_W:(kr + 1) * GRID_W, qp * LANES:(qp + 1) * LANES] = (
                        tiles_ref[half, tile_index[variant][kr][qp]])

    for variant, at in enumerate((0, 1, nq - 1)):
        pl.when(i == at)(functools.partial(fill_bias, variant))

    r0 = i * NB_QROWS
    ks = pl.multiple_of(jnp.clip(r0 - NB_KH // 2, 0, rows - NB_KROWS) * GRID_W, 2 * LANES)
    kw = k_ref[pl.ds(ks, span), :]
    kc = kc_ref[...]
    q = q_ref[...]
    tq = q.shape[0]
    lo = _lane_lo(q.shape)
    zero = jnp.zeros_like(q)
    high = jnp.zeros((1, LANES), F32)
    low = jnp.full((1, LANES), BIG, F32)
    halves = []
    logits = []
    for half in range(2):
        qm = jnp.where(lo, q, zero) if half == 0 else jnp.where(lo, zero, q)
        logits.append((lax.dot_general(kw, qm, _NT, preferred_element_type=F32),
                       lax.dot_general(kc, qm, _NT, preferred_element_type=F32)))
    for half in range(2):
        s_nb = logits[half][0] + bias_ref[half]
        s_cx = logits[half][1]
        if exact:
            m = jnp.maximum(jnp.max(s_nb, axis=0, keepdims=True), jnp.max(s_cx, axis=0, keepdims=True))
            s_nb = s_nb - m
            s_cx = s_cx - m
        vtw = _vt_window(vt_ref, (0, half), ks // LANES, span // LANES)
        vtc = _vt_window(vct_ref, (0, half), 0, kc.shape[0] // LANES)
        ot = (jnp.dot(vtw, jnp.exp2(s_nb).astype(BF16), preferred_element_type=F32)
              + jnp.dot(vtc, jnp.exp2(s_cx).astype(BF16), preferred_element_type=F32))
        den = ot[HEAD_DIM:HEAD_DIM + 1, :]
        high, low = _fold_denominators(high, low, den)
        halves.append(ot[0:HEAD_DIM, :] / den)
    o_ref[...] = jnp.concatenate(halves, axis=0).T.astype(o_ref.dtype)
    _store_flags(flag_ref, high, low)


def _nbr_attention(q, k, vt, kc, vct, tiles):
    n, qw = q.shape
    nc = kc.shape[0]
    tq = NB_QROWS * GRID_W
    nq = n // tq
    assert nq >= 3
    npair = qw // LANES
    tile_index = _nbr_tile_index(n // GRID_W)
    operands = (q, k, vt, kc, vct, tiles)

    def run(exact):
        return pl.pallas_call(
            functools.partial(_nbr_attn_kernel, exact, tile_index),
            grid=(npair, nq),
            in_specs=[pl.BlockSpec((tq, LANES), lambda p, i: (i, p)),
                      pl.BlockSpec((n, LANES), lambda p, i: (0, p)),
                      pl.BlockSpec((1, 2, n // LANES, VT_ROWS, LANES), lambda p, i: (p, 0, 0, 0, 0)),
                      pl.BlockSpec((nc, LANES), lambda p, i: (0, p)),
                      pl.BlockSpec((1, 2, nc // LANES, VT_ROWS, LANES), lambda p, i: (p, 0, 0, 0, 0)),
                      pl.BlockSpec((2, NB_TILES, GRID_W, LANES), lambda p, i: (p, 0, 0, 0))],
            out_specs=[pl.BlockSpec((tq, LANES), lambda p, i: (i, p)),
                       pl.BlockSpec((2 * SUBLANES, LANES), lambda p, i: (p * nq + i, 0))],
            out_shape=[jax.ShapeDtypeStruct((n, qw), BF16),
                       jax.ShapeDtypeStruct((npair * nq * 2 * SUBLANES, LANES), F32)],
            scratch_shapes=[pltpu.VMEM((2, NB_KROWS * GRID_W, tq), F32)],
            compiler_params=_params(("parallel", "arbitrary")),
            name="neighbourhood_attention_exact" if exact else "neighbourhood_attention",
        )(*operands)

    y, flags = run(False)
    return lax.cond(_outside_safe_range(flags), lambda: run(True)[0], lambda: y)


WIN_TQ = 256
WIN_SPAN = WIN_TQ + 2 * C_WINDOW
WIN_STEP_ROWS = 2 * WIN_TQ


def _window_attn_kernel(exact, sink_ref, q_ref, k_ref, vt_ref, kc_ref, vct_ref, o_ref, flag_ref):
    i = pl.program_id(0)
    n = k_ref.shape[0]
    tq = WIN_TQ
    nblk = q_ref.shape[1] // LANES
    nchunk = WIN_SPAN // LANES
    kc = kc_ref[...]
    lo = _lane_lo((tq, LANES))
    high = jnp.zeros((1, LANES), F32)
    low = jnp.full((1, LANES), BIG, F32)
    for sub in range(q_ref.shape[0] // tq):
        rs = slice(sub * tq, (sub + 1) * tq)
        q0 = i * q_ref.shape[0] + sub * tq
        ks = pl.multiple_of(jnp.clip(q0 - C_WINDOW, 0, n - WIN_SPAN), C_WINDOW)
        kw = k_ref[pl.ds(ks, WIN_SPAN), :]
        kpos = ks + lax.broadcasted_iota(jnp.int32, (WIN_SPAN, tq), 0)
        qpos = q0 + lax.broadcasted_iota(jnp.int32, (WIN_SPAN, tq), 1)
        valid = jnp.abs(kpos - qpos) <= C_WINDOW
        for j in range(nblk):
            q = q_ref[rs, j * LANES:(j + 1) * LANES]
            zero = jnp.zeros_like(q)
            halves = []
            logits = []
            for half in range(2):
                qm = jnp.where(lo, q, zero) if half == 0 else jnp.where(lo, zero, q)
                logits.append((lax.dot_general(kw, qm, _NT, preferred_element_type=F32),
                               lax.dot_general(kc, qm, _NT, preferred_element_type=F32)))
            for half in range(2):
                sink = jnp.full((1, tq), sink_ref[j + nblk * half], F32)
                s_w = jnp.where(valid, logits[half][0], NEG)
                s_c = logits[half][1]
                if exact:
                    m = jnp.maximum(jnp.maximum(jnp.max(s_w, axis=0, keepdims=True),
                                                jnp.max(s_c, axis=0, keepdims=True)), sink)
                    s_w = s_w - m
                    s_c = s_c - m
                    sink = sink - m
                vtw = _vt_window(vt_ref, (half,), ks // LANES, nchunk)
                vtc = _vt_window(vct_ref, (half,), 0, kc.shape[0] // LANES)
                ot = (jnp.dot(vtw, jnp.exp2(s_w).astype(BF16), preferred_element_type=F32)
                      + jnp.dot(vtc, jnp.exp2(s_c).astype(BF16), preferred_element_type=F32))
                den = ot[HEAD_DIM:HEAD_DIM + 1, :] + jnp.exp2(sink)
                high, low = _fold_denominators(high, low, den)
                halves.append(ot[0:HEAD_DIM, :] / den)
            o_ref[rs, j * LANES:(j + 1) * LANES] = jnp.concatenate(halves, axis=0).T.astype(o_ref.dtype)
    _store_flags(flag_ref, high, low)


def _window_attention(q, k, vt, kc, vct, sink):
    n, qw = q.shape
    nc = kc.shape[0]
    step = min(WIN_STEP_ROWS, n)
    nq = n // step
    operands = (sink, q, k, vt[0], kc, vct[0])

    def run(exact):
        return pl.pallas_call(
            functools.partial(_window_attn_kernel, exact),
            grid=(nq,),
            in_specs=[pl.BlockSpec(memory_space=pltpu.SMEM),
                      pl.BlockSpec((step, qw), lambda i: (i, 0)),
                      _full((n, LANES)), _full((2, n // LANES, VT_ROWS, LANES)),
                      _full((nc, LANES)), _full((2, nc // LANES, VT_ROWS, LANES))],
            out_specs=[pl.BlockSpec((step, qw), lambda i: (i, 0)),
                       pl.BlockSpec((2 * SUBLANES, LANES), lambda i: (i, 0))],
            out_shape=[jax.ShapeDtypeStruct((n, qw), BF16),
                       jax.ShapeDtypeStruct((nq * 2 * SUBLANES, LANES), F32)],
            compiler_params=_params(("parallel",)),
            name="window_attention_exact" if exact else "window_attention",
        )(*operands)

    y, flags = run(False)
    return lax.cond(_outside_safe_range(flags), lambda: run(True)[0], lambda: y)


def _ctx_attn_kernel(sink_ref, q_ref, k_ref, v_ref, o_ref):
    j = pl.program_id(0)
    nblk = pl.num_programs(0)
    q = q_ref[...]
    k = k_ref[...]
    v = v_ref[...]
    lo = _lane_lo(q.shape)
    zero = jnp.zeros_like(q)
    outs = []
    for half in range(2):
        qm = jnp.where(lo, q, zero) if half == 0 else jnp.where(lo, zero, q)
        sink = sink_ref[j + nblk * half]
        s = lax.dot_general(qm, k, _NT, preferred_element_type=F32)
        m = jnp.maximum(jnp.max(s, axis=-1, keepdims=True), sink)
        p = jnp.exp2(s - m)
        den = jnp.sum(p, axis=-1, keepdims=True) + jnp.exp2(sink - m)
        outs.append(jnp.dot(p.astype(BF16), v, preferred_element_type=F32) / den)
    o_ref[...] = jnp.where(lo, outs[0], outs[1]).astype(o_ref.dtype)


def _ctx_attention(q, k, v, sink, kv_per_block):
    c, qw = q.shape
    nblk = qw // LANES
    return pl.pallas_call(
        _ctx_attn_kernel,
        grid=(nblk,),
        in_specs=[pl.BlockSpec(memory_space=pltpu.SMEM),
                  pl.BlockSpec((c, LANES), lambda j: (0, j)),
                  pl.BlockSpec((c, LANES), lambda j: (0, j * kv_per_block)),
                  pl.BlockSpec((c, LANES), lambda j: (0, j * kv_per_block))],
        out_specs=pl.BlockSpec((c, LANES), lambda j: (0, j)),
        out_shape=jax.ShapeDtypeStruct((c, qw), BF16),
        compiler_params=_params(("parallel",)),
        name="context_attention",
    )(sink, q, k, v)


def _residual_layer_norm(x, f, gate, g, b):
    z = x + (gate * (1.0 / DN_ALPHA)) * f
    mu = jnp.mean(z, axis=-1, keepdims=True)
    zc = z - mu
    var = jnp.mean(zc * zc, axis=-1, keepdims=True)
    return zc * lax.rsqrt(var + LN_EPS / (DN_ALPHA * DN_ALPHA)) * g + b


def _outproj_kernel(x_ref, y1_ref, y2_ref, w1_ref, w2_ref, mod_ref, g_ref, b_ref, o_ref):
    rows = x_ref.shape[0] // 2
    for s in range(2):
        rs = slice(s * rows, (s + 1) * rows)
        f = (jnp.dot(y1_ref[rs, :], w1_ref[...], preferred_element_type=F32)
             + jnp.dot(y2_ref[rs, :], w2_ref[...], preferred_element_type=F32))
        o_ref[rs, :] = _residual_layer_norm(x_ref[rs, :], f, mod_ref[2:3, :], g_ref[...], b_ref[...])


def _out_project(x, y1, y2, c2, w, mod, g, b, tm):
    n = x.shape[0]
    half = D_MODEL // 2
    return pl.pallas_call(
        _outproj_kernel,
        grid=(n // tm,),
        in_specs=[pl.BlockSpec((tm, D_MODEL), lambda i: (i, 0)),
                  pl.BlockSpec((tm, half), lambda i: (i, 0)),
                  pl.BlockSpec((tm, half), lambda i: (i, c2)),
                  pl.BlockSpec((half, D_MODEL), lambda i: (0, 0)),
                  pl.BlockSpec((half, D_MODEL), lambda i: (1, 0)),
                  _full((6, D_MODEL)), _full((1, D_MODEL)), _full((1, D_MODEL))],
        out_specs=pl.BlockSpec((tm, D_MODEL), lambda i: (i, 0)),
        out_shape=jax.ShapeDtypeStruct((n, D_MODEL), F32),
        compiler_params=_params(("parallel",)),
        name="out_project_ln",
    )(x, y1, y2, w, w, mod, g, b)


FFN_CHUNKS = 2
FFN_FC = D_FF // FFN_CHUNKS


FFN_CHAINS = 1
FFN_TM = 512


def _ffn_kernel(x_ref, xp_ref, xn_ref, mod_ref, wu_ref, cw_ref, cb_ref, wd_ref, g_ref, b_ref, o_ref,
                h_ref, ua_ref, ug_ref):
    i = pl.program_id(0)
    tm = x_ref.shape[0]
    rows = tm // FFN_CHAINS
    shift = mod_ref[3:4, :]
    scale = 1.0 + mod_ref[4:5, :]
    hp = jnp.where(i > 0, xp_ref[...] * scale + shift, 0.0)
    hn = jnp.where(i < pl.num_programs(0) - 1, xn_ref[...] * scale + shift, 0.0)
    hm = x_ref[...] * scale + shift
    h_ref[...] = jnp.concatenate([hp, hm, hn], axis=0).astype(BF16)

    def conv(u_ref, col0):
        cols = slice(col0, col0 + FFN_FC)
        return (cb_ref[:, cols]
                + u_ref[HALO - 1:HALO - 1 + rows, :] * cw_ref[0:1, cols]
                + u_ref[HALO:HALO + rows, :] * cw_ref[1:2, cols]
                + u_ref[HALO + 1:HALO + 1 + rows, :] * cw_ref[2:3, cols])

    for s in range(FFN_CHAINS):
        r0 = s * rows
        hs = h_ref[r0:r0 + rows + 2 * HALO, :]
        f = None
        for c in range(FFN_CHUNKS):
            a0 = c * FFN_FC
            g0 = D_FF + c * FFN_FC
            ua = ua_ref.at[c]
            ug = ug_ref.at[c]
            ua[...] = jnp.dot(hs, wu_ref[:, a0:a0 + FFN_FC], preferred_element_type=F32)
            ug[...] = jnp.dot(hs, wu_ref[:, g0:g0 + FFN_FC], preferred_element_type=F32)
            gt = conv(ug, g0)
            act = (gt * jax.nn.sigmoid(gt) * conv(ua, a0)).astype(BF16)
            part = jnp.dot(act, wd_ref[a0:a0 + FFN_FC, :], preferred_element_type=F32)
            f = part if f is None else f + part
        o_ref[r0:r0 + rows, :] = _residual_layer_norm(x_ref[r0:r0 + rows, :], f, mod_ref[5:6, :],
                                                      g_ref[...], b_ref[...])


def _conv_ffn(x, mod, w_up, conv_w, conv_b, w_down, g, b, tm):
    n = x.shape[0]
    nh = n // HALO
    per = tm // HALO
    rows = tm // FFN_CHAINS
    resident = pl.Buffered(1)
    return pl.pallas_call(
        _ffn_kernel,
        grid=(n // tm,),
        in_specs=[pl.BlockSpec((tm, D_MODEL), lambda i: (i, 0)),
                  pl.BlockSpec((HALO, D_MODEL), lambda i: (jnp.maximum(i * per - 1, 0), 0)),
                  pl.BlockSpec((HALO, D_MODEL), lambda i: (jnp.minimum((i + 1) * per, nh - 1), 0)),
                  _full((6, D_MODEL)),
                  pl.BlockSpec((D_MODEL, 2 * D_FF), lambda i: (0, 0), pipeline_mode=resident),
                  _full((3, 2 * D_FF)), _full((1, 2 * D_FF)),
                  pl.BlockSpec((D_FF, D_MODEL), lambda i: (0, 0), pipeline_mode=resident),
                  _full((1, D_MODEL)), _full((1, D_MODEL))],
        out_specs=pl.BlockSpec((tm, D_MODEL), lambda i: (i, 0)),
        out_shape=jax.ShapeDtypeStruct((n, D_MODEL), F32),
        scratch_shapes=[pltpu.VMEM((tm + 2 * HALO, D_MODEL), BF16),
                        pltpu.VMEM((FFN_CHUNKS, rows + 2 * HALO, FFN_FC), F32),
                        pltpu.VMEM((FFN_CHUNKS, rows + 2 * HALO, FFN_FC), F32)],
        compiler_params=_params(("parallel",)),
        name="conv_ffn_ln",
    )(x, x, x, mod, w_up, conv_w, conv_b, w_down, g, b)


def _pair_perm(n_heads, kv_heads):
    per = n_heads // kv_heads
    order = []
    for j in range(per):
        for g in range(kv_heads):
            order.append(g * per + j)
    cols = np.concatenate([np.arange(h * HEAD_DIM, (h + 1) * HEAD_DIM) for h in order])
    return cols


_PERM_A = _pair_perm(A_HEADS, A_KV_HEADS)
_PERM_C = _pair_perm(C_HEADS, C_KV_HEADS)


def _rope_tables(n):
    t = np.arange(n)
    row = (t // GRID_W).astype(np.float64)
    col = (t % GRID_W).astype(np.float64)
    half = HEAD_DIM // 2
    inv = ROPE_THETA ** (-np.arange(0, half, 2, dtype=np.float64) / half)
    ang_r = row[:, None] * inv
    ang_c = col[:, None] * inv
    ang = np.concatenate([ang_r, ang_r, ang_c, ang_c], -1)
    ang = np.concatenate([ang, ang], -1)
    sign = np.where(np.arange(LANES) % 32 < 16, -1.0, 1.0)
    return jnp.asarray(np.cos(ang), F32), jnp.asarray(np.sin(ang) * sign, F32)


_EVEN_GROUPS = (
    (0, A_Q, 0, True, QK_SCALE, False),
    (A_Q, A_KV, A_Q, True, 1.0, False),
    (A_Q + A_KV, A_KV, None, False, 1.0, True),
    (A_Q + 2 * A_KV, B_W, None, False, QK_SCALE, False),
    (A_Q + 2 * A_KV + B_W, B_W, None, False, 1.0, False),
    (A_Q + 2 * A_KV + 2 * B_W, B_W, None, False, 1.0, True),
)
_EVEN_GROUPS_CTX = tuple((c0, w, g0, False, s, t) for (c0, w, g0, _, s, t) in _EVEN_GROUPS)
_ODD_GROUPS = (
    (0, C_Q, None, True, QK_SCALE, False),
    (C_Q, C_KV, None, True, 1.0, False),
    (C_Q + C_KV, C_KV, None, False, 1.0, True),
)
_ODD_GROUPS_CTX = tuple((c0, w, g0, False, s, t) for (c0, w, g0, _, s, t) in _ODD_GROUPS)

LAT_TM = 1024


def kernel(x, c, ctx, c_ctx, ada_w, ada_b, ln_g, ln_b, ev_w_in, ev_w_out, ev_q_gain, ev_k_gain, ev_rpb,
           od_w_in, od_w_out, od_sink, ffn_w_up, ffn_conv_w, ffn_conv_b, ffn_w_down):
    n = x.shape[1]
    nc = ctx.shape[1]
    x_lat = x[0]
    x_ctx = ctx[0]

    cvec = jnp.zeros((8, D_MODEL), F32).at[0].set(c[0]).at[1].set(c_ctx)
    mods = _modulation(cvec, ada_w, ada_b)

    cos, sin = _rope_tables(n)
    cos_c, sin_c = cos[:nc], sin[:nc]
    lane = np.arange(LANES)
    gmat = jnp.asarray((lane[:, None] // HEAD_DIM) == (lane[None, :] // HEAD_DIM), BF16)
    no_sink = jnp.full((C_HEADS,), NEG, F32)

    for l in range(DEPTH):
        i = l // 2
        ctx_out = l < DEPTH - 1
        m_lat = mods[l, 0].reshape(6, D_MODEL)
        m_ctx = mods[l, 1].reshape(6, D_MODEL)
        g0, b0 = ln_g[l, 0][None], ln_b[l, 0][None]
        g1, b1 = ln_g[l, 1][None], ln_b[l, 1][None]
        if l % 2 == 0:
            w_in = ev_w_in[i]
            w_in = jnp.concatenate([w_in[:, :A_Q][:, _PERM_A], w_in[:, A_Q:]], axis=1).astype(BF16)
            w_out = ev_w_out[i]
            w_out = jnp.concatenate([w_out[:A_Q][_PERM_A], w_out[A_Q:]], axis=0).astype(BF16)
            gains = jnp.concatenate([jnp.tile(ev_q_gain[i], A_HEADS), jnp.tile(ev_k_gain[i], A_KV_HEADS)])[None]
            qa, ka, va, vat, qb, kb, vb, vbt = _project(x_lat, m_lat, w_in, gains, cos, sin, gmat, _EVEN_GROUPS,
                                                        PROJ_TM)
            qa_c, ka_c, va_c, vat_c, qb_c, kb_c, vb_c, vbt_c = _project(x_ctx, m_ctx, w_in, gains, cos_c, sin_c,
                                                                        gmat, _EVEN_GROUPS_CTX, nc)
            ya = _global_attention(qa, ka, va, vat, ka_c, va_c, vat_c)
            yb = _nbr_attention(qb, kb, vbt, kb_c, vbt_c, _nbr_bias_tiles(ev_rpb[i] * LOG2E))
            x_lat = _out_project(x_lat, ya, yb, 0, w_out, m_lat, g0, b0, LAT_TM)
            if ctx_out:
                ya_c = _ctx_attention(qa_c, ka_c, va_c, no_sink, 0)
                yb_c = _ctx_attention(qb_c, kb_c, vb_c, no_sink, 1)
                x_ctx = _out_project(x_ctx, ya_c, yb_c, 0, w_out, m_ctx, g0, b0, nc)
        else:
            w_in = od_w_in[i]
            w_in = jnp.concatenate([w_in[:, :C_Q][:, _PERM_C], w_in[:, C_Q:]], axis=1).astype(BF16)
            w_out = od_w_out[i][_PERM_C].astype(BF16)
            sink = od_sink[i] * LOG2E
            gains = jnp.ones((1, A_Q + A_KV), F32)
            q, k, _, vt = _project(x_lat, m_lat, w_in, gains, cos, sin, gmat, _ODD_GROUPS, PROJ_TM)
            q_c, k_c, v_c, vt_c = _project(x_ctx, m_ctx, w_in, gains, cos_c, sin_c, gmat, _ODD_GROUPS_CTX, nc)
            y = _window_attention(q, k, vt, k_c, vt_c, sink)
            x_lat = _out_project(x_lat, y, y, 1, w_out, m_lat, g0, b0, LAT_TM)
            if ctx_out:
                y_c = _ctx_attention(q_c, k_c, v_c, sink, 0)
                x_ctx = _out_project(x_ctx, y_c, y_c, 1, w_out, m_ctx, g0, b0, nc)
        w_up = ffn_w_up[l].astype(BF16)
        w_down = ffn_w_down[l].astype(BF16)
        cb = ffn_conv_b[l][None]
        x_lat = _conv_ffn(x_lat, m_lat, w_up, ffn_conv_w[l], cb, w_down, g1, b1, FFN_TM)
        if ctx_out:
            x_ctx = _conv_ffn(x_ctx, m_ctx, w_up, ffn_conv_w[l], cb, w_down, g1, b1, nc)
    return x_lat[None]
```
